```python
import jax, jax.numpy as jnp
from jax import lax
import numpy as np

D_MODEL = 2048
BATCH = 4
SEQ = 4096
DEPTH = 4

MIX_WIDTH = D_MODEL
EPS = 1e-6
BLOCK = 128
NEG = -1e30

SWA_WIDTH = MIX_WIDTH // 2
SWA_HEAD_DIM = 64
SWA_Q_HEADS = SWA_WIDTH // SWA_HEAD_DIM
SWA_KV_HEADS = 2
SWA_GROUP = SWA_Q_HEADS // SWA_KV_HEADS
WINDOW = 128

MLA_WIDTH = MIX_WIDTH - SWA_WIDTH
MLA_V_DIM = 128
MLA_HEADS = MLA_WIDTH // MLA_V_DIM
MLA_NOPE_DIM = 128
MLA_ROPE_DIM = 64
MLA_QK_DIM = MLA_NOPE_DIM + MLA_ROPE_DIM
Q_LORA_RANK = 384
KV_LORA_RANK = 256
ROPE_THETA = 10000.0

A_Q = SWA_Q_HEADS * SWA_HEAD_DIM
A_KV = SWA_KV_HEADS * SWA_HEAD_DIM
A_GATE = SWA_WIDTH
B_GATE = MLA_WIDTH
IN_WIDTH = A_Q + 2 * A_KV + A_GATE + Q_LORA_RANK + KV_LORA_RANK + MLA_ROPE_DIM + B_GATE
SPLIT_POINTS = (
    A_Q,
    A_Q + A_KV,
    A_Q + 2 * A_KV,
    A_Q + 2 * A_KV + A_GATE,
    A_Q + 2 * A_KV + A_GATE + Q_LORA_RANK,
    A_Q + 2 * A_KV + A_GATE + Q_LORA_RANK + KV_LORA_RANK,
    A_Q + 2 * A_KV + A_GATE + Q_LORA_RANK + KV_LORA_RANK + MLA_ROPE_DIM,
)

kernel_name = "hybrid_swa_sink_alibi_mla_gated_trunk"


def rmsnorm(x, g):
    xf = x.astype(jnp.float32)
    y = xf * lax.rsqrt(jnp.mean(xf * xf, axis=-1, keepdims=True) + EPS) * g.astype(jnp.float32)
    return y.astype(x.dtype)


def rope(x, cos, sin):
    half = x.shape[-1] // 2
    x1, x2 = x[..., :half], x[..., half:]
    cos = cos.astype(x.dtype)
    sin = sin.astype(x.dtype)
    return jnp.concatenate([x1 * cos - x2 * sin, x2 * cos + x1 * sin], axis=-1)


def swa_attention(q, k, v, sinks):
    b, s = q.shape[0], q.shape[1]
    nb = s // BLOCK
    qb = q.reshape(b, nb, BLOCK, SWA_KV_HEADS, SWA_GROUP, SWA_HEAD_DIM)
    pad = ((0, 0), (BLOCK, 0), (0, 0), (0, 0))
    kp = jnp.pad(k, pad).reshape(b, nb + 1, BLOCK, SWA_KV_HEADS, SWA_HEAD_DIM)
    vp = jnp.pad(v, pad).reshape(b, nb + 1, BLOCK, SWA_KV_HEADS, SWA_HEAD_DIM)
    kk = jnp.concatenate([kp[:, :-1], kp[:, 1:]], axis=2)
    vv = jnp.concatenate([vp[:, :-1], vp[:, 1:]], axis=2)
    scores = jnp.einsum('bnqhgd,bnkhd->bnhgqk', qb, kk).astype(jnp.float32) * (SWA_HEAD_DIM ** -0.5)
    qi = jnp.arange(BLOCK)[:, None]
    ki = jnp.arange(2 * BLOCK)[None, :]
    delta = BLOCK + qi - ki
    key_pos = (jnp.arange(nb)[:, None] - 1) * BLOCK + jnp.arange(2 * BLOCK)[None, :]
    valid = ((delta >= 0) & (delta < WINDOW))[None] & (key_pos >= 0)[:, None, :]
    slopes = jnp.exp2(-8.0 * jnp.arange(1, SWA_Q_HEADS + 1, dtype=jnp.float32) / SWA_Q_HEADS)
    slopes = slopes.reshape(SWA_KV_HEADS, SWA_GROUP)
    alibi = -slopes[:, :, None, None] * delta.astype(jnp.float32)[None, None]
    scores = jnp.where(valid[None, :, None, None], scores + alibi, NEG)
    sink = sinks.astype(jnp.float32).reshape(SWA_KV_HEADS, SWA_GROUP, 1, 1)
    sink = jnp.broadcast_to(sink, scores.shape[:-1] + (1,))
    probs = jax.nn.softmax(jnp.concatenate([scores, sink], axis=-1), axis=-1)[..., :-1]
    out = jnp.einsum('bnhgqk,bnkhd->bnqhgd', probs.astype(v.dtype), vv)
    return out.reshape(b, s, SWA_WIDTH)


def mla_attention(c_q, c_kv, k_rope, q_a_g, kv_a_g, w_q_b, w_kv_b, cos, sin):
    b, s = c_q.shape[0], c_q.shape[1]
    q = (rmsnorm(c_q, q_a_g) @ w_q_b).reshape(b, s, MLA_HEADS, MLA_QK_DIM)
    q_nope = q[..., :MLA_NOPE_DIM]
    q_rope = rope(q[..., MLA_NOPE_DIM:], cos[:, None, :], sin[:, None, :])
    kv = (rmsnorm(c_kv, kv_a_g) @ w_kv_b).reshape(b, s, MLA_HEADS, MLA_NOPE_DIM + MLA_V_DIM)
    k_nope = kv[..., :MLA_NOPE_DIM]
    v = kv[..., MLA_NOPE_DIM:]
    k_r = rope(k_rope, cos, sin)
    nb = s // BLOCK
    qn_b = q_nope.reshape(b, nb, BLOCK, MLA_HEADS, MLA_NOPE_DIM).transpose(1, 0, 2, 3, 4)
    qr_b = q_rope.reshape(b, nb, BLOCK, MLA_HEADS, MLA_ROPE_DIM).transpose(1, 0, 2, 3, 4)
    key_pos = jnp.arange(s)
    scale = MLA_QK_DIM ** -0.5

    def one_block(args):
        qn, qr, i = args
        sc = (jnp.einsum('bqhd,bkhd->bhqk', qn, k_nope)
              + jnp.einsum('bqhd,bkd->bhqk', qr, k_r)).astype(jnp.float32) * scale
        q_pos = i * BLOCK + jnp.arange(BLOCK)
        mask = key_pos[None, :] <= q_pos[:, None]
        p = jax.nn.softmax(jnp.where(mask, sc, NEG), axis=-1)
        return jnp.einsum('bhqk,bkhd->bqhd', p.astype(v.dtype), v)

    out = lax.map(one_block, (qn_b, qr_b, jnp.arange(nb)))
    return out.transpose(1, 0, 2, 3, 4).reshape(b, s, MLA_WIDTH)


def setup_inputs(seed: int = 0) -> dict:
    key = jax.random.key(seed)
    ks = jax.random.split(key, 11)
    f32 = jnp.float32
    x = jax.random.normal(ks[0], (BATCH, SEQ, D_MODEL), f32)
    attn_norm_g = 1.0 + 0.02 * jax.random.normal(ks[1], (DEPTH, D_MODEL), f32)
    w_in = jax.random.normal(ks[2], (DEPTH, D_MODEL, IN_WIDTH), f32) * D_MODEL ** -0.5
    swa_sinks = 0.5 * jax.random.normal(ks[3], (DEPTH, SWA_Q_HEADS), f32)
    q_a_norm_g = 1.0 + 0.02 * jax.random.normal(ks[4], (DEPTH, Q_LORA_RANK), f32)
    kv_a_norm_g = 1.0 + 0.02 * jax.random.normal(ks[5], (DEPTH, KV_LORA_RANK), f32)
    w_q_b = jax.random.normal(ks[6], (DEPTH, Q_LORA_RANK, MLA_HEADS * MLA_QK_DIM), f32) * Q_LORA_RANK ** -0.5
    w_kv_b = jax.random.normal(ks[7], (DEPTH, KV_LORA_RANK, MLA_HEADS * (MLA_NOPE_DIM + MLA_V_DIM)), f32) * KV_LORA_RANK ** -0.5
    w_out = jax.random.normal(ks[8], (DEPTH, MIX_WIDTH, D_MODEL), f32) * MIX_WIDTH ** -0.5
    final_norm_g = 1.0 + 0.02 * jax.random.normal(ks[9], (D_MODEL,), f32)
    return {"x": x, "attn_norm_g": attn_norm_g, "w_in": w_in, "swa_sinks": swa_sinks,
            "q_a_norm_g": q_a_norm_g, "kv_a_norm_g": kv_a_norm_g, "w_q_b": w_q_b,
            "w_kv_b": w_kv_b, "w_out": w_out, "final_norm_g": final_norm_g}


def reference(x, attn_norm_g, w_in, swa_sinks, q_a_norm_g, kv_a_norm_g, w_q_b, w_kv_b, w_out, final_norm_g):
    b, s = x.shape[0], x.shape[1]
    pos = jnp.arange(s, dtype=jnp.float32)
    inv_freq = ROPE_THETA ** (-jnp.arange(0, MLA_ROPE_DIM, 2, dtype=jnp.float32) / MLA_ROPE_DIM)
    ang = pos[:, None] * inv_freq[None, :]
    cos, sin = jnp.cos(ang), jnp.sin(ang)
    for l in range(DEPTH):
        h = rmsnorm(x, attn_norm_g[l])
        proj = h @ w_in[l]
        qa, ka, va, ga, cq, ckv, kr, gb = jnp.split(proj, SPLIT_POINTS, axis=-1)
        ya = swa_attention(qa.reshape(b, s, SWA_Q_HEADS, SWA_HEAD_DIM),
                           ka.reshape(b, s, SWA_KV_HEADS, SWA_HEAD_DIM),
                           va.reshape(b, s, SWA_KV_HEADS, SWA_HEAD_DIM),
                           swa_sinks[l]) * jax.nn.silu(ga)
        yb = mla_attention(cq, ckv, kr, q_a_norm_g[l], kv_a_norm_g[l], w_q_b[l], w_kv_b[l],
                           cos, sin) * jax.nn.silu(gb)
        x = x + jnp.concatenate([ya, yb], axis=-1) @ w_out[l]
    return rmsnorm(x, final_norm_g)
```

```python
import functools
import math

import jax
import jax.numpy as jnp
from jax import lax
from jax.experimental import pallas as pl
from jax.experimental.pallas import tpu as pltpu

F32 = jnp.float32
BF16 = jnp.bfloat16

D_MODEL = 2048
DEPTH = 4
EPS = 1e-6
BLOCK = 128
WINDOW = 128
NEG = -1e30
LOG2E = math.log2(math.e)

SWA_WIDTH = 1024
SWA_HEAD_DIM = 64
SWA_Q_HEADS = 16
SWA_KV_HEADS = 2
SWA_GROUP = 8

MLA_WIDTH = 1024
MLA_V_DIM = 128
MLA_HEADS = 8
MLA_NOPE_DIM = 128
MLA_ROPE_DIM = 64
MLA_QK_DIM = 192
MLA_QK_PAD = 256
Q_LORA_RANK = 384
KV_LORA_RANK = 256
ROPE_THETA = 10000.0

A_Q = 1024
A_KV = 128
GATE_W = 2048
LAT_W = Q_LORA_RANK + KV_LORA_RANK + 4 * MLA_ROPE_DIM
KV_W = 4 * A_KV
IN_W = A_Q + GATE_W + LAT_W + KV_W

SWA_QSCALE = SWA_HEAD_DIM ** -0.5 * LOG2E
MLA_QSCALE = MLA_QK_DIM ** -0.5 * LOG2E

VMEM_LIMIT = 56 * 1024 * 1024


def _cparams(*sem):
    return pltpu.CompilerParams(dimension_semantics=sem, vmem_limit_bytes=VMEM_LIMIT)


def _resident(shape):
    nd = len(shape)
    return pl.BlockSpec(shape, lambda *_: (0,) * nd, pipeline_mode=pl.Buffered(1))


def _rms(x, g):
    return x * lax.rsqrt(jnp.mean(x * x, axis=-1, keepdims=True) + EPS) * g


_INPROJ_TM = 512
_INPROJ_CHUNKS = (
    (0, 0, 0, 512, SWA_QSCALE), (0, 512, 512, 512, SWA_QSCALE),
    (1, 0, 1024, 512, None), (1, 512, 1536, 512, None),
    (1, 1024, 2048, 512, None), (1, 1536, 2560, 512, None),
    (2, 0, 3072, 512, None), (2, 512, 3584, 384, None),
    (3, 0, 3968, 512, None),
)


def _inproj_kernel(x_ref, g_ref, w_ref, q_ref, gate_ref, lat_ref, kv_ref):
    h = _rms(x_ref[...], g_ref[...]).astype(BF16)
    outs = (q_ref, gate_ref, lat_ref, kv_ref)
    for oi, ooff, woff, width, scale in _INPROJ_CHUNKS:
        r = jnp.dot(h, w_ref[:, woff:woff + width], preferred_element_type=F32)
        if scale is not None:
            r = r * scale
        outs[oi][:, ooff:ooff + width] = r.astype(outs[oi].dtype)


def _inproj(x2d, g, w):
    t = x2d.shape[0]
    tm = _INPROJ_TM
    return pl.pallas_call(
        _inproj_kernel,
        grid=(t // tm,),
        in_specs=[
            pl.BlockSpec((tm, D_MODEL), lambda i: (i, 0)),
            _resident((1, D_MODEL)),
            _resident((D_MODEL, IN_W)),
        ],
        out_specs=[
            pl.BlockSpec((tm, A_Q), lambda i: (i, 0)),
            pl.BlockSpec((tm, GATE_W), lambda i: (i, 0)),
            pl.BlockSpec((tm, LAT_W), lambda i: (i, 0)),
            pl.BlockSpec((tm, KV_W), lambda i: (i, 0)),
        ],
        out_shape=[
            jax.ShapeDtypeStruct((t, A_Q), BF16),
            jax.ShapeDtypeStruct((t, GATE_W), BF16),
            jax.ShapeDtypeStruct((t, LAT_W), F32),
            jax.ShapeDtypeStruct((t, KV_W), BF16),
        ],
        compiler_params=_cparams("parallel"),
        name="inproj",
    )(x2d, g, w)


_MLAP_TM = 1024
_VT_TILE = 256


def _mla_proj_kernel(lat_ref, gq_ref, gkv_ref, wq_ref, wk_ref, wvt_ref, cos_ref, sin_ref,
                     q_ref, k_ref, vt_ref):
    tm = lat_ref.shape[1]
    cqn = _rms(lat_ref[0, :, 0:384], gq_ref[...]).astype(BF16)
    ckvn = _rms(lat_ref[0, :, 384:640], gkv_ref[...]).astype(BF16)
    cos = cos_ref[...]
    sin = sin_ref[...]
    krot = lat_ref[0, :, 640:768] * cos + lat_ref[0, :, 768:896] * sin
    lane = lax.broadcasted_iota(jnp.int32, (tm, 128), 1)
    low = lane < 64
    krot_half = (jnp.where(low, krot, 0.0).astype(BF16), jnp.where(low, 0.0, krot).astype(BF16))

    knope = jnp.dot(ckvn, wk_ref[...], preferred_element_type=F32)
    for h in range(MLA_HEADS):
        k_ref[0, h, :, 0:128] = knope[:, h * 128:(h + 1) * 128].astype(BF16)
        k_ref[0, h, :, 128:256] = krot_half[h % 2]

    vt = lax.dot_general(wvt_ref[...], ckvn, (((1,), (1,)), ((), ())),
                         preferred_element_type=F32)
    for h in range(MLA_HEADS):
        for c in range(tm // _VT_TILE):
            vt_ref[0, h, c] = vt[h * 128:(h + 1) * 128, c * _VT_TILE:(c + 1) * _VT_TILE].astype(BF16)

    qall = jnp.dot(cqn, wq_ref[...], preferred_element_type=F32)
    for p in range(MLA_HEADS // 2):
        qr = (qall[:, 1024 + p * 128:1152 + p * 128] * cos
              + qall[:, 1536 + p * 128:1664 + p * 128] * sin) * MLA_QSCALE
        for par in range(2):
            h = 2 * p + par
            q_ref[0, h, :, 0:128] = (qall[:, h * 128:(h + 1) * 128] * MLA_QSCALE).astype(BF16)
            keep = low if par == 0 else jnp.logical_not(low)
            q_ref[0, h, :, 128:256] = jnp.where(keep, qr, 0.0).astype(BF16)


def _mla_proj(lat, gq, gkv, wq, wk, wvt, cos2, sin2):
    b, s, _ = lat.shape
    tm = _MLAP_TM
    nvt = tm // _VT_TILE
    return pl.pallas_call(
        _mla_proj_kernel,
        grid=(b, s // tm),
        in_specs=[
            pl.BlockSpec((1, tm, LAT_W), lambda bi, i: (bi, i, 0)),
            _resident((1, Q_LORA_RANK)),
            _resident((1, KV_LORA_RANK)),
            _resident(wq.shape),
            _resident(wk.shape),
            _resident(wvt.shape),
            pl.BlockSpec((tm, 128), lambda bi, i: (i, 0)),
            pl.BlockSpec((tm, 128), lambda bi, i: (i, 0)),
        ],
        out_specs=[
            pl.BlockSpec((1, MLA_HEADS, tm, MLA_QK_PAD), lambda bi, i: (bi, 0, i, 0)),
            pl.BlockSpec((1, MLA_HEADS, tm, MLA_QK_PAD), lambda bi, i: (bi, 0, i, 0)),
            pl.BlockSpec((1, MLA_HEADS, nvt, MLA_V_DIM, _VT_TILE), lambda bi, i: (bi, 0, i, 0, 0)),
        ],
        out_shape=[
            jax.ShapeDtypeStruct((b, MLA_HEADS, s, MLA_QK_PAD), BF16),
            jax.ShapeDtypeStruct((b, MLA_HEADS, s, MLA_QK_PAD), BF16),
            jax.ShapeDtypeStruct((b, MLA_HEADS, s // _VT_TILE, MLA_V_DIM, _VT_TILE), BF16),
        ],
        compiler_params=_cparams("parallel", "parallel"),
        name="mla_proj",
    )(lat, gq, gkv, wq, wk, wvt, cos2, sin2)


_MLA_TQ = 512


def _mla_attn_kernel(q_ref, k_ref, vt_ref, o_ref, m_ref, l_ref, acc_ref):
    tq = q_ref.shape[2]
    qi = pl.program_id(2)
    q = q_ref[0, 0]
    m_ref[...] = jnp.full_like(m_ref, NEG)
    l_ref[...] = jnp.zeros_like(l_ref)
    acc_ref[...] = jnp.zeros_like(acc_ref)

    def tile(j, masked):
        kt = k_ref[0, 0, pl.ds(pl.multiple_of(j * tq, tq), tq), :]
        st = lax.dot_general(kt, q, (((1,), (1,)), ((), ())),
                             preferred_element_type=F32)
        if masked:
            kpos = lax.broadcasted_iota(jnp.int32, st.shape, 0)
            qpos = lax.broadcasted_iota(jnp.int32, st.shape, 1)
            st = jnp.where(kpos <= qpos, st, NEG)
        m_old = m_ref[...]
        m_new = jnp.maximum(m_old, jnp.max(st, axis=0, keepdims=True))
        alpha = jnp.exp2(m_old - m_new)
        p = jnp.exp2(st - m_new)
        l_ref[...] = alpha * l_ref[...] + jnp.sum(p, axis=0, keepdims=True)
        m_ref[...] = m_new
        pb = p.astype(BF16)
        nsub = tq // _VT_TILE
        pv = jnp.dot(vt_ref[0, 0, j * nsub], pb[0:_VT_TILE], preferred_element_type=F32)
        for c in range(1, nsub):
            pv += jnp.dot(vt_ref[0, 0, j * nsub + c], pb[c * _VT_TILE:(c + 1) * _VT_TILE],
                          preferred_element_type=F32)
        acc_ref[...] = alpha * acc_ref[...] + pv

    def body(j, carry):
        tile(j, False)
        return carry

    lax.fori_loop(0, qi, body, 0)
    tile(qi, True)
    out = acc_ref[...] / l_ref[...]
    o_ref[0] = out.T.astype(o_ref.dtype)


def _mla_attn(q, k, vt):
    b, h, s, _ = q.shape
    tq = _MLA_TQ
    return pl.pallas_call(
        _mla_attn_kernel,
        grid=(b, h, s // tq),
        in_specs=[
            pl.BlockSpec((1, 1, tq, MLA_QK_PAD), lambda bi, hi, i: (bi, hi, i, 0)),
            pl.BlockSpec((1, 1, s, MLA_QK_PAD), lambda bi, hi, i: (bi, hi, 0, 0)),
            pl.BlockSpec((1, 1, s // _VT_TILE, MLA_V_DIM, _VT_TILE), lambda bi, hi, i: (bi, hi, 0, 0, 0)),
        ],
        out_specs=pl.BlockSpec((1, tq, MLA_V_DIM), lambda bi, hi, i: (bi, i, hi)),
        out_shape=jax.ShapeDtypeStruct((b, s, MLA_WIDTH), BF16),
        scratch_shapes=[
            pltpu.VMEM((1, tq), F32),
            pltpu.VMEM((1, tq), F32),
            pltpu.VMEM((MLA_V_DIM, tq), F32),
        ],
        compiler_params=_cparams("parallel", "parallel", "arbitrary"),
        name="mla_attn",
    )(q, k, vt)


_SWA_TQ = 512
_SWA_SLOPES = tuple(2.0 ** (-8.0 * (i + 1) / SWA_Q_HEADS) for i in range(SWA_Q_HEADS))


def _swa_kernel(sink_ref, q_ref, kvc_ref, kvp_ref, o_ref, kvbuf_ref):
    tq = q_ref.shape[1]
    first_tile = pl.program_id(1) == 0
    kvbuf_ref[0:BLOCK] = kvp_ref[0]
    kvbuf_ref[BLOCK:BLOCK + tq] = kvc_ref[0]

    lane = lax.broadcasted_iota(jnp.int32, (2 * BLOCK, 128), 1)
    low_kv = lane < 64
    qlane = lax.broadcasted_iota(jnp.int32, (BLOCK, 128), 1)
    low_q = qlane < 64
    tpos = lax.broadcasted_iota(jnp.int32, (BLOCK, 2 * BLOCK), 0)
    spos = lax.broadcasted_iota(jnp.int32, (BLOCK, 2 * BLOCK), 1)
    delta = BLOCK + tpos - spos

    def block(blk, carry):
        r0 = pl.multiple_of(blk * BLOCK, BLOCK)
        qblk = q_ref[0, pl.ds(r0, BLOCK), :]
        kvw = kvbuf_ref[pl.ds(r0, 2 * BLOCK), :]
        no_prev = jnp.logical_and(first_tile, blk == 0)
        d = jnp.where(jnp.logical_and(no_prev, spos < BLOCK), -1, delta)
        valid = d.astype(jnp.uint32) < WINDOW
        df = delta.astype(F32)
        for hk in range(SWA_KV_HEADS):
            kpair = kvw[:, hk * 128:(hk + 1) * 128]
            vpair = kvw[:, 256 + hk * 128:256 + (hk + 1) * 128]
            vhalf = (jnp.where(low_kv, vpair, jnp.zeros_like(vpair)),
                     jnp.where(low_kv, jnp.zeros_like(vpair), vpair))
            for p in range(SWA_GROUP // 2):
                col = hk * 512 + p * 128
                qp = qblk[:, col:col + 128]
                out_pair = None
                for par in range(2):
                    head = hk * SWA_GROUP + 2 * p + par
                    keep = low_q if par == 0 else jnp.logical_not(low_q)
                    qz = jnp.where(keep, qp, jnp.zeros_like(qp))
                    s = lax.dot_general(qz, kpair, (((1,), (1,)), ((), ())),
                                        preferred_element_type=F32)
                    s = jnp.where(valid, s - (_SWA_SLOPES[head] * LOG2E) * df, NEG)
                    sink = sink_ref[head]
                    m = jnp.maximum(jnp.max(s, axis=-1, keepdims=True), sink)
                    pr = jnp.exp2(s - m)
                    l = jnp.sum(pr, axis=-1, keepdims=True) + jnp.exp2(sink - m)
                    o = jnp.dot(pr.astype(BF16), vhalf[par], preferred_element_type=F32)
                    o = o * (1.0 / l)
                    out_pair = o if out_pair is None else out_pair + o
                o_ref[0, pl.ds(r0, BLOCK), col:col + 128] = out_pair.astype(o_ref.dtype)
        return carry

    lax.fori_loop(0, tq // BLOCK, block, 0)


def _swa_attn(sinks2, qa, kv):
    b, s, _ = qa.shape
    tq = _SWA_TQ
    nb = tq // BLOCK
    return pl.pallas_call(
        _swa_kernel,
        grid=(b, s // tq),
        in_specs=[
            pl.BlockSpec(memory_space=pltpu.SMEM),
            pl.BlockSpec((1, tq, A_Q), lambda bi, i: (bi, i, 0)),
            pl.BlockSpec((1, tq, KV_W), lambda bi, i: (bi, i, 0)),
            pl.BlockSpec((1, BLOCK, KV_W), lambda bi, i: (bi, jnp.maximum(i * nb - 1, 0), 0)),
        ],
        out_specs=pl.BlockSpec((1, tq, SWA_WIDTH), lambda bi, i: (bi, i, 0)),
        out_shape=jax.ShapeDtypeStruct((b, s, SWA_WIDTH), BF16),
        scratch_shapes=[pltpu.VMEM((tq + BLOCK, KV_W), BF16)],
        compiler_params=_cparams("parallel", "parallel"),
        name="swa_attn",
    )(sinks2, qa, kv, kv)


_OUTPROJ_TM = 512
_OUTPROJ_TN = 512


def _outproj_kernel(ya_ref, yb_ref, gate_ref, x_ref, w_ref, o_ref):
    g = gate_ref[...].astype(F32)
    y = jnp.concatenate([ya_ref[...], yb_ref[...]], axis=-1).astype(F32)
    gated = (y * (g / (1.0 + jnp.exp(-g)))).astype(BF16)
    for c in range(D_MODEL // _OUTPROJ_TN):
        sl = slice(c * _OUTPROJ_TN, (c + 1) * _OUTPROJ_TN)
        o_ref[:, sl] = x_ref[:, sl] + jnp.dot(gated, w_ref[:, sl], preferred_element_type=F32)


def _outproj(ya, yb, gate, x2d, w):
    t = x2d.shape[0]
    tm = _OUTPROJ_TM
    return pl.pallas_call(
        _outproj_kernel,
        grid=(t // tm,),
        in_specs=[
            pl.BlockSpec((tm, SWA_WIDTH), lambda i: (i, 0)),
            pl.BlockSpec((tm, MLA_WIDTH), lambda i: (i, 0)),
            pl.BlockSpec((tm, GATE_W), lambda i: (i, 0)),
            pl.BlockSpec((tm, D_MODEL), lambda i: (i, 0)),
            _resident((D_MODEL, D_MODEL)),
        ],
        out_specs=pl.BlockSpec((tm, D_MODEL), lambda i: (i, 0)),
        out_shape=jax.ShapeDtypeStruct((t, D_MODEL), F32),
        compiler_params=_cparams("parallel"),
        name="outproj",
    )(ya, yb, gate, x2d, w)


_NORM_TM = 1024


def _final_norm_kernel(x_ref, g_ref, o_ref):
    o_ref[...] = _rms(x_ref[...], g_ref[...])


def _final_norm(x2d, g):
    t = x2d.shape[0]
    tm = _NORM_TM
    return pl.pallas_call(
        _final_norm_kernel,
        grid=(t // tm,),
        in_specs=[pl.BlockSpec((tm, D_MODEL), lambda i: (i, 0)), _resident((1, D_MODEL))],
        out_specs=pl.BlockSpec((tm, D_MODEL), lambda i: (i, 0)),
        out_shape=jax.ShapeDtypeStruct((t, D_MODEL), F32),
        compiler_params=_cparams("parallel"),
        name="final_norm",
    )(x2d, g)


def _swap_halves(w):
    half = w.shape[-1] // 2
    return jnp.concatenate([w[..., half:], w[..., :half]], axis=-1)


def _prep_w_in(w_in):
    o = 0
    qa = w_in[..., o:o + 1024]; o += 1024
    ka = w_in[..., o:o + 128]; o += 128
    va = w_in[..., o:o + 128]; o += 128
    ga = w_in[..., o:o + 1024]; o += 1024
    cq = w_in[..., o:o + 384]; o += 384
    ckv = w_in[..., o:o + 256]; o += 256
    kr = w_in[..., o:o + 64]; o += 64
    gb = w_in[..., o:o + 1024]
    krsw = _swap_halves(kr)
    k0, k1 = ka[..., :64], ka[..., 64:]
    v0, v1 = va[..., :64], va[..., 64:]
    cols = [qa, ga, gb, cq, ckv, kr, kr, krsw, krsw, k0, k0, k1, k1, v0, v0, v1, v1]
    return jnp.concatenate(cols, axis=-1).astype(BF16)


def _prep_w_q(w_q_b):
    w = w_q_b.reshape(DEPTH, Q_LORA_RANK, MLA_HEADS, MLA_QK_DIM)
    nope = w[..., :MLA_NOPE_DIM].reshape(DEPTH, Q_LORA_RANK, MLA_HEADS * MLA_NOPE_DIM)
    rope = w[..., MLA_NOPE_DIM:]
    rope_sw = _swap_halves(rope).reshape(DEPTH, Q_LORA_RANK, MLA_HEADS * MLA_ROPE_DIM)
    rope = rope.reshape(DEPTH, Q_LORA_RANK, MLA_HEADS * MLA_ROPE_DIM)
    return jnp.concatenate([nope, rope, rope_sw], axis=-1).astype(BF16)


def _prep_w_kv(w_kv_b):
    w = w_kv_b.reshape(DEPTH, KV_LORA_RANK, MLA_HEADS, MLA_NOPE_DIM + MLA_V_DIM)
    wk = w[..., :MLA_NOPE_DIM].reshape(DEPTH, KV_LORA_RANK, MLA_HEADS * MLA_NOPE_DIM)
    wv = w[..., MLA_NOPE_DIM:].reshape(DEPTH, KV_LORA_RANK, MLA_HEADS * MLA_V_DIM)
    return wk.astype(BF16), jnp.swapaxes(wv, 1, 2).astype(BF16)


def _rope_tables(s):
    pos = jnp.arange(s, dtype=F32)
    inv_freq = ROPE_THETA ** (-jnp.arange(0, MLA_ROPE_DIM, 2, dtype=F32) / MLA_ROPE_DIM)
    ang = pos[:, None] * inv_freq[None, :]
    cos, sin = jnp.cos(ang), jnp.sin(ang)
    cos2 = jnp.tile(jnp.concatenate([cos, cos], axis=-1), (1, 2))
    sin2 = jnp.tile(jnp.concatenate([-sin, sin], axis=-1), (1, 2))
    return cos2, sin2


def kernel(x, attn_norm_g, w_in, swa_sinks, q_a_norm_g, kv_a_norm_g, w_q_b, w_kv_b, w_out, final_norm_g):
    b, s, d = x.shape
    t = b * s
    w_in_p = _prep_w_in(w_in)
    w_q_p = _prep_w_q(w_q_b)
    w_k_p, w_vt_p = _prep_w_kv(w_kv_b)
    w_out_p = w_out.astype(BF16)
    cos2, sin2 = _rope_tables(s)
    sinks2 = swa_sinks.astype(F32) * LOG2E

    x2d = x.reshape(t, d)
    for l in range(DEPTH):
        qa, gate, lat, kv = _inproj(x2d, attn_norm_g[l][None], w_in_p[l])
        q, k, vt = _mla_proj(lat.reshape(b, s, LAT_W), q_a_norm_g[l][None], kv_a_norm_g[l][None],
                             w_q_p[l], w_k_p[l], w_vt_p[l], cos2, sin2)
        ya = _swa_attn(sinks2[l], qa.reshape(b, s, A_Q), kv.reshape(b, s, KV_W))
        yb = _mla_attn(q, k, vt)
        x2d = _outproj(ya.reshape(t, SWA_WIDTH), yb.reshape(t, MLA_WIDTH), gate, x2d, w_out_p[l])
    return _final_norm(x2d, final_norm_g[None]).reshape(b, s, d)
```

```python
import functools
import math

import jax
import jax.numpy as jnp
from jax import lax
from jax.experimental import pallas as pl
from jax.experimental.pallas import tpu as pltpu

F32 = jnp.float32
BF16 = jnp.bfloat16

D_MODEL = 2048
DEPTH = 4
EPS = 1e-6
BLOCK = 128
WINDOW = 128
NEG = -1e30
LOG2E = math.log2(math.e)

SWA_WIDTH = 1024
SWA_HEAD_DIM = 64
SWA_Q_HEADS = 16
SWA_KV_HEADS = 2
SWA_GROUP = 8

MLA_WIDTH = 1024
MLA_V_DIM = 128
MLA_HEADS = 8
MLA_NOPE_DIM = 128
MLA_ROPE_DIM = 64
MLA_QK_DIM = 192
MLA_QK_PAD = 256
Q_LORA_RANK = 384
KV_LORA_RANK = 256
ROPE_THETA = 10000.0

A_Q = 1024
A_KV = 128
GATE_W = 2048
LAT_W = Q_LORA_RANK + KV_LORA_RANK + 4 * MLA_ROPE_DIM
KV_W = 4 * A_KV
IN_W = A_Q + GATE_W + LAT_W + KV_W

SWA_QSCALE = SWA_HEAD_DIM ** -0.5 * LOG2E
MLA_QSCALE = MLA_QK_DIM ** -0.5 * LOG2E

VMEM_LIMIT = 56 * 1024 * 1024


def _cparams(*sem, flags=None):
    return pltpu.CompilerParams(dimension_semantics=sem, vmem_limit_bytes=VMEM_LIMIT, flags=flags)


def _resident(shape):
    nd = len(shape)
    return pl.BlockSpec(shape, lambda *_: (0,) * nd, pipeline_mode=pl.Buffered(1))


def _rms(x, g):
    return x * lax.rsqrt(jnp.mean(x * x, axis=-1, keepdims=True) + EPS) * g


_INPROJ_TM = 512
_INPROJ_CHUNKS = (
    (0, 0, 0, 512, SWA_QSCALE), (0, 512, 512, 512, SWA_QSCALE),
    (1, 0, 1024, 512, None), (1, 512, 1536, 512, None),
    (1, 1024, 2048, 512, None), (1, 1536, 2560, 512, None),
    (2, 0, 3072, 512, None), (2, 512, 3584, 384, None),
    (3, 0, 3968, 512, None),
)


def _inproj_kernel(x_ref, g_ref, w_ref, q_ref, gate_ref, lat_ref, kv_ref):
    h = _rms(x_ref[...], g_ref[...]).astype(BF16)
    outs = (q_ref, gate_ref, lat_ref, kv_ref)
    for oi, ooff, woff, width, scale in _INPROJ_CHUNKS:
        r = jnp.dot(h, w_ref[:, woff:woff + width], preferred_element_type=F32)
        if scale is not None:
            r = r * scale
        outs[oi][:, ooff:ooff + width] = r.astype(outs[oi].dtype)


def _inproj(x2d, g, w):
    t = x2d.shape[0]
    tm = _INPROJ_TM
    return pl.pallas_call(
        _inproj_kernel,
        grid=(t // tm,),
        in_specs=[
            pl.BlockSpec((tm, D_MODEL), lambda i: (i, 0)),
            _resident((1, D_MODEL)),
            _resident((D_MODEL, IN_W)),
        ],
        out_specs=[
            pl.BlockSpec((tm, A_Q), lambda i: (i, 0)),
            pl.BlockSpec((tm, GATE_W), lambda i: (i, 0)),
            pl.BlockSpec((tm, LAT_W), lambda i: (i, 0)),
            pl.BlockSpec((tm, KV_W), lambda i: (i, 0)),
        ],
        out_shape=[
            jax.ShapeDtypeStruct((t, A_Q), BF16),
            jax.ShapeDtypeStruct((t, GATE_W), BF16),
            jax.ShapeDtypeStruct((t, LAT_W), F32),
            jax.ShapeDtypeStruct((t, KV_W), BF16),
        ],
        compiler_params=_cparams("parallel"),
        name="inproj",
    )(x2d, g, w)


_MLAP_TM = 1024
_VT_TILE = 256


def _mla_proj_kernel(lat_ref, gq_ref, gkv_ref, wqt_ref, wk_ref, wvt_ref, cos_ref, sin_ref,
                     cost_ref, sint_ref, qt_ref, k_ref, vt_ref):
    tm = lat_ref.shape[1]
    cqn = _rms(lat_ref[0, :, 0:384], gq_ref[...]).astype(BF16)
    ckvn = _rms(lat_ref[0, :, 384:640], gkv_ref[...]).astype(BF16)
    krot = lat_ref[0, :, 640:768] * cos_ref[...] + lat_ref[0, :, 768:896] * sin_ref[...]
    lane = lax.broadcasted_iota(jnp.int32, (tm, 128), 1)
    krot = jnp.where(lane < 64, krot, 0.0).astype(BF16)

    knope = jnp.dot(ckvn, wk_ref[...], preferred_element_type=F32)
    for h in range(MLA_HEADS):
        k_ref[0, h, :, 0:128] = knope[:, h * 128:(h + 1) * 128].astype(BF16)
        k_ref[0, h, :, 128:256] = krot

    nt = (((1,), (1,)), ((), ()))
    vt = lax.dot_general(wvt_ref[...], ckvn, nt, preferred_element_type=F32)
    for h in range(MLA_HEADS):
        for c in range(tm // _VT_TILE):
            vt_ref[0, h, c] = vt[h * 128:(h + 1) * 128, c * _VT_TILE:(c + 1) * _VT_TILE].astype(BF16)

    qt = lax.dot_general(wqt_ref[...], cqn, nt, preferred_element_type=F32)
    cost = cost_ref[...]
    sint = sint_ref[...]
    for h in range(MLA_HEADS):
        qt_ref[0, h, 0:128, :] = (qt[h * 128:(h + 1) * 128] * MLA_QSCALE).astype(BF16)
        r0 = 1024 + h * MLA_ROPE_DIM
        qr = (qt[r0:r0 + MLA_ROPE_DIM] * cost + qt[r0 + 512:r0 + 512 + MLA_ROPE_DIM] * sint) * MLA_QSCALE
        qt_ref[0, h, 128:192, :] = qr.astype(BF16)
        qt_ref[0, h, 192:256, :] = jnp.zeros((MLA_QK_PAD - MLA_QK_DIM, tm), BF16)


def _mla_proj(lat, gq, gkv, wqt, wk, wvt, cos2, sin2, cost, sint):
    b, s, _ = lat.shape
    tm = _MLAP_TM
    nvt = tm // _VT_TILE
    return pl.pallas_call(
        _mla_proj_kernel,
        grid=(b, s // tm),
        in_specs=[
            pl.BlockSpec((1, tm, LAT_W), lambda bi, i: (bi, i, 0)),
            _resident((1, Q_LORA_RANK)),
            _resident((1, KV_LORA_RANK)),
            _resident(wqt.shape),
            _resident(wk.shape),
            _resident(wvt.shape),
            pl.BlockSpec((tm, 128), lambda bi, i: (i, 0)),
            pl.BlockSpec((tm, 128), lambda bi, i: (i, 0)),
            pl.BlockSpec((MLA_ROPE_DIM, tm), lambda bi, i: (0, i)),
            pl.BlockSpec((MLA_ROPE_DIM, tm), lambda bi, i: (0, i)),
        ],
        out_specs=[
            pl.BlockSpec((1, MLA_HEADS, MLA_QK_PAD, tm), lambda bi, i: (bi, 0, 0, i)),
            pl.BlockSpec((1, MLA_HEADS, tm, MLA_QK_PAD), lambda bi, i: (bi, 0, i, 0)),
            pl.BlockSpec((1, MLA_HEADS, nvt, MLA_V_DIM, _VT_TILE), lambda bi, i: (bi, 0, i, 0, 0)),
        ],
        out_shape=[
            jax.ShapeDtypeStruct((b, MLA_HEADS, MLA_QK_PAD, s), BF16),
            jax.ShapeDtypeStruct((b, MLA_HEADS, s, MLA_QK_PAD), BF16),
            jax.ShapeDtypeStruct((b, MLA_HEADS, s // _VT_TILE, MLA_V_DIM, _VT_TILE), BF16),
        ],
        compiler_params=_cparams("parallel", "parallel"),
        name="mla_proj",
    )(lat, gq, gkv, wqt, wk, wvt, cos2, sin2, cost, sint)


_MLA_TQ = 512


_MLA_HP = 4


def _mla_attn_kernel(qt_ref, k_ref, vt_ref, o_ref, m_ref, l_ref, acc_ref, st_ref, mx_ref):
    tq = qt_ref.shape[3]
    qi = pl.program_id(2)
    m_ref[...] = jnp.full_like(m_ref, NEG)
    l_ref[...] = jnp.zeros_like(l_ref)
    acc_ref[...] = jnp.zeros_like(acc_ref)
    nsub = tq // _VT_TILE

    def scores(hh, j, slot, masked):
        kt = k_ref[0, hh, pl.ds(pl.multiple_of(j * tq, tq), tq), :]
        st = jnp.dot(kt, qt_ref[0, hh], preferred_element_type=F32)
        if masked:
            kpos = lax.broadcasted_iota(jnp.int32, st.shape, 0)
            qpos = lax.broadcasted_iota(jnp.int32, st.shape, 1)
            st = jnp.where(kpos <= qpos, st, NEG)
        st_ref[slot, hh] = st
        mx_ref[slot, hh] = jnp.max(st, axis=0, keepdims=True)

    def update(hh, j, slot):
        m_old = m_ref[hh]
        m_new = jnp.maximum(m_old, mx_ref[slot, hh])
        alpha = jnp.exp2(m_old - m_new)
        p = jnp.exp2(st_ref[slot, hh] - m_new)
        l_ref[hh] = alpha * l_ref[hh] + jnp.sum(p, axis=0, keepdims=True)
        m_ref[hh] = m_new
        pb = p.astype(BF16)
        pv = jnp.dot(vt_ref[0, hh, j * nsub], pb[0:_VT_TILE], preferred_element_type=F32)
        for c in range(1, nsub):
            pv += jnp.dot(vt_ref[0, hh, j * nsub + c], pb[c * _VT_TILE:(c + 1) * _VT_TILE],
                          preferred_element_type=F32)
        acc_ref[hh] = alpha * acc_ref[hh] + pv

    def stage(s, slot, masked):
        for hh in range(_MLA_HP):
            scores(hh, s + 1, 1 - slot, masked)
            update(hh, s, slot)

    def finish(slot):
        for hh in range(_MLA_HP):
            update(hh, qi, slot)
        for hh in range(_MLA_HP):
            out = acc_ref[hh] / l_ref[hh]
            o_ref[0, :, hh * MLA_V_DIM:(hh + 1) * MLA_V_DIM] = out.T.astype(o_ref.dtype)

    @pl.when(qi == 0)
    def _():
        for hh in range(_MLA_HP):
            scores(hh, 0, 0, True)
        finish(0)

    @pl.when(qi > 0)
    def _():
        for hh in range(_MLA_HP):
            scores(hh, 0, 0, False)
        npairs = (qi - 1) // 2

        def body(t, carry):
            stage(2 * t, 0, False)
            stage(2 * t + 1, 1, False)
            return carry

        lax.fori_loop(0, npairs, body, 0)

        @pl.when(qi % 2 == 1)
        def _():
            stage(qi - 1, 0, True)
            finish(1)

        @pl.when(qi % 2 == 0)
        def _():
            stage(qi - 2, 0, False)
            stage(qi - 1, 1, True)
            finish(0)


def _mla_attn(qt, k, vt):
    b, h, _, s = qt.shape
    tq = _MLA_TQ
    hp = _MLA_HP
    return pl.pallas_call(
        _mla_attn_kernel,
        grid=(b, h // hp, s // tq),
        in_specs=[
            pl.BlockSpec((1, hp, MLA_QK_PAD, tq), lambda bi, hi, i: (bi, hi, 0, i)),
            pl.BlockSpec((1, hp, s, MLA_QK_PAD), lambda bi, hi, i: (bi, hi, 0, 0)),
            pl.BlockSpec((1, hp, s // _VT_TILE, MLA_V_DIM, _VT_TILE), lambda bi, hi, i: (bi, hi, 0, 0, 0)),
        ],
        out_specs=pl.BlockSpec((1, tq, hp * MLA_V_DIM), lambda bi, hi, i: (bi, i, hi)),
        out_shape=jax.ShapeDtypeStruct((b, s, MLA_WIDTH), BF16),
        scratch_shapes=[
            pltpu.VMEM((hp, 1, tq), F32),
            pltpu.VMEM((hp, 1, tq), F32),
            pltpu.VMEM((hp, MLA_V_DIM, tq), F32),
            pltpu.VMEM((2, hp, tq, tq), F32),
            pltpu.VMEM((2, hp, 1, tq), F32),
        ],
        compiler_params=_cparams("parallel", "parallel", "arbitrary"),
        name="mla_attn",
    )(qt, k, vt)


_SWA_TQ = 512
_SWA_SLOPES = tuple(2.0 ** (-8.0 * (i + 1) / SWA_Q_HEADS) for i in range(SWA_Q_HEADS))


def _swa_kernel(sink_ref, q_ref, kvc_ref, kvp_ref, o_ref, kvbuf_ref):
    tq = q_ref.shape[1]
    first_tile = pl.program_id(1) == 0
    kvbuf_ref[0:BLOCK] = kvp_ref[0]
    kvbuf_ref[BLOCK:BLOCK + tq] = kvc_ref[0]

    lane = lax.broadcasted_iota(jnp.int32, (2 * BLOCK, 128), 1)
    low_kv = lane < 64
    qlane = lax.broadcasted_iota(jnp.int32, (BLOCK, 128), 1)
    low_q = qlane < 64
    tpos = lax.broadcasted_iota(jnp.int32, (BLOCK, 2 * BLOCK), 0)
    spos = lax.broadcasted_iota(jnp.int32, (BLOCK, 2 * BLOCK), 1)
    delta = BLOCK + tpos - spos

    def block(blk, carry):
        r0 = pl.multiple_of(blk * BLOCK, BLOCK)
        qblk = q_ref[0, pl.ds(r0, BLOCK), :]
        kvw = kvbuf_ref[pl.ds(r0, 2 * BLOCK), :]
        no_prev = jnp.logical_and(first_tile, blk == 0)
        d = jnp.where(jnp.logical_and(no_prev, spos < BLOCK), -1, delta)
        valid = d.astype(jnp.uint32) < WINDOW
        df = delta.astype(F32)
        for hk in range(SWA_KV_HEADS):
            kpair = kvw[:, hk * 128:(hk + 1) * 128]
            vpair = kvw[:, 256 + hk * 128:256 + (hk + 1) * 128]
            vhalf = (jnp.where(low_kv, vpair, jnp.zeros_like(vpair)),
                     jnp.where(low_kv, jnp.zeros_like(vpair), vpair))
            for p in range(SWA_GROUP // 2):
                col = hk * 512 + p * 128
                qp = qblk[:, col:col + 128]
                out_pair = None
                for par in range(2):
                    head = hk * SWA_GROUP + 2 * p + par
                    keep = low_q if par == 0 else jnp.logical_not(low_q)
                    qz = jnp.where(keep, qp, jnp.zeros_like(qp))
                    s = lax.dot_general(qz, kpair, (((1,), (1,)), ((), ())),
                                        preferred_element_type=F32)
                    s = jnp.where(valid, s - (_SWA_SLOPES[head] * LOG2E) * df, NEG)
                    sink = sink_ref[head]
                    m = jnp.maximum(jnp.max(s, axis=-1, keepdims=True), sink)
                    pr = jnp.exp2(s - m)
                    l = jnp.sum(pr, axis=-1, keepdims=True) + jnp.exp2(sink - m)
                    o = jnp.dot(pr.astype(BF16), vhalf[par], preferred_element_type=F32)
                    o = o * (1.0 / l)
                    out_pair = o if out_pair is None else out_pair + o
                o_ref[0, pl.ds(r0, BLOCK), col:col + 128] = out_pair.astype(o_ref.dtype)
        return carry

    lax.fori_loop(0, tq // BLOCK, block, 0)


def _swa_attn(sinks2, qa, kv):
    b, s, _ = qa.shape
    tq = _SWA_TQ
    nb = tq // BLOCK
    return pl.pallas_call(
        _swa_kernel,
        grid=(b, s // tq),
        in_specs=[
            pl.BlockSpec(memory_space=pltpu.SMEM),
            pl.BlockSpec((1, tq, A_Q), lambda bi, i: (bi, i, 0)),
            pl.BlockSpec((1, tq, KV_W), lambda bi, i: (bi, i, 0)),
            pl.BlockSpec((1, BLOCK, KV_W), lambda bi, i: (bi, jnp.maximum(i * nb - 1, 0), 0)),
        ],
        out_specs=pl.BlockSpec((1, tq, SWA_WIDTH), lambda bi, i: (bi, i, 0)),
        out_shape=jax.ShapeDtypeStruct((b, s, SWA_WIDTH), BF16),
        scratch_shapes=[pltpu.VMEM((tq + BLOCK, KV_W), BF16)],
        compiler_params=_cparams("parallel", "parallel"),
        name="swa_attn",
    )(sinks2, qa, kv, kv)


_OUTPROJ_TM = 512
_OUTPROJ_TN = 512


def _outproj_kernel(ya_ref, yb_ref, gate_ref, x_ref, w_ref, o_ref):
    g = gate_ref[...].astype(F32)
    y = jnp.concatenate([ya_ref[...], yb_ref[...]], axis=-1).astype(F32)
    gated = (y * (g / (1.0 + jnp.exp(-g)))).astype(BF16)
    for c in range(D_MODEL // _OUTPROJ_TN):
        sl = slice(c * _OUTPROJ_TN, (c + 1) * _OUTPROJ_TN)
        o_ref[:, sl] = x_ref[:, sl] + jnp.dot(gated, w_ref[:, sl], preferred_element_type=F32)


def _outproj(ya, yb, gate, x2d, w):
    t = x2d.shape[0]
    tm = _OUTPROJ_TM
    return pl.pallas_call(
        _outproj_kernel,
        grid=(t // tm,),
        in_specs=[
            pl.BlockSpec((tm, SWA_WIDTH), lambda i: (i, 0)),
            pl.BlockSpec((tm, MLA_WIDTH), lambda i: (i, 0)),
            pl.BlockSpec((tm, GATE_W), lambda i: (i, 0)),
            pl.BlockSpec((tm, D_MODEL), lambda i: (i, 0)),
            _resident((D_MODEL, D_MODEL)),
        ],
        out_specs=pl.BlockSpec((tm, D_MODEL), lambda i: (i, 0)),
        out_shape=jax.ShapeDtypeStruct((t, D_MODEL), F32),
        compiler_params=_cparams("parallel"),
        name="outproj",
    )(ya, yb, gate, x2d, w)


_NORM_TM = 1024


def _final_norm_kernel(x_ref, g_ref, o_ref):
    o_ref[...] = _rms(x_ref[...], g_ref[...])


def _final_norm(x2d, g):
    t = x2d.shape[0]
    tm = _NORM_TM
    return pl.pallas_call(
        _final_norm_kernel,
        grid=(t // tm,),
        in_specs=[pl.BlockSpec((tm, D_MODEL), lambda i: (i, 0)), _resident((1, D_MODEL))],
        out_specs=pl.BlockSpec((tm, D_MODEL), lambda i: (i, 0)),
        out_shape=jax.ShapeDtypeStruct((t, D_MODEL), F32),
        compiler_params=_cparams("parallel"),
        name="final_norm",
    )(x2d, g)


def _swap_halves(w):
    half = w.shape[-1] // 2
    return jnp.concatenate([w[..., half:], w[..., :half]], axis=-1)


def _prep_w_in(w_in):
    o = 0
    qa = w_in[..., o:o + 1024]; o += 1024
    ka = w_in[..., o:o + 128]; o += 128
    va = w_in[..., o:o + 128]; o += 128
    ga = w_in[..., o:o + 1024]; o += 1024
    cq = w_in[..., o:o + 384]; o += 384
    ckv = w_in[..., o:o + 256]; o += 256
    kr = w_in[..., o:o + 64]; o += 64
    gb = w_in[..., o:o + 1024]
    krsw = _swap_halves(kr)
    k0, k1 = ka[..., :64], ka[..., 64:]
    v0, v1 = va[..., :64], va[..., 64:]
    cols = [qa, ga, gb, cq, ckv, kr, kr, krsw, krsw, k0, k0, k1, k1, v0, v0, v1, v1]
    return jnp.concatenate(cols, axis=-1).astype(BF16)


def _prep_w_q(w_q_b):
    w = w_q_b.reshape(DEPTH, Q_LORA_RANK, MLA_HEADS, MLA_QK_DIM)
    nope = w[..., :MLA_NOPE_DIM].reshape(DEPTH, Q_LORA_RANK, MLA_HEADS * MLA_NOPE_DIM)
    rope = w[..., MLA_NOPE_DIM:]
    rope_sw = _swap_halves(rope).reshape(DEPTH, Q_LORA_RANK, MLA_HEADS * MLA_ROPE_DIM)
    rope = rope.reshape(DEPTH, Q_LORA_RANK, MLA_HEADS * MLA_ROPE_DIM)
    return jnp.swapaxes(jnp.concatenate([nope, rope, rope_sw], axis=-1), 1, 2).astype(BF16)


def _prep_w_kv(w_kv_b):
    w = w_kv_b.reshape(DEPTH, KV_LORA_RANK, MLA_HEADS, MLA_NOPE_DIM + MLA_V_DIM)
    wk = w[..., :MLA_NOPE_DIM].reshape(DEPTH, KV_LORA_RANK, MLA_HEADS * MLA_NOPE_DIM)
    wv = w[..., MLA_NOPE_DIM:].reshape(DEPTH, KV_LORA_RANK, MLA_HEADS * MLA_V_DIM)
    return wk.astype(BF16), jnp.swapaxes(wv, 1, 2).astype(BF16)


def _rope_tables(s):
    pos = jnp.arange(s, dtype=F32)
    inv_freq = ROPE_THETA ** (-jnp.arange(0, MLA_ROPE_DIM, 2, dtype=F32) / MLA_ROPE_DIM)
    ang = pos[:, None] * inv_freq[None, :]
    cos, sin = jnp.cos(ang), jnp.sin(ang)
    cc = jnp.concatenate([cos, cos], axis=-1)
    ss = jnp.concatenate([-sin, sin], axis=-1)
    return jnp.tile(cc, (1, 2)), jnp.tile(ss, (1, 2)), cc.T, ss.T


def kernel(x, attn_norm_g, w_in, swa_sinks, q_a_norm_g, kv_a_norm_g, w_q_b, w_kv_b, w_out, final_norm_g):
    b, s, d = x.shape
    t = b * s
    w_in_p = _prep_w_in(w_in)
    w_q_p = _prep_w_q(w_q_b)
    w_k_p, w_vt_p = _prep_w_kv(w_kv_b)
    w_out_p = w_out.astype(BF16)
    cos2, sin2, cost, sint = _rope_tables(s)
    sinks2 = swa_sinks.astype(F32) * LOG2E

    x2d = x.reshape(t, d)
    for l in range(DEPTH):
        qa, gate, lat, kv = _inproj(x2d, attn_norm_g[l][None], w_in_p[l])
        qt, k, vt = _mla_proj(lat.reshape(b, s, LAT_W), q_a_norm_g[l][None], kv_a_norm_g[l][None],
                              w_q_p[l], w_k_p[l], w_vt_p[l], cos2, sin2, cost, sint)
        ya = _swa_attn(sinks2[l], qa.reshape(b, s, A_Q), kv.reshape(b, s, KV_W))
        yb = _mla_attn(qt, k, vt)
        x2d = _outproj(ya.reshape(t, SWA_WIDTH), yb.reshape(t, MLA_WIDTH), gate, x2d, w_out_p[l])
    return _final_norm(x2d, final_norm_g[None]).reshape(b, s, d)
```

```python
import functools
import math

import jax
import jax.numpy as jnp
from jax import lax
from jax.experimental import pallas as pl
from jax.experimental.pallas import tpu as pltpu

F32 = jnp.float32
BF16 = jnp.bfloat16

D_MODEL = 2048
DEPTH = 4
EPS = 1e-6
BLOCK = 128
WINDOW = 128
NEG = -1e30
LOG2E = math.log2(math.e)

SWA_WIDTH = 1024
SWA_HEAD_DIM = 64
SWA_Q_HEADS = 16
SWA_KV_HEADS = 2
SWA_GROUP = 8

MLA_WIDTH = 1024
MLA_V_DIM = 128
MLA_HEADS = 8
MLA_NOPE_DIM = 128
MLA_ROPE_DIM = 64
MLA_QK_DIM = 192
MLA_QK_PAD = 256
Q_LORA_RANK = 384
KV_LORA_RANK = 256
ROPE_THETA = 10000.0

A_Q = 1024
GATE_W = 2048
LAT_W = Q_LORA_RANK + KV_LORA_RANK + 4 * MLA_ROPE_DIM
KK_W = 4 * SWA_HEAD_DIM
VT_ROWS = 4 * SWA_HEAD_DIM
ROW_W = GATE_W + LAT_W + KK_W
COL_W = A_Q + VT_ROWS

SWA_QSCALE = SWA_HEAD_DIM ** -0.5 * LOG2E
MLA_QSCALE = MLA_QK_DIM ** -0.5 * LOG2E

VMEM_LIMIT = 56 * 1024 * 1024
NT_DIMS = (((1,), (1,)), ((), ()))


def _cparams(*sem):
    return pltpu.CompilerParams(dimension_semantics=sem, vmem_limit_bytes=VMEM_LIMIT)


def _resident(shape):
    nd = len(shape)
    return pl.BlockSpec(shape, lambda *_: (0,) * nd, pipeline_mode=pl.Buffered(1))


def _layer(stacked, l):
    tail = stacked.shape[1:]
    return pl.BlockSpec((None,) + tail, lambda *_: (l,) + (0,) * len(tail),
                        pipeline_mode=pl.Buffered(1))


def _rms(x, g):
    return x * lax.rsqrt(jnp.mean(x * x, axis=-1, keepdims=True) + EPS) * g


_INPROJ_TM = 512
_INPROJ_ROW_CHUNKS = (
    (0, 0, 0, 512), (0, 512, 512, 512), (0, 1024, 1024, 512), (0, 1536, 1536, 512),
    (1, 0, 2048, 512), (1, 512, 2560, 384),
    (2, 0, 2944, 256),
)
_INPROJ_COL_CHUNKS = (
    (3, 0, 0, 512, SWA_QSCALE), (3, 512, 512, 512, SWA_QSCALE),
    (4, 0, 1024, 256, None),
)


def _inproj_kernel(x_ref, g_ref, wr_ref, wc_ref, gate_ref, lat_ref, kk_ref, qt_ref, vt_ref):
    h = _rms(x_ref[...], g_ref[...]).astype(BF16)
    outs = (gate_ref, lat_ref, kk_ref, qt_ref, vt_ref)
    for oi, ooff, woff, width in _INPROJ_ROW_CHUNKS:
        r = jnp.dot(h, wr_ref[:, woff:woff + width], preferred_element_type=F32)
        outs[oi][:, ooff:ooff + width] = r.astype(outs[oi].dtype)
    for oi, ooff, woff, rows, scale in _INPROJ_COL_CHUNKS:
        r = lax.dot_general(wc_ref[woff:woff + rows, :], h, NT_DIMS, preferred_element_type=F32)
        if scale is not None:
            r = r * scale
        outs[oi][ooff:ooff + rows, :] = r.astype(outs[oi].dtype)


def _inproj(x2d, g, w_row, w_col, l):
    t = x2d.shape[0]
    tm = _INPROJ_TM
    return pl.pallas_call(
        _inproj_kernel,
        grid=(t // tm,),
        in_specs=[
            pl.BlockSpec((tm, D_MODEL), lambda i: (i, 0)),
            _layer(g, l),
            _layer(w_row, l),
            _layer(w_col, l),
        ],
        out_specs=[
            pl.BlockSpec((tm, GATE_W), lambda i: (i, 0)),
            pl.BlockSpec((tm, LAT_W), lambda i: (i, 0)),
            pl.BlockSpec((tm, KK_W), lambda i: (i, 0)),
            pl.BlockSpec((A_Q, tm), lambda i: (0, i)),
            pl.BlockSpec((VT_ROWS, tm), lambda i: (0, i)),
        ],
        out_shape=[
            jax.ShapeDtypeStruct((t, GATE_W), BF16),
            jax.ShapeDtypeStruct((t, LAT_W), F32),
            jax.ShapeDtypeStruct((t, KK_W), BF16),
            jax.ShapeDtypeStruct((A_Q, t), BF16),
            jax.ShapeDtypeStruct((VT_ROWS, t), BF16),
        ],
        compiler_params=_cparams("parallel"),
        name="inproj",
    )(x2d, g, w_row, w_col)


_MLAP_TM = 1024
_VT_TILE = 256


def _mla_proj_kernel(lat_ref, gq_ref, gkv_ref, wqt_ref, wk_ref, wvt_ref, cos_ref, sin_ref,
                     cost_ref, sint_ref, qt_ref, k_ref, vt_ref):
    tm = lat_ref.shape[1]
    cqn = _rms(lat_ref[0, :, 0:384], gq_ref[...]).astype(BF16)
    ckvn = _rms(lat_ref[0, :, 384:640], gkv_ref[...]).astype(BF16)
    krot = lat_ref[0, :, 640:768] * cos_ref[...] + lat_ref[0, :, 768:896] * sin_ref[...]
    lane = lax.broadcasted_iota(jnp.int32, (tm, 128), 1)
    krot = jnp.where(lane < 64, krot, 0.0).astype(BF16)

    knope = jnp.dot(ckvn, wk_ref[...], preferred_element_type=F32)
    for h in range(MLA_HEADS):
        k_ref[0, h, :, 0:128] = knope[:, h * 128:(h + 1) * 128].astype(BF16)
        k_ref[0, h, :, 128:256] = krot

    vt = lax.dot_general(wvt_ref[...], ckvn, NT_DIMS, preferred_element_type=F32)
    for h in range(MLA_HEADS):
        for c in range(tm // _VT_TILE):
            vt_ref[0, h, c] = vt[h * 128:(h + 1) * 128, c * _VT_TILE:(c + 1) * _VT_TILE].astype(BF16)

    qt = lax.dot_general(wqt_ref[...], cqn, NT_DIMS, preferred_element_type=F32)
    cost = cost_ref[...]
    sint = sint_ref[...]
    for h in range(MLA_HEADS):
        qt_ref[0, h, 0:128, :] = (qt[h * 128:(h + 1) * 128] * MLA_QSCALE).astype(BF16)
        r0 = 1024 + h * MLA_ROPE_DIM
        qr = (qt[r0:r0 + MLA_ROPE_DIM] * cost + qt[r0 + 512:r0 + 512 + MLA_ROPE_DIM] * sint) * MLA_QSCALE
        qt_ref[0, h, 128:192, :] = qr.astype(BF16)
        qt_ref[0, h, 192:256, :] = jnp.zeros((MLA_QK_PAD - MLA_QK_DIM, tm), BF16)


def _mla_proj(lat, gq, gkv, wqt, wk, wvt, cos2, sin2, cost, sint, l):
    b, s, _ = lat.shape
    tm = _MLAP_TM
    nvt = tm // _VT_TILE
    return pl.pallas_call(
        _mla_proj_kernel,
        grid=(b, s // tm),
        in_specs=[
            pl.BlockSpec((1, tm, LAT_W), lambda bi, i: (bi, i, 0)),
            _layer(gq, l),
            _layer(gkv, l),
            _layer(wqt, l),
            _layer(wk, l),
            _layer(wvt, l),
            pl.BlockSpec((tm, 128), lambda bi, i: (i, 0)),
            pl.BlockSpec((tm, 128), lambda bi, i: (i, 0)),
            pl.BlockSpec((MLA_ROPE_DIM, tm), lambda bi, i: (0, i)),
            pl.BlockSpec((MLA_ROPE_DIM, tm), lambda bi, i: (0, i)),
        ],
        out_specs=[
            pl.BlockSpec((1, MLA_HEADS, MLA_QK_PAD, tm), lambda bi, i: (bi, 0, 0, i)),
            pl.BlockSpec((1, MLA_HEADS, tm, MLA_QK_PAD), lambda bi, i: (bi, 0, i, 0)),
            pl.BlockSpec((1, MLA_HEADS, nvt, MLA_V_DIM, _VT_TILE), lambda bi, i: (bi, 0, i, 0, 0)),
        ],
        out_shape=[
            jax.ShapeDtypeStruct((b, MLA_HEADS, MLA_QK_PAD, s), BF16),
            jax.ShapeDtypeStruct((b, MLA_HEADS, s, MLA_QK_PAD), BF16),
            jax.ShapeDtypeStruct((b, MLA_HEADS, s // _VT_TILE, MLA_V_DIM, _VT_TILE), BF16),
        ],
        compiler_params=_cparams("parallel", "parallel"),
        name="mla_proj",
    )(lat, gq, gkv, wqt, wk, wvt, cos2, sin2, cost, sint)


_MLA_TQ = 512
_MLA_HP = 4


def _mla_attn_kernel(qt_ref, k_ref, vt_ref, o_ref, m_ref, l_ref, acc_ref, st_ref, mx_ref):
    tq = qt_ref.shape[3]
    qi = pl.program_id(2)
    m_ref[...] = jnp.full_like(m_ref, NEG)
    l_ref[...] = jnp.zeros_like(l_ref)
    acc_ref[...] = jnp.zeros_like(acc_ref)
    nsub = tq // _VT_TILE

    def scores(hh, j, slot, masked):
        kt = k_ref[0, hh, pl.ds(pl.multiple_of(j * tq, tq), tq), :]
        st = jnp.dot(kt, qt_ref[0, hh], preferred_element_type=F32)
        if masked:
            kpos = lax.broadcasted_iota(jnp.int32, st.shape, 0)
            qpos = lax.broadcasted_iota(jnp.int32, st.shape, 1)
            st = jnp.where(kpos <= qpos, st, NEG)
        st_ref[slot, hh] = st
        mx_ref[slot, hh] = jnp.max(st, axis=0, keepdims=True)

    def update(hh, j, slot):
        m_old = m_ref[hh]
        m_new = jnp.maximum(m_old, mx_ref[slot, hh])
        alpha = jnp.exp2(m_old - m_new)
        p = jnp.exp2(st_ref[slot, hh] - m_new)
        l_ref[hh] = alpha * l_ref[hh] + jnp.sum(p, axis=0, keepdims=True)
        m_ref[hh] = m_new
        pb = p.astype(BF16)
        pv = jnp.dot(vt_ref[0, hh, j * nsub], pb[0:_VT_TILE], preferred_element_type=F32)
        for c in range(1, nsub):
            pv += jnp.dot(vt_ref[0, hh, j * nsub + c], pb[c * _VT_TILE:(c + 1) * _VT_TILE],
                          preferred_element_type=F32)
        acc_ref[hh] = alpha * acc_ref[hh] + pv

    def stage(s, slot, masked):
        for hh in range(_MLA_HP):
            scores(hh, s + 1, 1 - slot, masked)
            update(hh, s, slot)

    def finish(slot):
        for hh in range(_MLA_HP):
            update(hh, qi, slot)
        for hh in range(_MLA_HP):
            out = acc_ref[hh] / l_ref[hh]
            o_ref[0, :, hh * MLA_V_DIM:(hh + 1) * MLA_V_DIM] = out.T.astype(o_ref.dtype)

    @pl.when(qi == 0)
    def _():
        for hh in range(_MLA_HP):
            scores(hh, 0, 0, True)
        finish(0)

    @pl.when(qi > 0)
    def _():
        for hh in range(_MLA_HP):
            scores(hh, 0, 0, False)
        npairs = (qi - 1) // 2

        def body(t, carry):
            stage(2 * t, 0, False)
            stage(2 * t + 1, 1, False)
            return carry

        lax.fori_loop(0, npairs, body, 0)

        @pl.when(qi % 2 == 1)
        def _():
            stage(qi - 1, 0, True)
            finish(1)

        @pl.when(qi % 2 == 0)
        def _():
            stage(qi - 2, 0, False)
            stage(qi - 1, 1, True)
            finish(0)


def _mla_attn(qt, k, vt):
    b, h, _, s = qt.shape
    tq = _MLA_TQ
    hp = _MLA_HP
    return pl.pallas_call(
        _mla_attn_kernel,
        grid=(b, h // hp, s // tq),
        in_specs=[
            pl.BlockSpec((1, hp, MLA_QK_PAD, tq), lambda bi, hi, i: (bi, hi, 0, i)),
            pl.BlockSpec((1, hp, s, MLA_QK_PAD), lambda bi, hi, i: (bi, hi, 0, 0)),
            pl.BlockSpec((1, hp, s // _VT_TILE, MLA_V_DIM, _VT_TILE), lambda bi, hi, i: (bi, hi, 0, 0, 0)),
        ],
        out_specs=pl.BlockSpec((1, tq, hp * MLA_V_DIM), lambda bi, hi, i: (bi, i, hi)),
        out_shape=jax.ShapeDtypeStruct((b, s, MLA_WIDTH), BF16),
        scratch_shapes=[
            pltpu.VMEM((hp, 1, tq), F32),
            pltpu.VMEM((hp, 1, tq), F32),
            pltpu.VMEM((hp, MLA_V_DIM, tq), F32),
            pltpu.VMEM((2, hp, tq, tq), F32),
            pltpu.VMEM((2, hp, 1, tq), F32),
        ],
        compiler_params=_cparams("parallel", "parallel", "arbitrary"),
        name="mla_attn",
    )(qt, k, vt)


_SWA_TQ = 512
_SWA_NB = _SWA_TQ // BLOCK
_SWA_GW = SWA_GROUP * BLOCK


def _swa_kernel(sink_ref, bias_ref, qt_ref, kc_ref, kp_ref, vc_ref, vp_ref, o_ref, kbuf_ref, vbuf_ref):
    tq = kc_ref.shape[1]
    kbuf_ref[0:BLOCK] = kp_ref[0]
    kbuf_ref[BLOCK:BLOCK + tq] = kc_ref[0]
    vbuf_ref[:, 0:BLOCK] = vp_ref[...]
    vbuf_ref[:, BLOCK:BLOCK + tq] = vc_ref[...]
    first_variant = jnp.where(pl.program_id(1) == 0, 1, 0)
    half = SWA_HEAD_DIM
    zeros = jnp.zeros((half, BLOCK), BF16)

    def scores(blk, hk):
        r0 = blk * BLOCK
        kwin = kbuf_ref[r0:r0 + 2 * BLOCK, hk * BLOCK:(hk + 1) * BLOCK]
        cols = []
        for p in range(SWA_GROUP // 2):
            row = hk * (SWA_GROUP * half) + p * BLOCK
            qt = qt_ref[row:row + BLOCK, r0:r0 + BLOCK]
            cols.append(jnp.concatenate([qt[0:half], zeros], axis=0))
            cols.append(jnp.concatenate([zeros, qt[half:BLOCK]], axis=0))
        qz = jnp.concatenate(cols, axis=1)
        st = jnp.dot(kwin, qz, preferred_element_type=F32)
        variant = first_variant if blk == 0 else 0
        return st + bias_ref[variant, hk]

    def finish(blk, hk, s2):
        r0 = blk * BLOCK
        sink = sink_ref[hk]
        m = jnp.maximum(jnp.max(s2, axis=0, keepdims=True), sink)
        p = jnp.exp2(s2 - m)
        l = jnp.sum(p, axis=0, keepdims=True) + jnp.exp2(sink - m)
        vwin = vbuf_ref[hk * BLOCK:(hk + 1) * BLOCK, r0:r0 + 2 * BLOCK]
        ot = jnp.dot(vwin, p.astype(BF16), preferred_element_type=F32) * (1.0 / l)
        for pr in range(SWA_GROUP // 2):
            c0 = 2 * pr * BLOCK
            pair_t = jnp.concatenate([ot[0:half, c0:c0 + BLOCK],
                                      ot[half:BLOCK, c0 + BLOCK:c0 + 2 * BLOCK]], axis=0)
            col = hk * (SWA_GROUP * half) + pr * BLOCK
            o_ref[0, r0:r0 + BLOCK, col:col + BLOCK] = pair_t.T.astype(o_ref.dtype)

    chains = [(blk, hk) for blk in range(tq // BLOCK) for hk in range(SWA_KV_HEADS)]
    s_next = scores(*chains[0])
    for c, chain in enumerate(chains):
        s_cur = s_next
        if c + 1 < len(chains):
            s_next = scores(*chains[c + 1])
        finish(*chain, s_cur)


def _swa_attn(sink_rows, bias, qat, kk, vt, b, s):
    tq = _SWA_TQ
    nq = s // tq
    nb = _SWA_NB

    def prev_block(i):
        return jnp.maximum(i * nb - 1, 0)

    return pl.pallas_call(
        _swa_kernel,
        grid=(b, nq),
        in_specs=[
            _resident(sink_rows.shape),
            _resident(bias.shape),
            pl.BlockSpec((A_Q, tq), lambda bi, i: (0, bi * nq + i)),
            pl.BlockSpec((1, tq, KK_W), lambda bi, i: (bi, i, 0)),
            pl.BlockSpec((1, BLOCK, KK_W), lambda bi, i: (bi, prev_block(i), 0)),
            pl.BlockSpec((VT_ROWS, tq), lambda bi, i: (0, bi * nq + i)),
            pl.BlockSpec((VT_ROWS, BLOCK), lambda bi, i: (0, bi * nq * nb + prev_block(i))),
        ],
        out_specs=pl.BlockSpec((1, tq, SWA_WIDTH), lambda bi, i: (bi, i, 0)),
        out_shape=jax.ShapeDtypeStruct((b, s, SWA_WIDTH), BF16),
        scratch_shapes=[
            pltpu.VMEM((tq + BLOCK, KK_W), BF16),
            pltpu.VMEM((VT_ROWS, tq + BLOCK), BF16),
        ],
        compiler_params=_cparams("parallel", "parallel"),
        name="swa_attn",
    )(sink_rows, bias, qat, kk.reshape(b, s, KK_W), kk.reshape(b, s, KK_W), vt, vt)


def _swa_tables(swa_sinks):
    srow = jnp.arange(2 * BLOCK, dtype=jnp.int32)[:, None]
    tcol = jnp.arange(BLOCK, dtype=jnp.int32)[None, :]
    delta = BLOCK + tcol - srow
    valid = (delta >= 0) & (delta < WINDOW)
    slopes = jnp.exp2(-8.0 * jnp.arange(1, SWA_Q_HEADS + 1, dtype=F32) / SWA_Q_HEADS)
    slopes = slopes.reshape(SWA_KV_HEADS, SWA_GROUP)
    alibi = -(slopes * LOG2E)[:, :, None, None] * delta.astype(F32)[None, None]
    has_prev = jnp.where(valid[None, None], alibi, NEG)
    no_prev = jnp.where((srow >= BLOCK)[None, None], has_prev, NEG)
    bias = jnp.stack([has_prev, no_prev])
    bias = bias.transpose(0, 1, 3, 2, 4).reshape(2, SWA_KV_HEADS, 2 * BLOCK, _SWA_GW)
    sinks2 = swa_sinks.astype(F32) * LOG2E
    sink_rows = jnp.repeat(sinks2.reshape(DEPTH, SWA_KV_HEADS, 1, SWA_GROUP), BLOCK, axis=-1)
    return bias, sink_rows


_OUTPROJ_TM = 512
_OUTPROJ_TN = 512


def _outproj_kernel(*refs, final):
    if final:
        ya_ref, yb_ref, gate_ref, x_ref, w_ref, gfin_ref, o_ref = refs
    else:
        ya_ref, yb_ref, gate_ref, x_ref, w_ref, o_ref = refs
    g = gate_ref[...].astype(F32)
    y = jnp.concatenate([ya_ref[...], yb_ref[...]], axis=-1).astype(F32)
    gated = (y * (g / (1.0 + jnp.exp(-g)))).astype(BF16)
    for c in range(D_MODEL // _OUTPROJ_TN):
        sl = slice(c * _OUTPROJ_TN, (c + 1) * _OUTPROJ_TN)
        o_ref[:, sl] = x_ref[:, sl] + jnp.dot(gated, w_ref[:, sl], preferred_element_type=F32)
    if final:
        o_ref[...] = _rms(o_ref[...], gfin_ref[...])


def _outproj(ya, yb, gate, x2d, w, l, final_g=None):
    t = x2d.shape[0]
    tm = _OUTPROJ_TM
    final = final_g is not None
    in_specs = [
        pl.BlockSpec((tm, SWA_WIDTH), lambda i: (i, 0)),
        pl.BlockSpec((tm, MLA_WIDTH), lambda i: (i, 0)),
        pl.BlockSpec((tm, GATE_W), lambda i: (i, 0)),
        pl.BlockSpec((tm, D_MODEL), lambda i: (i, 0)),
        _layer(w, l),
    ]
    args = [ya, yb, gate, x2d, w]
    if final:
        in_specs.append(_resident(final_g.shape))
        args.append(final_g)
    return pl.pallas_call(
        functools.partial(_outproj_kernel, final=final),
        grid=(t // tm,),
        in_specs=in_specs,
        out_specs=pl.BlockSpec((tm, D_MODEL), lambda i: (i, 0)),
        out_shape=jax.ShapeDtypeStruct((t, D_MODEL), F32),
        compiler_params=_cparams("parallel"),
        name="outproj_final" if final else "outproj",
    )(*args)


def _swap_halves(w):
    half = w.shape[-1] // 2
    return jnp.concatenate([w[..., half:], w[..., :half]], axis=-1)


def _prep_w_in(w_in):
    w = w_in.astype(BF16)
    o = 0
    qa = w[..., o:o + 1024]; o += 1024
    ka = w[..., o:o + 128]; o += 128
    va = w[..., o:o + 128]; o += 128
    ga = w[..., o:o + 1024]; o += 1024
    cq = w[..., o:o + 384]; o += 384
    ckv = w[..., o:o + 256]; o += 256
    kr = w[..., o:o + 64]; o += 64
    gb = w[..., o:o + 1024]
    krsw = _swap_halves(kr)
    k0, k1 = ka[..., :64], ka[..., 64:]
    v0, v1 = va[..., :64], va[..., 64:]
    w_row = jnp.concatenate([ga, gb, cq, ckv, kr, kr, krsw, krsw, k0, k0, k1, k1], axis=-1)
    w_col = jnp.swapaxes(jnp.concatenate([qa, v0, v0, v1, v1], axis=-1), 1, 2)
    return w_row, w_col


def _prep_w_q(w_q_b):
    w = w_q_b.reshape(DEPTH, Q_LORA_RANK, MLA_HEADS, MLA_QK_DIM)
    nope = w[..., :MLA_NOPE_DIM].reshape(DEPTH, Q_LORA_RANK, MLA_HEADS * MLA_NOPE_DIM)
    rope = w[..., MLA_NOPE_DIM:]
    rope_sw = _swap_halves(rope).reshape(DEPTH, Q_LORA_RANK, MLA_HEADS * MLA_ROPE_DIM)
    rope = rope.reshape(DEPTH, Q_LORA_RANK, MLA_HEADS * MLA_ROPE_DIM)
    return jnp.swapaxes(jnp.concatenate([nope, rope, rope_sw], axis=-1), 1, 2).astype(BF16)


def _prep_w_kv(w_kv_b):
    w = w_kv_b.reshape(DEPTH, KV_LORA_RANK, MLA_HEADS, MLA_NOPE_DIM + MLA_V_DIM)
    wk = w[..., :MLA_NOPE_DIM].reshape(DEPTH, KV_LORA_RANK, MLA_HEADS * MLA_NOPE_DIM)
    wv = w[..., MLA_NOPE_DIM:].reshape(DEPTH, KV_LORA_RANK, MLA_HEADS * MLA_V_DIM)
    return wk.astype(BF16), jnp.swapaxes(wv, 1, 2).astype(BF16)


def _rope_tables(s):
    pos = jnp.arange(s, dtype=F32)
    inv_freq = ROPE_THETA ** (-jnp.arange(0, MLA_ROPE_DIM, 2, dtype=F32) / MLA_ROPE_DIM)
    ang = pos[:, None] * inv_freq[None, :]
    cos, sin = jnp.cos(ang), jnp.sin(ang)
    cc = jnp.concatenate([cos, cos], axis=-1)
    ss = jnp.concatenate([-sin, sin], axis=-1)
    return jnp.tile(cc, (1, 2)), jnp.tile(ss, (1, 2)), cc.T, ss.T


def kernel(x, attn_norm_g, w_in, swa_sinks, q_a_norm_g, kv_a_norm_g, w_q_b, w_kv_b, w_out, final_norm_g):
    b, s, d = x.shape
    t = b * s
    w_row, w_col = _prep_w_in(w_in)
    w_q_p = _prep_w_q(w_q_b)
    w_k_p, w_vt_p = _prep_w_kv(w_kv_b)
    w_out_p = w_out.astype(BF16)
    cos2, sin2, cost, sint = _rope_tables(s)
    swa_bias, sink_rows = _swa_tables(swa_sinks)
    g_attn = attn_norm_g[:, None, :]
    g_q = q_a_norm_g[:, None, :]
    g_kv = kv_a_norm_g[:, None, :]

    x2d = x.reshape(t, d)
    for l in range(DEPTH):
        gate, lat, kk, qat, vat = _inproj(x2d, g_attn, w_row, w_col, l)
        qt, k, vt = _mla_proj(lat.reshape(b, s, LAT_W), g_q, g_kv, w_q_p, w_k_p, w_vt_p,
                              cos2, sin2, cost, sint, l)
        ya = _swa_attn(sink_rows[l], swa_bias, qat, kk, vat, b, s)
        yb = _mla_attn(qt, k, vt)
        final_g = final_norm_g[None] if l == DEPTH - 1 else None
        x2d = _outproj(ya.reshape(t, SWA_WIDTH), yb.reshape(t, MLA_WIDTH), gate, x2d, w_out_p, l, final_g)
    return x2d.reshape(b, s, d)
```

```python
import functools
import math

import jax
import jax.numpy as jnp
from jax import lax
from jax.experimental import pallas as pl
from jax.experimental.pallas import tpu as pltpu

F32 = jnp.float32
BF16 = jnp.bfloat16

D_MODEL = 2048
DEPTH = 4
EPS = 1e-6
BLOCK = 128
WINDOW = 128
NEG = -1e30
LOG2E = math.log2(math.e)

SWA_WIDTH = 1024
SWA_HEAD_DIM = 64
SWA_Q_HEADS = 16
SWA_KV_HEADS = 2
SWA_GROUP = 8

MLA_WIDTH = 1024
MLA_V_DIM = 128
MLA_HEADS = 8
MLA_NOPE_DIM = 128
MLA_ROPE_DIM = 64
MLA_QK_DIM = 192
MLA_QK_PAD = 256
Q_LORA_RANK = 384
KV_LORA_RANK = 256
ROPE_THETA = 10000.0

A_Q = 1024
GATE_W = 2048
IN_WIDTH = 4032
LAT_W = Q_LORA_RANK + KV_LORA_RANK + 2 * MLA_ROPE_DIM
KA_W = SWA_KV_HEADS * SWA_HEAD_DIM
KK_W = 2 * KA_W
VT_ROWS = SWA_KV_HEADS * SWA_HEAD_DIM
ROW_W = GATE_W + LAT_W + KA_W
COL_W = A_Q + VT_ROWS

SWA_QSCALE = SWA_HEAD_DIM ** -0.5 * LOG2E
MLA_QSCALE = MLA_QK_DIM ** -0.5 * LOG2E

VMEM_LIMIT = 56 * 1024 * 1024
NT_DIMS = (((1,), (1,)), ((), ()))


def _cparams(*sem):
    return pltpu.CompilerParams(dimension_semantics=sem, vmem_limit_bytes=VMEM_LIMIT)


def _resident(shape):
    nd = len(shape)
    return pl.BlockSpec(shape, lambda *_: (0,) * nd, pipeline_mode=pl.Buffered(1))


def _layer(stacked, l):
    tail = stacked.shape[1:]
    return pl.BlockSpec((None,) + tail, lambda *_: (l,) + (0,) * len(tail),
                        pipeline_mode=pl.Buffered(1))


def _rms(x, g):
    return x * lax.rsqrt(jnp.mean(x * x, axis=-1, keepdims=True) + EPS) * g


_INPROJ_TM = 512
_INPROJ_ROW_CHUNKS = (
    (0, 0, 0, 512), (0, 512, 512, 512), (0, 1024, 1024, 512), (0, 1536, 1536, 512),
    (1, 0, 2048, 512),
)
_INPROJ_COL_CHUNKS = (
    (3, 0, 0, 512, SWA_QSCALE), (3, 512, 512, 512, SWA_QSCALE),
    (4, 0, 1024, 128, None),
)


def _inproj_kernel(x_ref, g_ref, wr_ref, wc_ref, gate_ref, lat_ref, kk_ref, qt_ref, vt_ref):
    h = _rms(x_ref[...], g_ref[...]).astype(BF16)
    outs = (gate_ref, lat_ref, kk_ref, qt_ref, vt_ref)
    for oi, ooff, woff, width in _INPROJ_ROW_CHUNKS:
        r = jnp.dot(h, wr_ref[:, woff:woff + width], preferred_element_type=F32)
        outs[oi][:, ooff:ooff + width] = r.astype(outs[oi].dtype)
    r = jnp.dot(h, wr_ref[:, 2560:ROW_W], preferred_element_type=F32)
    lat_ref[:, 512:LAT_W] = r[:, 0:LAT_W - 512]
    ka = r[:, LAT_W - 512:]
    ka_sw = pltpu.roll(ka, SWA_HEAD_DIM, 1)
    low = lax.broadcasted_iota(jnp.int32, ka.shape, 1) < SWA_HEAD_DIM
    kk_ref[:, 0:KA_W] = jnp.where(low, ka, ka_sw).astype(BF16)
    kk_ref[:, KA_W:KK_W] = jnp.where(low, ka_sw, ka).astype(BF16)
    for oi, ooff, woff, rows, scale in _INPROJ_COL_CHUNKS:
        r = lax.dot_general(wc_ref[woff:woff + rows, :], h, NT_DIMS, preferred_element_type=F32)
        if scale is not None:
            r = r * scale
        outs[oi][ooff:ooff + rows, :] = r.astype(outs[oi].dtype)


def _inproj(x2d, g, w_row, w_col, l):
    t = x2d.shape[0]
    tm = _INPROJ_TM
    return pl.pallas_call(
        _inproj_kernel,
        grid=(t // tm,),
        in_specs=[
            pl.BlockSpec((tm, D_MODEL), lambda i: (i, 0)),
            _layer(g, l),
            _layer(w_row, l),
            _layer(w_col, l),
        ],
        out_specs=[
            pl.BlockSpec((tm, GATE_W), lambda i: (i, 0)),
            pl.BlockSpec((tm, LAT_W), lambda i: (i, 0)),
            pl.BlockSpec((tm, KK_W), lambda i: (i, 0)),
            pl.BlockSpec((A_Q, tm), lambda i: (0, i)),
            pl.BlockSpec((VT_ROWS, tm), lambda i: (0, i)),
        ],
        out_shape=[
            jax.ShapeDtypeStruct((t, GATE_W), BF16),
            jax.ShapeDtypeStruct((t, LAT_W), F32),
            jax.ShapeDtypeStruct((t, KK_W), BF16),
            jax.ShapeDtypeStruct((A_Q, t), BF16),
            jax.ShapeDtypeStruct((VT_ROWS, t), BF16),
        ],
        compiler_params=_cparams("parallel"),
        name="inproj",
    )(x2d, g, w_row, w_col)


_MLAP_TM = 1024
_VT_TILE = 256
_VT_AUG = MLA_V_DIM + 16


def _mla_proj_kernel(lat_ref, gq_ref, gkv_ref, wqt_ref, wk_ref, wvt_ref, cos_ref, sin_ref,
                     cost_ref, sint_ref, qt_ref, k_ref, vt_ref):
    tm = lat_ref.shape[1]
    cqn = _rms(lat_ref[0, :, 0:384], gq_ref[...]).astype(BF16)
    ckvn = _rms(lat_ref[0, :, 384:640], gkv_ref[...]).astype(BF16)
    kslab = lat_ref[0, :, 640:768]
    krot = kslab * cos_ref[...] + pltpu.roll(kslab, MLA_ROPE_DIM, 1) * sin_ref[...]
    lane = lax.broadcasted_iota(jnp.int32, (tm, 128), 1)
    krot = jnp.where(lane < 64, krot, 0.0).astype(BF16)

    knope = jnp.dot(ckvn, wk_ref[...], preferred_element_type=F32)
    for h in range(MLA_HEADS):
        k_ref[0, h, :, 0:128] = knope[:, h * 128:(h + 1) * 128].astype(BF16)
        k_ref[0, h, :, 128:256] = krot

    vt = lax.dot_general(wvt_ref[...], ckvn, NT_DIMS, preferred_element_type=F32)
    ones = jnp.ones((_VT_AUG - MLA_V_DIM, _VT_TILE), BF16)
    for h in range(MLA_HEADS):
        for c in range(tm // _VT_TILE):
            vt_ref[0, h, c, 0:MLA_V_DIM] = vt[h * 128:(h + 1) * 128, c * _VT_TILE:(c + 1) * _VT_TILE].astype(BF16)
            vt_ref[0, h, c, MLA_V_DIM:_VT_AUG] = ones

    qt = lax.dot_general(wqt_ref[...], cqn, NT_DIMS, preferred_element_type=F32)
    cost = cost_ref[...]
    sint = sint_ref[...]
    for h in range(MLA_HEADS):
        qt_ref[0, h, 0:128, :] = (qt[h * 128:(h + 1) * 128] * MLA_QSCALE).astype(BF16)
        r0 = 1024 + h * MLA_ROPE_DIM
        qr = (qt[r0:r0 + MLA_ROPE_DIM] * cost + qt[r0 + 512:r0 + 512 + MLA_ROPE_DIM] * sint) * MLA_QSCALE
        qt_ref[0, h, 128:192, :] = qr.astype(BF16)
        qt_ref[0, h, 192:256, :] = jnp.zeros((MLA_QK_PAD - MLA_QK_DIM, tm), BF16)


def _mla_proj(lat, gq, gkv, wqt, wk, wvt, cos2, sin2, cost, sint, l):
    b, s, _ = lat.shape
    tm = _MLAP_TM
    nvt = tm // _VT_TILE
    return pl.pallas_call(
        _mla_proj_kernel,
        grid=(b, s // tm),
        in_specs=[
            pl.BlockSpec((1, tm, LAT_W), lambda bi, i: (bi, i, 0)),
            _layer(gq, l),
            _layer(gkv, l),
            _layer(wqt, l),
            _layer(wk, l),
            _layer(wvt, l),
            pl.BlockSpec((tm, 128), lambda bi, i: (i, 0)),
            pl.BlockSpec((tm, 128), lambda bi, i: (i, 0)),
            pl.BlockSpec((MLA_ROPE_DIM, tm), lambda bi, i: (0, i)),
            pl.BlockSpec((MLA_ROPE_DIM, tm), lambda bi, i: (0, i)),
        ],
        out_specs=[
            pl.BlockSpec((1, MLA_HEADS, MLA_QK_PAD, tm), lambda bi, i: (bi, 0, 0, i)),
            pl.BlockSpec((1, MLA_HEADS, tm, MLA_QK_PAD), lambda bi, i: (bi, 0, i, 0)),
            pl.BlockSpec((1, MLA_HEADS, nvt, _VT_AUG, _VT_TILE), lambda bi, i: (bi, 0, i, 0, 0)),
        ],
        out_shape=[
            jax.ShapeDtypeStruct((b, MLA_HEADS, MLA_QK_PAD, s), BF16),
            jax.ShapeDtypeStruct((b, MLA_HEADS, s, MLA_QK_PAD), BF16),
            jax.ShapeDtypeStruct((b, MLA_HEADS, s // _VT_TILE, _VT_AUG, _VT_TILE), BF16),
        ],
        compiler_params=_cparams("parallel", "parallel"),
        name="mla_proj",
    )(lat, gq, gkv, wqt, wk, wvt, cos2, sin2, cost, sint)


_MLA_TQ = 512
_MLA_HP = 4


def _mla_attn_kernel(qt_ref, k_ref, vt_ref, o_ref, m_ref, acc_ref, st_ref, mx_ref):
    tq = qt_ref.shape[3]
    qi = pl.program_id(2)
    m_ref[...] = jnp.full_like(m_ref, NEG)
    acc_ref[...] = jnp.zeros_like(acc_ref)
    nsub = tq // _VT_TILE

    def scores(hh, j, slot, masked):
        kt = k_ref[0, hh, pl.ds(pl.multiple_of(j * tq, tq), tq), :]
        st = jnp.dot(kt, qt_ref[0, hh], preferred_element_type=F32)
        if masked:
            kpos = lax.broadcasted_iota(jnp.int32, st.shape, 0)
            qpos = lax.broadcasted_iota(jnp.int32, st.shape, 1)
            st = jnp.where(kpos <= qpos, st, NEG)
        st_ref[slot, hh] = st
        mx_ref[slot, hh] = jnp.max(st, axis=0, keepdims=True)

    def update(hh, j, slot):
        m_old = m_ref[hh]
        m_new = jnp.maximum(m_old, mx_ref[slot, hh])
        alpha = jnp.exp2(m_old - m_new)
        pb = jnp.exp2(st_ref[slot, hh] - m_new).astype(BF16)
        m_ref[hh] = m_new
        pv = jnp.dot(vt_ref[0, hh, j * nsub], pb[0:_VT_TILE], preferred_element_type=F32)
        for c in range(1, nsub):
            pv += jnp.dot(vt_ref[0, hh, j * nsub + c], pb[c * _VT_TILE:(c + 1) * _VT_TILE],
                          preferred_element_type=F32)
        acc_ref[hh] = alpha * acc_ref[hh] + pv

    def stage(s, slot, masked):
        for hh in range(_MLA_HP):
            scores(hh, s + 1, 1 - slot, masked)
            update(hh, s, slot)

    def finish(slot):
        for hh in range(_MLA_HP):
            update(hh, qi, slot)
        for hh in range(_MLA_HP):
            out = acc_ref[hh, 0:MLA_V_DIM] / acc_ref[hh, MLA_V_DIM:MLA_V_DIM + 1]
            o_ref[0, :, hh * MLA_V_DIM:(hh + 1) * MLA_V_DIM] = out.T.astype(o_ref.dtype)

    @pl.when(qi == 0)
    def _():
        for hh in range(_MLA_HP):
            scores(hh, 0, 0, True)
        finish(0)

    @pl.when(qi > 0)
    def _():
        for hh in range(_MLA_HP):
            scores(hh, 0, 0, False)
        npairs = (qi - 1) // 2

        def body(t, carry):
            stage(2 * t, 0, False)
            stage(2 * t + 1, 1, False)
            return carry

        lax.fori_loop(0, npairs, body, 0)

        @pl.when(qi % 2 == 1)
        def _():
            stage(qi - 1, 0, True)
            finish(1)

        @pl.when(qi % 2 == 0)
        def _():
            stage(qi - 2, 0, False)
            stage(qi - 1, 1, True)
            finish(0)


def _mla_attn(qt, k, vt):
    b, h, _, s = qt.shape
    tq = _MLA_TQ
    hp = _MLA_HP
    return pl.pallas_call(
        _mla_attn_kernel,
        grid=(b, h // hp, s // tq),
        in_specs=[
            pl.BlockSpec((1, hp, MLA_QK_PAD, tq), lambda bi, hi, i: (bi, hi, 0, i)),
            pl.BlockSpec((1, hp, s, MLA_QK_PAD), lambda bi, hi, i: (bi, hi, 0, 0)),
            pl.BlockSpec((1, hp, s // _VT_TILE, _VT_AUG, _VT_TILE), lambda bi, hi, i: (bi, hi, 0, 0, 0)),
        ],
        out_specs=pl.BlockSpec((1, tq, hp * MLA_V_DIM), lambda bi, hi, i: (bi, i, hi)),
        out_shape=jax.ShapeDtypeStruct((b, s, MLA_WIDTH), BF16),
        scratch_shapes=[
            pltpu.VMEM((hp, 1, tq), F32),
            pltpu.VMEM((hp, _VT_AUG, tq), F32),
            pltpu.VMEM((2, hp, tq, tq), F32),
            pltpu.VMEM((2, hp, 1, tq), F32),
        ],
        compiler_params=_cparams("parallel", "parallel", "arbitrary"),
        name="mla_attn",
    )(qt, k, vt)


_SWA_TQ = 512
_SWA_NB = _SWA_TQ // BLOCK
_SWA_GW = SWA_GROUP * BLOCK


def _swa_kernel(sink_ref, bias_ref, qt_ref, kc_ref, kp_ref, vc_ref, vp_ref, o_ref, kbuf_ref, vbuf_ref):
    tq = kc_ref.shape[1]
    kbuf_ref[0:BLOCK] = kp_ref[0]
    kbuf_ref[BLOCK:BLOCK + tq] = kc_ref[0]
    vbuf_ref[:, 0:BLOCK] = vp_ref[...]
    vbuf_ref[:, BLOCK:BLOCK + tq] = vc_ref[...]
    first_variant = jnp.where(pl.program_id(1) == 0, 1, 0)
    half = SWA_HEAD_DIM
    zeros = jnp.zeros((half, BLOCK), BF16)
    ones = jnp.ones((16, 2 * BLOCK), BF16)
    r_io =lax.broadcasted_iota(jnp.int32, (BLOCK, _SWA_GW), 0)
    t_io = lax.broadcasted_iota(jnp.int32, (BLOCK, _SWA_GW), 1) & (BLOCK - 1)
    prev_visible = t_io < r_io

    def scores(blk, hk):
        r0 = blk * BLOCK
        kwin = kbuf_ref[r0:r0 + 2 * BLOCK, hk * BLOCK:(hk + 1) * BLOCK]
        cols = []
        for p in range(SWA_GROUP // 2):
            row = hk * (SWA_GROUP * half) + p * BLOCK
            qt = qt_ref[row:row + BLOCK, r0:r0 + BLOCK]
            cols.append(jnp.concatenate([qt[0:half], zeros], axis=0))
            cols.append(jnp.concatenate([zeros, qt[half:BLOCK]], axis=0))
        qz = jnp.concatenate(cols, axis=1)
        st = jnp.dot(kwin, qz, preferred_element_type=F32)
        variant = first_variant if blk == 0 else 0
        return jnp.where(prev_visible, st[0:BLOCK], st[BLOCK:2 * BLOCK]) + bias_ref[variant, hk]

    def finish(blk, hk, s2):
        r0 = blk * BLOCK
        sink = sink_ref[hk]
        m = jnp.maximum(jnp.max(s2, axis=0, keepdims=True), sink)
        p = jnp.exp2(s2 - m)
        pstack = jnp.concatenate([jnp.where(prev_visible, p, 0.0).astype(BF16),
                                  jnp.where(prev_visible, 0.0, p).astype(BF16)], axis=0)
        vwin = jnp.concatenate([vbuf_ref[hk * half:(hk + 1) * half, r0:r0 + 2 * BLOCK], ones], axis=0)
        ot = jnp.dot(vwin, pstack, preferred_element_type=F32)
        l = ot[half:half + 1] + jnp.exp2(sink - m)
        ot = ot[0:half] * (1.0 / l)
        for pr in range(SWA_GROUP // 2):
            c0 = 2 * pr * BLOCK
            pair_t = jnp.concatenate([ot[:, c0:c0 + BLOCK], ot[:, c0 + BLOCK:c0 + 2 * BLOCK]], axis=0)
            col = hk * (SWA_GROUP * half) + pr * BLOCK
            o_ref[0, r0:r0 + BLOCK, col:col + BLOCK] = pair_t.T.astype(o_ref.dtype)

    chains = [(blk, hk) for blk in range(tq // BLOCK) for hk in range(SWA_KV_HEADS)]
    s_next = scores(*chains[0])
    for c, chain in enumerate(chains):
        s_cur = s_next
        if c + 1 < len(chains):
            s_next = scores(*chains[c + 1])
        finish(*chain, s_cur)


def _swa_attn(sink_rows, bias, qat, kk, vt, b, s):
    tq = _SWA_TQ
    nq = s // tq
    nb = _SWA_NB

    def prev_block(i):
        return jnp.maximum(i * nb - 1, 0)

    return pl.pallas_call(
        _swa_kernel,
        grid=(b, nq),
        in_specs=[
            _resident(sink_rows.shape),
            _resident(bias.shape),
            pl.BlockSpec((A_Q, tq), lambda bi, i: (0, bi * nq + i)),
            pl.BlockSpec((1, tq, KK_W), lambda bi, i: (bi, i, 0)),
            pl.BlockSpec((1, BLOCK, KK_W), lambda bi, i: (bi, prev_block(i), 0)),
            pl.BlockSpec((VT_ROWS, tq), lambda bi, i: (0, bi * nq + i)),
            pl.BlockSpec((VT_ROWS, BLOCK), lambda bi, i: (0, bi * nq * nb + prev_block(i))),
        ],
        out_specs=pl.BlockSpec((1, tq, SWA_WIDTH), lambda bi, i: (bi, i, 0)),
        out_shape=jax.ShapeDtypeStruct((b, s, SWA_WIDTH), BF16),
        scratch_shapes=[
            pltpu.VMEM((tq + BLOCK, KK_W), BF16),
            pltpu.VMEM((VT_ROWS, tq + BLOCK), BF16),
        ],
        compiler_params=_cparams("parallel", "parallel"),
        name="swa_attn",
    )(sink_rows, bias, qat, kk.reshape(b, s, KK_W), kk.reshape(b, s, KK_W), vt, vt)


def _swa_tables(swa_sinks):
    row = jnp.arange(BLOCK, dtype=jnp.int32)[:, None]
    tcol = jnp.arange(BLOCK, dtype=jnp.int32)[None, :]
    delta = jnp.where(tcol < row, BLOCK + tcol - row, tcol - row)
    slopes = jnp.exp2(-8.0 * jnp.arange(1, SWA_Q_HEADS + 1, dtype=F32) / SWA_Q_HEADS)
    slopes = slopes.reshape(SWA_KV_HEADS, SWA_GROUP)
    has_prev = -(slopes * LOG2E)[:, :, None, None] * delta.astype(F32)[None, None]
    no_prev = jnp.where((tcol < row)[None, None], NEG, has_prev)
    bias = jnp.stack([has_prev, no_prev])
    bias = bias.transpose(0, 1, 3, 2, 4).reshape(2, SWA_KV_HEADS, BLOCK, _SWA_GW)
    sinks2 = swa_sinks.astype(F32) * LOG2E
    sink_rows = jnp.repeat(sinks2.reshape(DEPTH, SWA_KV_HEADS, 1, SWA_GROUP), BLOCK, axis=-1)
    return bias, sink_rows


_OUTPROJ_TM = 512
_OUTPROJ_TN = 512


def _outproj_kernel(*refs, final):
    if final:
        ya_ref, yb_ref, gate_ref, x_ref, w_ref, gfin_ref, o_ref = refs
    else:
        ya_ref, yb_ref, gate_ref, x_ref, w_ref, o_ref = refs
    g = gate_ref[...].astype(F32)
    y = jnp.concatenate([ya_ref[...], yb_ref[...]], axis=-1).astype(F32)
    gated = (y * (g / (1.0 + jnp.exp(-g)))).astype(BF16)
    for c in range(D_MODEL // _OUTPROJ_TN):
        sl = slice(c * _OUTPROJ_TN, (c + 1) * _OUTPROJ_TN)
        o_ref[:, sl] = x_ref[:, sl] + jnp.dot(gated, w_ref[:, sl], preferred_element_type=F32)
    if final:
        o_ref[...] = _rms(o_ref[...], gfin_ref[...])


def _outproj(ya, yb, gate, x2d, w, l, final_g=None):
    t = x2d.shape[0]
    tm = _OUTPROJ_TM
    final = final_g is not None
    in_specs = [
        pl.BlockSpec((tm, SWA_WIDTH), lambda i: (i, 0)),
        pl.BlockSpec((tm, MLA_WIDTH), lambda i: (i, 0)),
        pl.BlockSpec((tm, GATE_W), lambda i: (i, 0)),
        pl.BlockSpec((tm, D_MODEL), lambda i: (i, 0)),
        _layer(w, l),
    ]
    args = [ya, yb, gate, x2d, w]
    if final:
        in_specs.append(_resident(final_g.shape))
        args.append(final_g)
    return pl.pallas_call(
        functools.partial(_outproj_kernel, final=final),
        grid=(t // tm,),
        in_specs=in_specs,
        out_specs=pl.BlockSpec((tm, D_MODEL), lambda i: (i, 0)),
        out_shape=jax.ShapeDtypeStruct((t, D_MODEL), F32),
        compiler_params=_cparams("parallel"),
        name="outproj_final" if final else "outproj",
    )(*args)


def _swap_halves(w):
    half = w.shape[-1] // 2
    return jnp.concatenate([w[..., half:], w[..., :half]], axis=-1)


_PREP_TR = 256


def _prep_w_in_kernel(w_ref, row_ref, col_ref):
    x = w_ref[0]
    o = 0
    qa = x[:, o:o + A_Q]; o += A_Q
    ka = x[:, o:o + KA_W]; o += KA_W
    va = x[:, o:o + VT_ROWS]; o += VT_ROWS
    ga = x[:, o:o + 1024]; o += 1024
    cq = x[:, o:o + Q_LORA_RANK]; o += Q_LORA_RANK
    ckv = x[:, o:o + KV_LORA_RANK]; o += KV_LORA_RANK
    kr = x[:, o:o + MLA_ROPE_DIM]; o += MLA_ROPE_DIM
    gb = x[:, o:o + 1024]
    row = jnp.concatenate([ga, gb, cq, ckv, kr, _swap_halves(kr), ka], axis=1)
    row_ref[0] = row.astype(BF16)
    col_ref[0] = jnp.concatenate([qa, va], axis=1).T.astype(BF16)


def _prep_w_in(w_in):
    tr = _PREP_TR
    return pl.pallas_call(
        _prep_w_in_kernel,
        grid=(DEPTH, D_MODEL // tr),
        in_specs=[pl.BlockSpec((1, tr, IN_WIDTH), lambda l, i: (l, i, 0))],
        out_specs=[
            pl.BlockSpec((1, tr, ROW_W), lambda l, i: (l, i, 0)),
            pl.BlockSpec((1, COL_W, tr), lambda l, i: (l, 0, i)),
        ],
        out_shape=[
            jax.ShapeDtypeStruct((DEPTH, D_MODEL, ROW_W), BF16),
            jax.ShapeDtypeStruct((DEPTH, COL_W, D_MODEL), BF16),
        ],
        compiler_params=_cparams("parallel", "parallel"),
        name="prep_w_in",
    )(w_in)


def _prep_w_q(w_q_b):
    w = w_q_b.reshape(DEPTH, Q_LORA_RANK, MLA_HEADS, MLA_QK_DIM)
    nope = w[..., :MLA_NOPE_DIM].reshape(DEPTH, Q_LORA_RANK, MLA_HEADS * MLA_NOPE_DIM)
    rope = w[..., MLA_NOPE_DIM:]
    rope_sw = _swap_halves(rope).reshape(DEPTH, Q_LORA_RANK, MLA_HEADS * MLA_ROPE_DIM)
    rope = rope.reshape(DEPTH, Q_LORA_RANK, MLA_HEADS * MLA_ROPE_DIM)
    return jnp.swapaxes(jnp.concatenate([nope, rope, rope_sw], axis=-1), 1, 2).astype(BF16)


def _prep_w_kv(w_kv_b):
    w = w_kv_b.reshape(DEPTH, KV_LORA_RANK, MLA_HEADS, MLA_NOPE_DIM + MLA_V_DIM)
    wk = w[..., :MLA_NOPE_DIM].reshape(DEPTH, KV_LORA_RANK, MLA_HEADS * MLA_NOPE_DIM)
    wv = w[..., MLA_NOPE_DIM:].reshape(DEPTH, KV_LORA_RANK, MLA_HEADS * MLA_V_DIM)
    return wk.astype(BF16), jnp.swapaxes(wv, 1, 2).astype(BF16)


def _rope_tables(s):
    pos = jnp.arange(s, dtype=F32)
    inv_freq = ROPE_THETA ** (-jnp.arange(0, MLA_ROPE_DIM, 2, dtype=F32) / MLA_ROPE_DIM)
    ang = pos[:, None] * inv_freq[None, :]
    cos, sin = jnp.cos(ang), jnp.sin(ang)
    cc = jnp.concatenate([cos, cos], axis=-1)
    ss = jnp.concatenate([-sin, sin], axis=-1)
    return jnp.tile(cc, (1, 2)), jnp.tile(ss, (1, 2)), cc.T, ss.T


def kernel(x, attn_norm_g, w_in, swa_sinks, q_a_norm_g, kv_a_norm_g, w_q_b, w_kv_b, w_out, final_norm_g):
    b, s, d = x.shape
    t = b * s
    w_row, w_col = _prep_w_in(w_in)
    w_q_p = _prep_w_q(w_q_b)
    w_k_p, w_vt_p = _prep_w_kv(w_kv_b)
    w_out_p = w_out.astype(BF16)
    cos2, sin2, cost, sint = _rope_tables(s)
    swa_bias, sink_rows = _swa_tables(swa_sinks)
    g_attn = attn_norm_g[:, None, :]
    g_q = q_a_norm_g[:, None, :]
    g_kv = kv_a_norm_g[:, None, :]

    x2d = x.reshape(t, d)
    for l in range(DEPTH):
        gate, lat, kk, qat, vat = _inproj(x2d, g_attn, w_row, w_col, l)
        qt, k, vt = _mla_proj(lat.reshape(b, s, LAT_W), g_q, g_kv, w_q_p, w_k_p, w_vt_p,
                              cos2, sin2, cost, sint, l)
        ya = _swa_attn(sink_rows[l], swa_bias, qat, kk, vat, b, s)
        yb = _mla_attn(qt, k, vt)
        final_g = final_norm_g[None] if l == DEPTH - 1 else None
        x2d = _outproj(ya.reshape(t, SWA_WIDTH), yb.reshape(t, MLA_WIDTH), gate, x2d, w_out_p, l, final_g)
    return x2d.reshape(b, s, d)
```

```python
import functools
import math

import jax
import jax.numpy as jnp
from jax import lax
from jax.experimental import pallas as pl
from jax.experimental.pallas import tpu as pltpu

F32 = jnp.float32
BF16 = jnp.bfloat16

D_MODEL = 2048
DEPTH = 4
EPS = 1e-6
BLOCK = 128
WINDOW = 128
NEG = -1e30
LOG2E = math.log2(math.e)

SWA_WIDTH = 1024
SWA_HEAD_DIM = 64
SWA_Q_HEADS = 16
SWA_KV_HEADS = 2
SWA_GROUP = 8

MLA_WIDTH = 1024
MLA_V_DIM = 128
MLA_HEADS = 8
MLA_NOPE_DIM = 128
MLA_ROPE_DIM = 64
MLA_QK_DIM = 192
MLA_QK_PAD = 256
Q_LORA_RANK = 384
KV_LORA_RANK = 256
ROPE_THETA = 10000.0

A_Q = 1024
GATE_W = 2048
IN_WIDTH = 4032
LAT_W = Q_LORA_RANK + KV_LORA_RANK + 2 * MLA_ROPE_DIM
KA_W = SWA_KV_HEADS * SWA_HEAD_DIM
KK_W = 2 * KA_W
VT_ROWS = SWA_KV_HEADS * SWA_HEAD_DIM
ROW_W = GATE_W + LAT_W + KA_W
COL_W = A_Q + VT_ROWS

SWA_QSCALE = SWA_HEAD_DIM ** -0.5 * LOG2E
MLA_QSCALE = MLA_QK_DIM ** -0.5 * LOG2E

VMEM_LIMIT = 56 * 1024 * 1024
NT_DIMS = (((1,), (1,)), ((), ()))


def _cparams(*sem):
    return pltpu.CompilerParams(dimension_semantics=sem, vmem_limit_bytes=VMEM_LIMIT)


def _resident(shape):
    nd = len(shape)
    return pl.BlockSpec(shape, lambda *_: (0,) * nd, pipeline_mode=pl.Buffered(1))


def _layer(stacked, l):
    tail = stacked.shape[1:]
    return pl.BlockSpec((None,) + tail, lambda *_: (l,) + (0,) * len(tail),
                        pipeline_mode=pl.Buffered(1))


def _rms(x, g):
    return x * lax.rsqrt(jnp.mean(x * x, axis=-1, keepdims=True) + EPS) * g


_INPROJ_TM = 512
_INPROJ_ROW_CHUNKS = (
    (0, 0, 0, 512), (0, 512, 512, 512), (0, 1024, 1024, 512), (0, 1536, 1536, 512),
    (1, 0, 2048, 512),
)
_INPROJ_COL_CHUNKS = (
    (3, 0, 0, 512, SWA_QSCALE), (3, 512, 512, 512, SWA_QSCALE),
    (4, 0, 1024, 128, None),
)


def _inproj_kernel(x_ref, g_ref, wr_ref, wc_ref, gate_ref, lat_ref, kk_ref, qt_ref, vt_ref):
    h = _rms(x_ref[...], g_ref[...]).astype(BF16)
    outs = (gate_ref, lat_ref, kk_ref, qt_ref, vt_ref)
    for oi, ooff, woff, width in _INPROJ_ROW_CHUNKS:
        r = jnp.dot(h, wr_ref[:, woff:woff + width], preferred_element_type=F32)
        if outs[oi] is gate_ref:
            r = r * (1.0 / (1.0 + jnp.exp(-r)))
        outs[oi][:, ooff:ooff + width] = r.astype(outs[oi].dtype)
    r = jnp.dot(h, wr_ref[:, 2560:ROW_W], preferred_element_type=F32)
    lat_ref[:, 512:LAT_W] = r[:, 0:LAT_W - 512]
    ka = r[:, LAT_W - 512:]
    ka_sw = pltpu.roll(ka, SWA_HEAD_DIM, 1)
    low = lax.broadcasted_iota(jnp.int32, ka.shape, 1) < SWA_HEAD_DIM
    kk_ref[:, 0:KA_W] = jnp.where(low, ka, ka_sw).astype(BF16)
    kk_ref[:, KA_W:KK_W] = jnp.where(low, ka_sw, ka).astype(BF16)
    for oi, ooff, woff, rows, scale in _INPROJ_COL_CHUNKS:
        r = lax.dot_general(wc_ref[woff:woff + rows, :], h, NT_DIMS, preferred_element_type=F32)
        if scale is not None:
            r = r * scale
        outs[oi][ooff:ooff + rows, :] = r.astype(outs[oi].dtype)


def _inproj(x2d, g, w_row, w_col, l):
    t = x2d.shape[0]
    tm = _INPROJ_TM
    return pl.pallas_call(
        _inproj_kernel,
        grid=(t // tm,),
        in_specs=[
            pl.BlockSpec((tm, D_MODEL), lambda i: (i, 0)),
            _layer(g, l),
            _layer(w_row, l),
            _layer(w_col, l),
        ],
        out_specs=[
            pl.BlockSpec((tm, GATE_W), lambda i: (i, 0)),
            pl.BlockSpec((tm, LAT_W), lambda i: (i, 0)),
            pl.BlockSpec((tm, KK_W), lambda i: (i, 0)),
            pl.BlockSpec((A_Q, tm), lambda i: (0, i)),
            pl.BlockSpec((VT_ROWS, tm), lambda i: (0, i)),
        ],
        out_shape=[
            jax.ShapeDtypeStruct((t, GATE_W), BF16),
            jax.ShapeDtypeStruct((t, LAT_W), F32),
            jax.ShapeDtypeStruct((t, KK_W), BF16),
            jax.ShapeDtypeStruct((A_Q, t), BF16),
            jax.ShapeDtypeStruct((VT_ROWS, t), BF16),
        ],
        compiler_params=_cparams("parallel"),
        name="inproj",
    )(x2d, g, w_row, w_col)


_MLAP_TM = 1024
_VT_TILE = 256
_VT_AUG = MLA_V_DIM + 16


def _mla_proj_kernel(lat_ref, gq_ref, gkv_ref, wqt_ref, wk_ref, wvt_ref, cos_ref, sin_ref,
                     cost_ref, sint_ref, qt_ref, k_ref, vt_ref):
    tm = lat_ref.shape[1]
    cqn = _rms(lat_ref[0, :, 0:384], gq_ref[...]).astype(BF16)
    ckvn = _rms(lat_ref[0, :, 384:640], gkv_ref[...]).astype(BF16)
    kslab = lat_ref[0, :, 640:768]
    krot = kslab * cos_ref[...] + pltpu.roll(kslab, MLA_ROPE_DIM, 1) * sin_ref[...]
    lane = lax.broadcasted_iota(jnp.int32, (tm, 128), 1)
    krot = jnp.where(lane < 64, krot, 0.0).astype(BF16)

    knope = jnp.dot(ckvn, wk_ref[...], preferred_element_type=F32)
    for h in range(MLA_HEADS):
        k_ref[0, h, :, 0:128] = knope[:, h * 128:(h + 1) * 128].astype(BF16)
        k_ref[0, h, :, 128:256] = krot

    vt = lax.dot_general(wvt_ref[...], ckvn, NT_DIMS, preferred_element_type=F32)
    ones = jnp.ones((_VT_AUG - MLA_V_DIM, _VT_TILE), BF16)
    for h in range(MLA_HEADS):
        for c in range(tm // _VT_TILE):
            vt_ref[0, h, c, 0:MLA_V_DIM] = vt[h * 128:(h + 1) * 128, c * _VT_TILE:(c + 1) * _VT_TILE].astype(BF16)
            vt_ref[0, h, c, MLA_V_DIM:_VT_AUG] = ones

    qt = lax.dot_general(wqt_ref[...], cqn, NT_DIMS, preferred_element_type=F32)
    cost = cost_ref[...]
    sint = sint_ref[...]
    pad = jnp.zeros((MLA_QK_PAD - MLA_QK_DIM, _MLA_TQ), BF16)
    for h in range(MLA_HEADS):
        qn = (qt[h * 128:(h + 1) * 128] * MLA_QSCALE).astype(BF16)
        r0 = 1024 + h * MLA_ROPE_DIM
        qr = ((qt[r0:r0 + MLA_ROPE_DIM] * cost + qt[r0 + 512:r0 + 512 + MLA_ROPE_DIM] * sint)
              * MLA_QSCALE).astype(BF16)
        for c in range(tm // _MLA_TQ):
            cols = slice(c * _MLA_TQ, (c + 1) * _MLA_TQ)
            qt_ref[0, h, c, 0:128, :] = qn[:, cols]
            qt_ref[0, h, c, 128:192, :] = qr[:, cols]
            qt_ref[0, h, c, 192:256, :] = pad


def _mla_proj(lat, gq, gkv, wqt, wk, wvt, cos2, sin2, cost, sint, l):
    b, s, _ = lat.shape
    tm = _MLAP_TM
    nvt = tm // _VT_TILE
    return pl.pallas_call(
        _mla_proj_kernel,
        grid=(b, s // tm),
        in_specs=[
            pl.BlockSpec((1, tm, LAT_W), lambda bi, i: (bi, i, 0)),
            _layer(gq, l),
            _layer(gkv, l),
            _layer(wqt, l),
            _layer(wk, l),
            _layer(wvt, l),
            pl.BlockSpec((tm, 128), lambda bi, i: (i, 0)),
            pl.BlockSpec((tm, 128), lambda bi, i: (i, 0)),
            pl.BlockSpec((MLA_ROPE_DIM, tm), lambda bi, i: (0, i)),
            pl.BlockSpec((MLA_ROPE_DIM, tm), lambda bi, i: (0, i)),
        ],
        out_specs=[
            pl.BlockSpec((1, MLA_HEADS, tm // _MLA_TQ, MLA_QK_PAD, _MLA_TQ), lambda bi, i: (bi, 0, i, 0, 0)),
            pl.BlockSpec((1, MLA_HEADS, tm, MLA_QK_PAD), lambda bi, i: (bi, 0, i, 0)),
            pl.BlockSpec((1, MLA_HEADS, nvt, _VT_AUG, _VT_TILE), lambda bi, i: (bi, 0, i, 0, 0)),
        ],
        out_shape=[
            jax.ShapeDtypeStruct((b, MLA_HEADS, s // _MLA_TQ, MLA_QK_PAD, _MLA_TQ), BF16),
            jax.ShapeDtypeStruct((b, MLA_HEADS, s, MLA_QK_PAD), BF16),
            jax.ShapeDtypeStruct((b, MLA_HEADS, s // _VT_TILE, _VT_AUG, _VT_TILE), BF16),
        ],
        compiler_params=_cparams("parallel", "parallel"),
        name="mla_proj",
    )(lat, gq, gkv, wqt, wk, wvt, cos2, sin2, cost, sint)


_MLA_TQ = 512
_MLA_HP = 2
_MLA_LOOP_STAGES = 4


def _mla_tile_order(nq):
    off_diag = [(qi, j) for qi in range(nq) for j in range(qi)]
    return off_diag, [(qi, qi) for qi in range(nq)]


def _mla_attn_kernel(order_ref, qt_ref, k_ref, vt_ref, o_ref, *scratch):
    m_refs = scratch[0:_MLA_HP]
    acc_refs = scratch[_MLA_HP:2 * _MLA_HP]
    st_ref, mx_ref = scratch[2 * _MLA_HP:]
    nq, tq = qt_ref.shape[2], qt_ref.shape[4]
    for hh in range(_MLA_HP):
        m_refs[hh][...] = jnp.full_like(m_refs[hh], NEG)
        acc_refs[hh][...] = jnp.zeros_like(acc_refs[hh])
    nsub = tq // _VT_TILE

    def scores(hh, qi, j, slot, masked):
        kt = k_ref[0, hh, pl.ds(pl.multiple_of(j * tq, tq), tq), :]
        st = jnp.dot(kt, qt_ref[0, hh, qi], preferred_element_type=F32)
        if masked:
            kpos = lax.broadcasted_iota(jnp.int32, st.shape, 0)
            qpos = lax.broadcasted_iota(jnp.int32, st.shape, 1)
            st = jnp.where(kpos <= qpos, st, NEG)
        st_ref[slot, hh] = st
        mx_ref[slot, hh] = jnp.max(st, axis=0, keepdims=True)

    def update(hh, qi, j, slot):
        m_old = m_refs[hh][qi]
        m_new = jnp.maximum(m_old, mx_ref[slot, hh])
        alpha = jnp.exp2(m_old - m_new)
        pb = jnp.exp2(st_ref[slot, hh] - m_new).astype(BF16)
        m_refs[hh][qi] = m_new
        pv = jnp.dot(vt_ref[0, hh, j * nsub], pb[0:_VT_TILE], preferred_element_type=F32)
        for c in range(1, nsub):
            pv += jnp.dot(vt_ref[0, hh, j * nsub + c], pb[c * _VT_TILE:(c + 1) * _VT_TILE],
                          preferred_element_type=F32)
        acc_refs[hh][qi] = alpha * acc_refs[hh][qi] + pv

    def stage(cur, nxt, slot, masked_next):
        for hh in range(_MLA_HP):
            if nxt is not None:
                scores(hh, nxt[0], nxt[1], 1 - slot, masked_next)
            update(hh, cur[0], cur[1], slot)

    off_diag, diag = _mla_tile_order(nq)
    n_loop = (len(off_diag) - 1) // _MLA_LOOP_STAGES * _MLA_LOOP_STAGES
    for hh in range(_MLA_HP):
        scores(hh, off_diag[0][0], off_diag[0][1], 0, False)

    def body(t, carry):
        for u in range(_MLA_LOOP_STAGES):
            g = t * _MLA_LOOP_STAGES + u
            cur = (order_ref[0, g], order_ref[1, g])
            nxt = (order_ref[0, g + 1], order_ref[1, g + 1])
            stage(cur, nxt, u % 2, False)
        return carry

    lax.fori_loop(0, n_loop // _MLA_LOOP_STAGES, body, 0)
    tail = off_diag[n_loop:] + diag
    for i, cur in enumerate(tail):
        g = n_loop + i
        nxt = tail[i + 1] if i + 1 < len(tail) else None
        stage(cur, nxt, g % 2, nxt is not None and nxt[0] == nxt[1])

    for qi in range(nq):
        for hh in range(_MLA_HP):
            out = acc_refs[hh][qi, 0:MLA_V_DIM] / acc_refs[hh][qi, MLA_V_DIM:MLA_V_DIM + 1]
            o_ref[0, qi * tq:(qi + 1) * tq, hh * MLA_V_DIM:(hh + 1) * MLA_V_DIM] = out.T.astype(o_ref.dtype)


def _mla_attn(qt, k, vt):
    b, h, nq, _, tq = qt.shape
    s = nq * tq
    hp = _MLA_HP
    off_diag, _ = _mla_tile_order(nq)
    order = jnp.asarray(list(zip(*off_diag)), jnp.int32)
    return pl.pallas_call(
        _mla_attn_kernel,
        grid=(b, h // hp),
        in_specs=[
            pl.BlockSpec(memory_space=pltpu.SMEM),
            pl.BlockSpec((1, hp, nq, MLA_QK_PAD, tq), lambda bi, hi: (bi, hi, 0, 0, 0)),
            pl.BlockSpec((1, hp, s, MLA_QK_PAD), lambda bi, hi: (bi, hi, 0, 0)),
            pl.BlockSpec((1, hp, s // _VT_TILE, _VT_AUG, _VT_TILE), lambda bi, hi: (bi, hi, 0, 0, 0)),
        ],
        out_specs=pl.BlockSpec((1, s, hp * MLA_V_DIM), lambda bi, hi: (bi, 0, hi)),
        out_shape=jax.ShapeDtypeStruct((b, s, MLA_WIDTH), BF16),
        scratch_shapes=(
            [pltpu.VMEM((nq, 1, tq), F32) for _ in range(hp)]
            + [pltpu.VMEM((nq, _VT_AUG, tq), F32) for _ in range(hp)]
            + [pltpu.VMEM((2, hp, tq, tq), F32), pltpu.VMEM((2, hp, 1, tq), F32)]
        ),
        compiler_params=_cparams("parallel", "parallel"),
        name="mla_attn",
    )(order, qt, k, vt)


_SWA_TQ = 512
_SWA_NB = _SWA_TQ // BLOCK
_SWA_GW = SWA_GROUP * BLOCK


def _swa_kernel(sink_ref, bias_ref, qt_ref, kc_ref, kp_ref, vc_ref, vp_ref, o_ref, kbuf_ref, vbuf_ref):
    tq = kc_ref.shape[1]
    kbuf_ref[0:BLOCK] = kp_ref[0]
    kbuf_ref[BLOCK:BLOCK + tq] = kc_ref[0]
    vbuf_ref[:, 0:BLOCK] = vp_ref[...]
    vbuf_ref[:, BLOCK:BLOCK + tq] = vc_ref[...]
    first_variant = jnp.where(pl.program_id(1) == 0, 1, 0)
    half = SWA_HEAD_DIM
    zeros = jnp.zeros((half, BLOCK), BF16)
    ones = jnp.ones((16, 2 * BLOCK), BF16)
    r_io =lax.broadcasted_iota(jnp.int32, (BLOCK, _SWA_GW), 0)
    t_io = lax.broadcasted_iota(jnp.int32, (BLOCK, _SWA_GW), 1) & (BLOCK - 1)
    prev_visible = t_io < r_io

    def scores(blk, hk):
        r0 = blk * BLOCK
        kwin = kbuf_ref[r0:r0 + 2 * BLOCK, hk * BLOCK:(hk + 1) * BLOCK]
        cols = []
        for p in range(SWA_GROUP // 2):
            row = hk * (SWA_GROUP * half) + p * BLOCK
            qt = qt_ref[row:row + BLOCK, r0:r0 + BLOCK]
            cols.append(jnp.concatenate([qt[0:half], zeros], axis=0))
            cols.append(jnp.concatenate([zeros, qt[half:BLOCK]], axis=0))
        qz = jnp.concatenate(cols, axis=1)
        st = jnp.dot(kwin, qz, preferred_element_type=F32)
        variant = first_variant if blk == 0 else 0
        return jnp.where(prev_visible, st[0:BLOCK], st[BLOCK:2 * BLOCK]) + bias_ref[variant, hk]

    def finish(blk, hk, s2):
        r0 = blk * BLOCK
        sink = sink_ref[hk]
        m = jnp.maximum(jnp.max(s2, axis=0, keepdims=True), sink)
        p = jnp.exp2(s2 - m)
        pstack = jnp.concatenate([jnp.where(prev_visible, p, 0.0).astype(BF16),
                                  jnp.where(prev_visible, 0.0, p).astype(BF16)], axis=0)
        vwin = jnp.concatenate([vbuf_ref[hk * half:(hk + 1) * half, r0:r0 + 2 * BLOCK], ones], axis=0)
        ot = jnp.dot(vwin, pstack, preferred_element_type=F32)
        l = ot[half:half + 1] + jnp.exp2(sink - m)
        ot = ot[0:half] * (1.0 / l)
        for pr in range(SWA_GROUP // 2):
            c0 = 2 * pr * BLOCK
            pair_t = jnp.concatenate([ot[:, c0:c0 + BLOCK], ot[:, c0 + BLOCK:c0 + 2 * BLOCK]], axis=0)
            col = hk * (SWA_GROUP * half) + pr * BLOCK
            o_ref[0, r0:r0 + BLOCK, col:col + BLOCK] = pair_t.T.astype(o_ref.dtype)

    chains = [(blk, hk) for blk in range(tq // BLOCK) for hk in range(SWA_KV_HEADS)]
    s_next = scores(*chains[0])
    for c, chain in enumerate(chains):
        s_cur = s_next
        if c + 1 < len(chains):
            s_next = scores(*chains[c + 1])
        finish(*chain, s_cur)


def _swa_attn(sink_rows, bias, qat, kk, vt, b, s):
    tq = _SWA_TQ
    nq = s // tq
    nb = _SWA_NB

    def prev_block(i):
        return jnp.maximum(i * nb - 1, 0)

    return pl.pallas_call(
        _swa_kernel,
        grid=(b, nq),
        in_specs=[
            _resident(sink_rows.shape),
            _resident(bias.shape),
            pl.BlockSpec((A_Q, tq), lambda bi, i: (0, bi * nq + i)),
            pl.BlockSpec((1, tq, KK_W), lambda bi, i: (bi, i, 0)),
            pl.BlockSpec((1, BLOCK, KK_W), lambda bi, i: (bi, prev_block(i), 0)),
            pl.BlockSpec((VT_ROWS, tq), lambda bi, i: (0, bi * nq + i)),
            pl.BlockSpec((VT_ROWS, BLOCK), lambda bi, i: (0, bi * nq * nb + prev_block(i))),
        ],
        out_specs=pl.BlockSpec((1, tq, SWA_WIDTH), lambda bi, i: (bi, i, 0)),
        out_shape=jax.ShapeDtypeStruct((b, s, SWA_WIDTH), BF16),
        scratch_shapes=[
            pltpu.VMEM((tq + BLOCK, KK_W), BF16),
            pltpu.VMEM((VT_ROWS, tq + BLOCK), BF16),
        ],
        compiler_params=_cparams("parallel", "parallel"),
        name="swa_attn",
    )(sink_rows, bias, qat, kk.reshape(b, s, KK_W), kk.reshape(b, s, KK_W), vt, vt)


def _swa_tables(swa_sinks):
    row = jnp.arange(BLOCK, dtype=jnp.int32)[:, None]
    tcol = jnp.arange(BLOCK, dtype=jnp.int32)[None, :]
    delta = jnp.where(tcol < row, BLOCK + tcol - row, tcol - row)
    slopes = jnp.exp2(-8.0 * jnp.arange(1, SWA_Q_HEADS + 1, dtype=F32) / SWA_Q_HEADS)
    slopes = slopes.reshape(SWA_KV_HEADS, SWA_GROUP)
    has_prev = -(slopes * LOG2E)[:, :, None, None] * delta.astype(F32)[None, None]
    no_prev = jnp.where((tcol < row)[None, None], NEG, has_prev)
    bias = jnp.stack([has_prev, no_prev])
    bias = bias.transpose(0, 1, 3, 2, 4).reshape(2, SWA_KV_HEADS, BLOCK, _SWA_GW)
    sinks2 = swa_sinks.astype(F32) * LOG2E
    sink_rows = jnp.repeat(sinks2.reshape(DEPTH, SWA_KV_HEADS, 1, SWA_GROUP), BLOCK, axis=-1)
    return bias, sink_rows


_OUTPROJ_TM = 512
_OUTPROJ_TN = 512


def _outproj_kernel(*refs, final):
    if final:
        ya_ref, yb_ref, gate_ref, x_ref, w_ref, gfin_ref, o_ref = refs
    else:
        ya_ref, yb_ref, gate_ref, x_ref, w_ref, o_ref = refs
    gated = jnp.concatenate([ya_ref[...], yb_ref[...]], axis=-1) * gate_ref[...]
    for c in range(D_MODEL // _OUTPROJ_TN):
        sl = slice(c * _OUTPROJ_TN, (c + 1) * _OUTPROJ_TN)
        o_ref[:, sl] = x_ref[:, sl] + jnp.dot(gated, w_ref[:, sl], preferred_element_type=F32)
    if final:
        o_ref[...] = _rms(o_ref[...], gfin_ref[...])


def _outproj(ya, yb, gate, x2d, w, l, final_g=None):
    t = x2d.shape[0]
    tm = _OUTPROJ_TM
    final = final_g is not None
    in_specs = [
        pl.BlockSpec((tm, SWA_WIDTH), lambda i: (i, 0)),
        pl.BlockSpec((tm, MLA_WIDTH), lambda i: (i, 0)),
        pl.BlockSpec((tm, GATE_W), lambda i: (i, 0)),
        pl.BlockSpec((tm, D_MODEL), lambda i: (i, 0)),
        _layer(w, l),
    ]
    args = [ya, yb, gate, x2d, w]
    if final:
        in_specs.append(_resident(final_g.shape))
        args.append(final_g)
    return pl.pallas_call(
        functools.partial(_outproj_kernel, final=final),
        grid=(t // tm,),
        in_specs=in_specs,
        out_specs=pl.BlockSpec((tm, D_MODEL), lambda i: (i, 0)),
        out_shape=jax.ShapeDtypeStruct((t, D_MODEL), F32),
        compiler_params=_cparams("parallel"),
        name="outproj_final" if final else "outproj",
    )(*args)


def _swap_halves(w):
    half = w.shape[-1] // 2
    return jnp.concatenate([w[..., half:], w[..., :half]], axis=-1)


_PREP_TR = 256


def _prep_w_in_kernel(w_ref, row_ref, col_ref):
    def rows(lo, n):
        return w_ref[0, lo:lo + n, :]

    o = 0
    qa = rows(o, A_Q); o += A_Q
    ka = rows(o, KA_W); o += KA_W
    va = rows(o, VT_ROWS); o += VT_ROWS
    ga = rows(o, 1024); o += 1024
    cq = rows(o, Q_LORA_RANK); o += Q_LORA_RANK
    ckv = rows(o, KV_LORA_RANK); o += KV_LORA_RANK
    half = MLA_ROPE_DIM // 2
    kr = rows(o, MLA_ROPE_DIM)
    kr_sw = jnp.concatenate([rows(o + half, half), rows(o, half)], axis=0)
    o += MLA_ROPE_DIM
    gb = rows(o, 1024)
    col_ref[0, 0:A_Q, :] = qa.astype(BF16)
    col_ref[0, A_Q:COL_W, :] = va.astype(BF16)
    off = 0
    for piece in (ga, gb, cq, ckv, jnp.concatenate([kr, kr_sw, ka], axis=0)):
        n = piece.shape[0]
        row_ref[0, :, off:off + n] = piece.T.astype(BF16)
        off += n


def _prep_w_in(w_in):
    tr = _PREP_TR
    w_t = jnp.swapaxes(w_in, 1, 2)
    return pl.pallas_call(
        _prep_w_in_kernel,
        grid=(DEPTH, D_MODEL // tr),
        in_specs=[pl.BlockSpec((1, IN_WIDTH, tr), lambda l, i: (l, 0, i))],
        out_specs=[
            pl.BlockSpec((1, tr, ROW_W), lambda l, i: (l, i, 0)),
            pl.BlockSpec((1, COL_W, tr), lambda l, i: (l, 0, i)),
        ],
        out_shape=[
            jax.ShapeDtypeStruct((DEPTH, D_MODEL, ROW_W), BF16),
            jax.ShapeDtypeStruct((DEPTH, COL_W, D_MODEL), BF16),
        ],
        compiler_params=_cparams("parallel", "parallel"),
        name="prep_w_in",
    )(w_t)


def _prep_w_q(w_q_b):
    w = w_q_b.reshape(DEPTH, Q_LORA_RANK, MLA_HEADS, MLA_QK_DIM)
    nope = w[..., :MLA_NOPE_DIM].reshape(DEPTH, Q_LORA_RANK, MLA_HEADS * MLA_NOPE_DIM)
    rope = w[..., MLA_NOPE_DIM:]
    rope_sw = _swap_halves(rope).reshape(DEPTH, Q_LORA_RANK, MLA_HEADS * MLA_ROPE_DIM)
    rope = rope.reshape(DEPTH, Q_LORA_RANK, MLA_HEADS * MLA_ROPE_DIM)
    return jnp.swapaxes(jnp.concatenate([nope, rope, rope_sw], axis=-1), 1, 2).astype(BF16)


def _prep_w_kv(w_kv_b):
    w = w_kv_b.reshape(DEPTH, KV_LORA_RANK, MLA_HEADS, MLA_NOPE_DIM + MLA_V_DIM)
    wk = w[..., :MLA_NOPE_DIM].reshape(DEPTH, KV_LORA_RANK, MLA_HEADS * MLA_NOPE_DIM)
    wv = w[..., MLA_NOPE_DIM:].reshape(DEPTH, KV_LORA_RANK, MLA_HEADS * MLA_V_DIM)
    return wk.astype(BF16), jnp.swapaxes(wv, 1, 2).astype(BF16)


def _rope_tables(s):
    pos = jnp.arange(s, dtype=F32)
    inv_freq = ROPE_THETA ** (-jnp.arange(0, MLA_ROPE_DIM, 2, dtype=F32) / MLA_ROPE_DIM)
    ang = pos[:, None] * inv_freq[None, :]
    cos, sin = jnp.cos(ang), jnp.sin(ang)
    cc = jnp.concatenate([cos, cos], axis=-1)
    ss = jnp.concatenate([-sin, sin], axis=-1)
    return jnp.tile(cc, (1, 2)), jnp.tile(ss, (1, 2)), cc.T, ss.T


def kernel(x, attn_norm_g, w_in, swa_sinks, q_a_norm_g, kv_a_norm_g, w_q_b, w_kv_b, w_out, final_norm_g):
    b, s, d = x.shape
    t = b * s
    w_row, w_col = _prep_w_in(w_in)
    w_q_p = _prep_w_q(w_q_b)
    w_k_p, w_vt_p = _prep_w_kv(w_kv_b)
    w_out_p = w_out.astype(BF16)
    cos2, sin2, cost, sint = _rope_tables(s)
    swa_bias, sink_rows = _swa_tables(swa_sinks)
    g_attn = attn_norm_g[:, None, :]
    g_q = q_a_norm_g[:, None, :]
    g_kv = kv_a_norm_g[:, None, :]

    x2d = x.reshape(t, d)
    for l in range(DEPTH):
        gate, lat, kk, qat, vat = _inproj(x2d, g_attn, w_row, w_col, l)
        qt, k, vt = _mla_proj(lat.reshape(b, s, LAT_W), g_q, g_kv, w_q_p, w_k_p, w_vt_p,
                              cos2, sin2, cost, sint, l)
        ya = _swa_attn(sink_rows[l], swa_bias, qat, kk, vat, b, s)
        yb = _mla_attn(qt, k, vt)
        final_g = final_norm_g[None] if l == DEPTH - 1 else None
        x2d = _outproj(ya.reshape(t, SWA_WIDTH), yb.reshape(t, MLA_WIDTH), gate, x2d, w_out_p, l, final_g)
    return x2d.reshape(b, s, d)
```

```python
import functools
import math

import jax
import jax.numpy as jnp
from jax import lax
from jax.experimental import pallas as pl
from jax.experimental.pallas import tpu as pltpu

F32 = jnp.float32
BF16 = jnp.bfloat16

D_MODEL = 2048
DEPTH = 4
EPS = 1e-6
BLOCK = 128
WINDOW = 128
NEG = -1e30
LOG2E = math.log2(math.e)

SWA_WIDTH = 1024
SWA_HEAD_DIM = 64
SWA_Q_HEADS = 16
SWA_KV_HEADS = 2
SWA_GROUP = 8

MLA_WIDTH = 1024
MLA_V_DIM = 128
MLA_HEADS = 8
MLA_NOPE_DIM = 128
MLA_ROPE_DIM = 64
MLA_QK_DIM = 192
MLA_QK_PAD = 256
Q_LORA_RANK = 384
KV_LORA_RANK = 256
ROPE_THETA = 10000.0

A_Q = 1024
GATE_W = 2048
IN_WIDTH = 4032
LAT_W = Q_LORA_RANK + KV_LORA_RANK + 2 * MLA_ROPE_DIM
KA_W = SWA_KV_HEADS * SWA_HEAD_DIM
KK_W = 2 * KA_W
VT_ROWS = SWA_KV_HEADS * SWA_HEAD_DIM
ROW_W = GATE_W + LAT_W + KA_W
COL_W = A_Q + VT_ROWS

SWA_QSCALE = SWA_HEAD_DIM ** -0.5 * LOG2E
MLA_QSCALE = MLA_QK_DIM ** -0.5 * LOG2E

VMEM_LIMIT = 56 * 1024 * 1024
NT_DIMS = (((1,), (1,)), ((), ()))


def _cparams(*sem):
    return pltpu.CompilerParams(dimension_semantics=sem, vmem_limit_bytes=VMEM_LIMIT)


def _resident(shape):
    nd = len(shape)
    return pl.BlockSpec(shape, lambda *_: (0,) * nd, pipeline_mode=pl.Buffered(1))


def _layer(stacked, l):
    tail = stacked.shape[1:]
    return pl.BlockSpec((None,) + tail, lambda *_: (l,) + (0,) * len(tail),
                        pipeline_mode=pl.Buffered(1))


def _rms(x, g):
    return x * lax.rsqrt(jnp.mean(x * x, axis=-1, keepdims=True) + EPS) * g


_INPROJ_TM = 512
_INPROJ_TN = 512
_VT_TILE = 256
_VT_AUG = MLA_V_DIM + 16
_LAT0 = GATE_W


def _inproj_kernel(x_ref, g_ref, wr_ref, wc_ref, gq_ref, gkv_ref, wqt_ref, wk_ref, wvt_ref,
                   cos_ref, sin_ref, cost_ref, sint_ref,
                   gate_ref, kk_ref, qat_ref, vat_ref, qt_ref, k_ref, vt_ref):
    tm = x_ref.shape[0]
    h = _rms(x_ref[...], g_ref[...]).astype(BF16)

    lat_a = jnp.dot(h, wr_ref[:, _LAT0:_LAT0 + 512], preferred_element_type=F32)
    lat_b = jnp.dot(h, wr_ref[:, _LAT0 + 512:ROW_W], preferred_element_type=F32)
    cq = lat_a[:, 0:Q_LORA_RANK]
    ckv = jnp.concatenate([lat_a[:, Q_LORA_RANK:512], lat_b[:, 0:128]], axis=1)
    kslab = lat_b[:, 128:256]
    ka = lat_b[:, 256:384]
    ka_sw = pltpu.roll(ka, SWA_HEAD_DIM, 1)
    low = lax.broadcasted_iota(jnp.int32, ka.shape, 1) < SWA_HEAD_DIM
    kk_ref[:, 0:KA_W] = jnp.where(low, ka, ka_sw).astype(BF16)
    kk_ref[:, KA_W:KK_W] = jnp.where(low, ka_sw, ka).astype(BF16)

    for c in range(GATE_W // _INPROJ_TN):
        cols = slice(c * _INPROJ_TN, (c + 1) * _INPROJ_TN)
        r = jnp.dot(h, wr_ref[:, cols], preferred_element_type=F32)
        gate_ref[:, cols] = (r * (1.0 / (1.0 + jnp.exp(-r)))).astype(BF16)
    for c in range(A_Q // _INPROJ_TN):
        rows = slice(c * _INPROJ_TN, (c + 1) * _INPROJ_TN)
        r = lax.dot_general(wc_ref[rows, :], h, NT_DIMS, preferred_element_type=F32)
        qat_ref[rows, :] = (r * SWA_QSCALE).astype(BF16)
    vat_ref[...] = lax.dot_general(wc_ref[A_Q:COL_W, :], h, NT_DIMS,
                                   preferred_element_type=F32).astype(BF16)

    cqn = _rms(cq, gq_ref[...]).astype(BF16)
    ckvn = _rms(ckv, gkv_ref[...]).astype(BF16)
    krot = kslab * cos_ref[...] + pltpu.roll(kslab, MLA_ROPE_DIM, 1) * sin_ref[...]
    krot = jnp.where(low, krot, 0.0).astype(BF16)
    knope = jnp.dot(ckvn, wk_ref[...], preferred_element_type=F32)
    for hd in range(MLA_HEADS):
        k_ref[0, hd, :, 0:128] = knope[:, hd * 128:(hd + 1) * 128].astype(BF16)
        k_ref[0, hd, :, 128:256] = krot

    vt = lax.dot_general(wvt_ref[...], ckvn, NT_DIMS, preferred_element_type=F32)
    ones = jnp.ones((_VT_AUG - MLA_V_DIM, _VT_TILE), BF16)
    for hd in range(MLA_HEADS):
        for c in range(tm // _VT_TILE):
            vt_ref[0, hd, c, 0:MLA_V_DIM] = vt[hd * 128:(hd + 1) * 128, c * _VT_TILE:(c + 1) * _VT_TILE].astype(BF16)
            vt_ref[0, hd, c, MLA_V_DIM:_VT_AUG] = ones

    qt = lax.dot_general(wqt_ref[...], cqn, NT_DIMS, preferred_element_type=F32)
    cost = cost_ref[...]
    sint = sint_ref[...]
    pad = jnp.zeros((MLA_QK_PAD - MLA_QK_DIM, tm), BF16)
    for hd in range(MLA_HEADS):
        r0 = 1024 + hd * MLA_ROPE_DIM
        qr = (qt[r0:r0 + MLA_ROPE_DIM] * cost + qt[r0 + 512:r0 + 512 + MLA_ROPE_DIM] * sint) * MLA_QSCALE
        qt_ref[0, hd, 0, 0:128, :] = (qt[hd * 128:(hd + 1) * 128] * MLA_QSCALE).astype(BF16)
        qt_ref[0, hd, 0, 128:192, :] = qr.astype(BF16)
        qt_ref[0, hd, 0, 192:256, :] = pad


def _inproj(x2d, g, w_row, w_col, gq, gkv, wqt, wk, wvt, cos2, sin2, cost, sint, l, b, s):
    t = x2d.shape[0]
    tm = _INPROJ_TM
    nt = s // tm
    nvt = tm // _VT_TILE
    return pl.pallas_call(
        _inproj_kernel,
        grid=(t // tm,),
        in_specs=[
            pl.BlockSpec((tm, D_MODEL), lambda i: (i, 0)),
            _layer(g, l),
            _layer(w_row, l),
            _layer(w_col, l),
            _layer(gq, l),
            _layer(gkv, l),
            _layer(wqt, l),
            _layer(wk, l),
            _layer(wvt, l),
            pl.BlockSpec((tm, 128), lambda i: (i % nt, 0)),
            pl.BlockSpec((tm, 128), lambda i: (i % nt, 0)),
            pl.BlockSpec((MLA_ROPE_DIM, tm), lambda i: (0, i % nt)),
            pl.BlockSpec((MLA_ROPE_DIM, tm), lambda i: (0, i % nt)),
        ],
        out_specs=[
            pl.BlockSpec((tm, GATE_W), lambda i: (i, 0)),
            pl.BlockSpec((tm, KK_W), lambda i: (i, 0)),
            pl.BlockSpec((A_Q, tm), lambda i: (0, i)),
            pl.BlockSpec((VT_ROWS, tm), lambda i: (0, i)),
            pl.BlockSpec((1, MLA_HEADS, 1, MLA_QK_PAD, tm), lambda i: (i // nt, 0, i % nt, 0, 0)),
            pl.BlockSpec((1, MLA_HEADS, tm, MLA_QK_PAD), lambda i: (i // nt, 0, i % nt, 0)),
            pl.BlockSpec((1, MLA_HEADS, nvt, _VT_AUG, _VT_TILE), lambda i: (i // nt, 0, i % nt, 0, 0)),
        ],
        out_shape=[
            jax.ShapeDtypeStruct((t, GATE_W), BF16),
            jax.ShapeDtypeStruct((t, KK_W), BF16),
            jax.ShapeDtypeStruct((A_Q, t), BF16),
            jax.ShapeDtypeStruct((VT_ROWS, t), BF16),
            jax.ShapeDtypeStruct((b, MLA_HEADS, nt, MLA_QK_PAD, tm), BF16),
            jax.ShapeDtypeStruct((b, MLA_HEADS, s, MLA_QK_PAD), BF16),
            jax.ShapeDtypeStruct((b, MLA_HEADS, s // _VT_TILE, _VT_AUG, _VT_TILE), BF16),
        ],
        compiler_params=_cparams("parallel"),
        name="inproj",
    )(x2d, g, w_row, w_col, gq, gkv, wqt, wk, wvt, cos2, sin2, cost, sint)


_MLA_TQ = 512
_MLA_HP = 2
_MLA_LOOP_STAGES = 4


def _mla_tile_order(nq):
    off_diag = [(qi, j) for qi in range(nq) for j in range(qi)]
    return off_diag, [(qi, qi) for qi in range(nq)]


def _mla_attn_kernel(order_ref, qt_ref, k_ref, vt_ref, o_ref, *scratch):
    m_refs = scratch[0:_MLA_HP]
    acc_refs = scratch[_MLA_HP:2 * _MLA_HP]
    st_ref, mx_ref = scratch[2 * _MLA_HP:]
    nq, tq = qt_ref.shape[2], qt_ref.shape[4]
    for hh in range(_MLA_HP):
        m_refs[hh][...] = jnp.full_like(m_refs[hh], NEG)
        acc_refs[hh][...] = jnp.zeros_like(acc_refs[hh])
    nsub = tq // _VT_TILE

    def scores(hh, qi, j, slot):
        kt = k_ref[0, hh, pl.ds(pl.multiple_of(j * tq, tq), tq), :]
        st = jnp.dot(kt, qt_ref[0, hh, qi], preferred_element_type=F32)
        st_ref[slot, hh] = st
        mx_ref[slot, hh] = jnp.max(st, axis=0, keepdims=True)

    def update(hh, qi, j, slot):
        m_old = m_refs[hh][qi]
        m_new = jnp.maximum(m_old, mx_ref[slot, hh])
        alpha = jnp.exp2(m_old - m_new)
        pb = jnp.exp2(st_ref[slot, hh] - m_new).astype(BF16)
        m_refs[hh][qi] = m_new
        pv = jnp.dot(vt_ref[0, hh, j * nsub], pb[0:_VT_TILE], preferred_element_type=F32)
        for c in range(1, nsub):
            pv += jnp.dot(vt_ref[0, hh, j * nsub + c], pb[c * _VT_TILE:(c + 1) * _VT_TILE],
                          preferred_element_type=F32)
        acc_refs[hh][qi] = alpha * acc_refs[hh][qi] + pv

    hq = _VT_TILE
    assert tq == 2 * hq

    def causal(block):
        kpos = lax.broadcasted_iota(jnp.int32, block.shape, 0)
        qpos = lax.broadcasted_iota(jnp.int32, block.shape, 1)
        return jnp.where(kpos <= qpos, block, NEG)

    def scores_diag(hh, qi, slot):
        base = qi * tq
        qt = qt_ref[0, hh, qi]
        top = jnp.dot(k_ref[0, hh, base:base + hq, :], qt, preferred_element_type=F32)
        top = jnp.concatenate([causal(top[:, 0:hq]), top[:, hq:tq]], axis=1)
        bot = causal(jnp.dot(k_ref[0, hh, base + hq:base + tq, :], qt[:, hq:tq],
                             preferred_element_type=F32))
        st_ref[slot, hh, 0:hq, :] = top
        st_ref[slot, hh, hq:tq, hq:tq] = bot
        mtop = jnp.max(top, axis=0, keepdims=True)
        mbot = jnp.max(bot, axis=0, keepdims=True)
        mx_ref[slot, hh] = jnp.concatenate([mtop[:, 0:hq], jnp.maximum(mtop[:, hq:tq], mbot)], axis=1)

    def update_diag(hh, qi, slot):
        m_old = m_refs[hh][qi]
        m_new = jnp.maximum(m_old, mx_ref[slot, hh])
        alpha = jnp.exp2(m_old - m_new)
        m_refs[hh][qi] = m_new
        p_top = jnp.exp2(st_ref[slot, hh, 0:hq, :] - m_new).astype(BF16)
        p_bot = jnp.exp2(st_ref[slot, hh, hq:tq, hq:tq] - m_new[:, hq:tq]).astype(BF16)
        pv = jnp.dot(vt_ref[0, hh, qi * nsub], p_top, preferred_element_type=F32)
        pv_r = jnp.dot(vt_ref[0, hh, qi * nsub + 1], p_bot, preferred_element_type=F32)
        acc = acc_refs[hh]
        acc[qi, :, 0:hq] = alpha[:, 0:hq] * acc[qi, :, 0:hq] + pv[:, 0:hq]
        acc[qi, :, hq:tq] = alpha[:, hq:tq] * acc[qi, :, hq:tq] + (pv[:, hq:tq] + pv_r)

    def stage(cur, nxt, slot, diag_cur, diag_next):
        for hh in range(_MLA_HP):
            if nxt is not None and diag_next:
                scores_diag(hh, nxt[0], 1 - slot)
            elif nxt is not None:
                scores(hh, nxt[0], nxt[1], 1 - slot)
            if diag_cur:
                update_diag(hh, cur[0], slot)
            else:
                update(hh, cur[0], cur[1], slot)

    off_diag, diag = _mla_tile_order(nq)
    n_loop = (len(off_diag) - 1) // _MLA_LOOP_STAGES * _MLA_LOOP_STAGES
    for hh in range(_MLA_HP):
        scores(hh, off_diag[0][0], off_diag[0][1], 0)

    def body(t, carry):
        for u in range(_MLA_LOOP_STAGES):
            g = t * _MLA_LOOP_STAGES + u
            cur = (order_ref[0, g], order_ref[1, g])
            nxt = (order_ref[0, g + 1], order_ref[1, g + 1])
            stage(cur, nxt, u % 2, False, False)
        return carry

    lax.fori_loop(0, n_loop // _MLA_LOOP_STAGES, body, 0)
    tail = off_diag[n_loop:] + diag
    for i, cur in enumerate(tail):
        g = n_loop + i
        nxt = tail[i + 1] if i + 1 < len(tail) else None
        stage(cur, nxt, g % 2, cur[0] == cur[1], nxt is not None and nxt[0] == nxt[1])

    for qi in range(nq):
        for hh in range(_MLA_HP):
            out = acc_refs[hh][qi, 0:MLA_V_DIM] / acc_refs[hh][qi, MLA_V_DIM:MLA_V_DIM + 1]
            o_ref[0, qi * tq:(qi + 1) * tq, hh * MLA_V_DIM:(hh + 1) * MLA_V_DIM] = out.T.astype(o_ref.dtype)


def _mla_attn(qt, k, vt):
    b, h, nq, _, tq = qt.shape
    s = nq * tq
    hp = _MLA_HP
    off_diag, _ = _mla_tile_order(nq)
    order = jnp.asarray(list(zip(*off_diag)), jnp.int32)
    return pl.pallas_call(
        _mla_attn_kernel,
        grid=(b, h // hp),
        in_specs=[
            pl.BlockSpec(memory_space=pltpu.SMEM),
            pl.BlockSpec((1, hp, nq, MLA_QK_PAD, tq), lambda bi, hi: (bi, hi, 0, 0, 0)),
            pl.BlockSpec((1, hp, s, MLA_QK_PAD), lambda bi, hi: (bi, hi, 0, 0)),
            pl.BlockSpec((1, hp, s // _VT_TILE, _VT_AUG, _VT_TILE), lambda bi, hi: (bi, hi, 0, 0, 0)),
        ],
        out_specs=pl.BlockSpec((1, s, hp * MLA_V_DIM), lambda bi, hi: (bi, 0, hi)),
        out_shape=jax.ShapeDtypeStruct((b, s, MLA_WIDTH), BF16),
        scratch_shapes=(
            [pltpu.VMEM((nq, 1, tq), F32) for _ in range(hp)]
            + [pltpu.VMEM((nq, _VT_AUG, tq), F32) for _ in range(hp)]
            + [pltpu.VMEM((2, hp, tq, tq), F32), pltpu.VMEM((2, hp, 1, tq), F32)]
        ),
        compiler_params=_cparams("parallel", "parallel"),
        name="mla_attn",
    )(order, qt, k, vt)


_SWA_TQ = 512
_SWA_NB = _SWA_TQ // BLOCK
_SWA_GW = SWA_GROUP * BLOCK


def _swa_kernel(sink_ref, bias_ref, qt_ref, kc_ref, kp_ref, vc_ref, vp_ref, o_ref, kbuf_ref, vbuf_ref):
    tq = kc_ref.shape[1]
    kbuf_ref[0:BLOCK] = kp_ref[0]
    kbuf_ref[BLOCK:BLOCK + tq] = kc_ref[0]
    vbuf_ref[:, 0:BLOCK] = vp_ref[...]
    vbuf_ref[:, BLOCK:BLOCK + tq] = vc_ref[...]
    first_variant = jnp.where(pl.program_id(1) == 0, 1, 0)
    half = SWA_HEAD_DIM
    zeros = jnp.zeros((half, BLOCK), BF16)
    ones = jnp.ones((16, 2 * BLOCK), BF16)
    r_io =lax.broadcasted_iota(jnp.int32, (BLOCK, _SWA_GW), 0)
    t_io = lax.broadcasted_iota(jnp.int32, (BLOCK, _SWA_GW), 1) & (BLOCK - 1)
    prev_visible = t_io < r_io

    def scores(blk, hk):
        r0 = blk * BLOCK
        kwin = kbuf_ref[r0:r0 + 2 * BLOCK, hk * BLOCK:(hk + 1) * BLOCK]
        cols = []
        for p in range(SWA_GROUP // 2):
            row = hk * (SWA_GROUP * half) + p * BLOCK
            qt = qt_ref[row:row + BLOCK, r0:r0 + BLOCK]
            cols.append(jnp.concatenate([qt[0:half], zeros], axis=0))
            cols.append(jnp.concatenate([zeros, qt[half:BLOCK]], axis=0))
        qz = jnp.concatenate(cols, axis=1)
        st = jnp.dot(kwin, qz, preferred_element_type=F32)
        variant = first_variant if blk == 0 else 0
        return jnp.where(prev_visible, st[0:BLOCK], st[BLOCK:2 * BLOCK]) + bias_ref[variant, hk]

    def finish(blk, hk, s2):
        r0 = blk * BLOCK
        sink = sink_ref[hk]
        m = jnp.maximum(jnp.max(s2, axis=0, keepdims=True), sink)
        p = jnp.exp2(s2 - m)
        pstack = jnp.concatenate([jnp.where(prev_visible, p, 0.0).astype(BF16),
                                  jnp.where(prev_visible, 0.0, p).astype(BF16)], axis=0)
        vwin = jnp.concatenate([vbuf_ref[hk * half:(hk + 1) * half, r0:r0 + 2 * BLOCK], ones], axis=0)
        ot = jnp.dot(vwin, pstack, preferred_element_type=F32)
        l = ot[half:half + 1] + jnp.exp2(sink - m)
        ot = ot[0:half] * (1.0 / l)
        for pr in range(SWA_GROUP // 2):
            c0 = 2 * pr * BLOCK
            pair_t = jnp.concatenate([ot[:, c0:c0 + BLOCK], ot[:, c0 + BLOCK:c0 + 2 * BLOCK]], axis=0)
            col = hk * (SWA_GROUP * half) + pr * BLOCK
            o_ref[0, r0:r0 + BLOCK, col:col + BLOCK] = pair_t.T.astype(o_ref.dtype)

    chains = [(blk, hk) for blk in range(tq // BLOCK) for hk in range(SWA_KV_HEADS)]
    s_next = scores(*chains[0])
    for c, chain in enumerate(chains):
        s_cur = s_next
        if c + 1 < len(chains):
            s_next = scores(*chains[c + 1])
        finish(*chain, s_cur)


def _swa_attn(sink_rows, bias, qat, kk, vt, b, s):
    tq = _SWA_TQ
    nq = s // tq
    nb = _SWA_NB

    def prev_block(i):
        return jnp.maximum(i * nb - 1, 0)

    return pl.pallas_call(
        _swa_kernel,
        grid=(b, nq),
        in_specs=[
            _resident(sink_rows.shape),
            _resident(bias.shape),
            pl.BlockSpec((A_Q, tq), lambda bi, i: (0, bi * nq + i)),
            pl.BlockSpec((1, tq, KK_W), lambda bi, i: (bi, i, 0)),
            pl.BlockSpec((1, BLOCK, KK_W), lambda bi, i: (bi, prev_block(i), 0)),
            pl.BlockSpec((VT_ROWS, tq), lambda bi, i: (0, bi * nq + i)),
            pl.BlockSpec((VT_ROWS, BLOCK), lambda bi, i: (0, bi * nq * nb + prev_block(i))),
        ],
        out_specs=pl.BlockSpec((1, tq, SWA_WIDTH), lambda bi, i: (bi, i, 0)),
        out_shape=jax.ShapeDtypeStruct((b, s, SWA_WIDTH), BF16),
        scratch_shapes=[
            pltpu.VMEM((tq + BLOCK, KK_W), BF16),
            pltpu.VMEM((VT_ROWS, tq + BLOCK), BF16),
        ],
        compiler_params=_cparams("parallel", "parallel"),
        name="swa_attn",
    )(sink_rows, bias, qat, kk.reshape(b, s, KK_W), kk.reshape(b, s, KK_W), vt, vt)


def _swa_tables(swa_sinks):
    row = jnp.arange(BLOCK, dtype=jnp.int32)[:, None]
    tcol = jnp.arange(BLOCK, dtype=jnp.int32)[None, :]
    delta = jnp.where(tcol < row, BLOCK + tcol - row, tcol - row)
    slopes = jnp.exp2(-8.0 * jnp.arange(1, SWA_Q_HEADS + 1, dtype=F32) / SWA_Q_HEADS)
    slopes = slopes.reshape(SWA_KV_HEADS, SWA_GROUP)
    has_prev = -(slopes * LOG2E)[:, :, None, None] * delta.astype(F32)[None, None]
    no_prev = jnp.where((tcol < row)[None, None], NEG, has_prev)
    bias = jnp.stack([has_prev, no_prev])
    bias = bias.transpose(0, 1, 3, 2, 4).reshape(2, SWA_KV_HEADS, BLOCK, _SWA_GW)
    sinks2 = swa_sinks.astype(F32) * LOG2E
    sink_rows = jnp.repeat(sinks2.reshape(DEPTH, SWA_KV_HEADS, 1, SWA_GROUP), BLOCK, axis=-1)
    return bias, sink_rows


_OUTPROJ_TM = 512
_OUTPROJ_TN = 512


def _outproj_kernel(*refs, final):
    if final:
        ya_ref, yb_ref, gate_ref, x_ref, w_ref, gfin_ref, o_ref = refs
    else:
        ya_ref, yb_ref, gate_ref, x_ref, w_ref, o_ref = refs
    gated = jnp.concatenate([ya_ref[...], yb_ref[...]], axis=-1) * gate_ref[...]
    for c in range(D_MODEL // _OUTPROJ_TN):
        sl = slice(c * _OUTPROJ_TN, (c + 1) * _OUTPROJ_TN)
        o_ref[:, sl] = x_ref[:, sl] + jnp.dot(gated, w_ref[:, sl], preferred_element_type=F32)
    if final:
        o_ref[...] = _rms(o_ref[...], gfin_ref[...])


def _outproj(ya, yb, gate, x2d, w, l, final_g=None):
    t = x2d.shape[0]
    tm = _OUTPROJ_TM
    final = final_g is not None
    in_specs = [
        pl.BlockSpec((tm, SWA_WIDTH), lambda i: (i, 0)),
        pl.BlockSpec((tm, MLA_WIDTH), lambda i: (i, 0)),
        pl.BlockSpec((tm, GATE_W), lambda i: (i, 0)),
        pl.BlockSpec((tm, D_MODEL), lambda i: (i, 0)),
        _layer(w, l),
    ]
    args = [ya, yb, gate, x2d, w]
    if final:
        in_specs.append(_resident(final_g.shape))
        args.append(final_g)
    return pl.pallas_call(
        functools.partial(_outproj_kernel, final=final),
        grid=(t // tm,),
        in_specs=in_specs,
        out_specs=pl.BlockSpec((tm, D_MODEL), lambda i: (i, 0)),
        out_shape=jax.ShapeDtypeStruct((t, D_MODEL), F32),
        compiler_params=_cparams("parallel"),
        name="outproj_final" if final else "outproj",
    )(*args)


def _swap_halves(w):
    half = w.shape[-1] // 2
    return jnp.concatenate([w[..., half:], w[..., :half]], axis=-1)


_PREP_TR = 256


def _prep_w_in_kernel(w_ref, row_ref, col_ref):
    def rows(lo, n):
        return w_ref[0, lo:lo + n, :]

    o = 0
    qa = rows(o, A_Q); o += A_Q
    ka = rows(o, KA_W); o += KA_W
    va = rows(o, VT_ROWS); o += VT_ROWS
    ga = rows(o, 1024); o += 1024
    cq = rows(o, Q_LORA_RANK); o += Q_LORA_RANK
    ckv = rows(o, KV_LORA_RANK); o += KV_LORA_RANK
    half = MLA_ROPE_DIM // 2
    kr = rows(o, MLA_ROPE_DIM)
    kr_sw = jnp.concatenate([rows(o + half, half), rows(o, half)], axis=0)
    o += MLA_ROPE_DIM
    gb = rows(o, 1024)
    col_ref[0, 0:A_Q, :] = qa.astype(BF16)
    col_ref[0, A_Q:COL_W, :] = va.astype(BF16)
    off = 0
    for piece in (ga, gb, cq, ckv, jnp.concatenate([kr, kr_sw, ka], axis=0)):
        n = piece.shape[0]
        row_ref[0, :, off:off + n] = piece.T.astype(BF16)
        off += n


def _prep_w_in(w_in):
    tr = _PREP_TR
    w_t = jnp.swapaxes(w_in, 1, 2)
    return pl.pallas_call(
        _prep_w_in_kernel,
        grid=(DEPTH, D_MODEL // tr),
        in_specs=[pl.BlockSpec((1, IN_WIDTH, tr), lambda l, i: (l, 0, i))],
        out_specs=[
            pl.BlockSpec((1, tr, ROW_W), lambda l, i: (l, i, 0)),
            pl.BlockSpec((1, COL_W, tr), lambda l, i: (l, 0, i)),
        ],
        out_shape=[
            jax.ShapeDtypeStruct((DEPTH, D_MODEL, ROW_W), BF16),
            jax.ShapeDtypeStruct((DEPTH, COL_W, D_MODEL), BF16),
        ],
        compiler_params=_cparams("parallel", "parallel"),
        name="prep_w_in",
    )(w_t)


def _prep_w_q(w_q_b):
    w = w_q_b.reshape(DEPTH, Q_LORA_RANK, MLA_HEADS, MLA_QK_DIM)
    nope = w[..., :MLA_NOPE_DIM].reshape(DEPTH, Q_LORA_RANK, MLA_HEADS * MLA_NOPE_DIM)
    rope = w[..., MLA_NOPE_DIM:]
    rope_sw = _swap_halves(rope).reshape(DEPTH, Q_LORA_RANK, MLA_HEADS * MLA_ROPE_DIM)
    rope = rope.reshape(DEPTH, Q_LORA_RANK, MLA_HEADS * MLA_ROPE_DIM)
    return jnp.swapaxes(jnp.concatenate([nope, rope, rope_sw], axis=-1), 1, 2).astype(BF16)


def _prep_w_kv(w_kv_b):
    w = w_kv_b.reshape(DEPTH, KV_LORA_RANK, MLA_HEADS, MLA_NOPE_DIM + MLA_V_DIM)
    wk = w[..., :MLA_NOPE_DIM].reshape(DEPTH, KV_LORA_RANK, MLA_HEADS * MLA_NOPE_DIM)
    wv = w[..., MLA_NOPE_DIM:].reshape(DEPTH, KV_LORA_RANK, MLA_HEADS * MLA_V_DIM)
    return wk.astype(BF16), jnp.swapaxes(wv, 1, 2).astype(BF16)


def _rope_tables(s):
    pos = jnp.arange(s, dtype=F32)
    inv_freq = ROPE_THETA ** (-jnp.arange(0, MLA_ROPE_DIM, 2, dtype=F32) / MLA_ROPE_DIM)
    ang = pos[:, None] * inv_freq[None, :]
    cos, sin = jnp.cos(ang), jnp.sin(ang)
    cc = jnp.concatenate([cos, cos], axis=-1)
    ss = jnp.concatenate([-sin, sin], axis=-1)
    return jnp.tile(cc, (1, 2)), jnp.tile(ss, (1, 2)), cc.T, ss.T


def kernel(x, attn_norm_g, w_in, swa_sinks, q_a_norm_g, kv_a_norm_g, w_q_b, w_kv_b, w_out, final_norm_g):
    b, s, d = x.shape
    t = b * s
    w_row, w_col = _prep_w_in(w_in)
    w_q_p = _prep_w_q(w_q_b)
    w_k_p, w_vt_p = _prep_w_kv(w_kv_b)
    w_out_p = w_out.astype(BF16)
    cos2, sin2, cost, sint = _rope_tables(s)
    swa_bias, sink_rows = _swa_tables(swa_sinks)
    g_attn = attn_norm_g[:, None, :]
    g_q = q_a_norm_g[:, None, :]
    g_kv = kv_a_norm_g[:, None, :]

    x2d = x.reshape(t, d)
    for l in range(DEPTH):
        gate, kk, qat, vat, qt, k, vt = _inproj(x2d, g_attn, w_row, w_col, g_q, g_kv, w_q_p, w_k_p, w_vt_p,
                                                cos2, sin2, cost, sint, l, b, s)
        ya = _swa_attn(sink_rows[l], swa_bias, qat, kk, vat, b, s)
        yb = _mla_attn(qt, k, vt)
        final_g = final_norm_g[None] if l == DEPTH - 1 else None
        x2d = _outproj(ya.reshape(t, SWA_WIDTH), yb.reshape(t, MLA_WIDTH), gate, x2d, w_out_p, l, final_g)
    return x2d.reshape(b, s, d)
```

```python
import functools
import math

import jax
import jax.numpy as jnp
from jax import lax
from jax.experimental import pallas as pl
from jax.experimental.pallas import tpu as pltpu

F32 = jnp.float32
BF16 = jnp.bfloat16

D_MODEL = 2048
DEPTH = 4
EPS = 1e-6
BLOCK = 128
WINDOW = 128
NEG = -1e30
LOG2E = math.log2(math.e)

SWA_WIDTH = 1024
SWA_HEAD_DIM = 64
SWA_Q_HEADS = 16
SWA_KV_HEADS = 2
SWA_GROUP = 8

MLA_WIDTH = 1024
MLA_V_DIM = 128
MLA_HEADS = 8
MLA_NOPE_DIM = 128
MLA_ROPE_DIM = 64
MLA_QK_DIM = 192
MLA_QK_PAD = 256
Q_LORA_RANK = 384
KV_LORA_RANK = 256
ROPE_THETA = 10000.0

A_Q = 1024
GATE_W = 2048
IN_WIDTH = 4032
LAT_W = Q_LORA_RANK + KV_LORA_RANK + 2 * MLA_ROPE_DIM
KA_W = SWA_KV_HEADS * SWA_HEAD_DIM
KK_W = 2 * KA_W
VT_ROWS = SWA_KV_HEADS * SWA_HEAD_DIM
ROW_W = GATE_W + LAT_W + KA_W
COL_W = A_Q + VT_ROWS

SWA_QSCALE = SWA_HEAD_DIM ** -0.5 * LOG2E
MLA_QSCALE = MLA_QK_DIM ** -0.5 * LOG2E

VMEM_LIMIT = 56 * 1024 * 1024
NT_DIMS = (((1,), (1,)), ((), ()))


def _cparams(*sem):
    return pltpu.CompilerParams(dimension_semantics=sem, vmem_limit_bytes=VMEM_LIMIT)


def _resident(shape):
    nd = len(shape)
    return pl.BlockSpec(shape, lambda *_: (0,) * nd, pipeline_mode=pl.Buffered(1))


def _layer(stacked, l):
    tail = stacked.shape[1:]
    return pl.BlockSpec((None,) + tail, lambda *_: (l,) + (0,) * len(tail),
                        pipeline_mode=pl.Buffered(1))


def _rms(x, g):
    return x * lax.rsqrt(jnp.mean(x * x, axis=-1, keepdims=True) + EPS) * g


_INPROJ_TM = 512
_INPROJ_TN = 512
_VT_TILE = 256
_VT_AUG = MLA_V_DIM + 16
_LAT0 = GATE_W


def _inproj_kernel(x_ref, g_ref, wr_ref, wc_ref, gq_ref, gkv_ref, wqt_ref, wk_ref, wvt_ref,
                   cos_ref, sin_ref, cost_ref, sint_ref,
                   gate_ref, kk_ref, qat_ref, vat_ref, qt_ref, k_ref, vt_ref):
    tm = x_ref.shape[0]
    h = _rms(x_ref[...], g_ref[...]).astype(BF16)

    lat_a = jnp.dot(h, wr_ref[:, _LAT0:_LAT0 + 512], preferred_element_type=F32)
    lat_b = jnp.dot(h, wr_ref[:, _LAT0 + 512:ROW_W], preferred_element_type=F32)
    cq = lat_a[:, 0:Q_LORA_RANK]
    ckv = jnp.concatenate([lat_a[:, Q_LORA_RANK:512], lat_b[:, 0:128]], axis=1)
    kslab = lat_b[:, 128:256]
    ka = lat_b[:, 256:384]
    ka_sw = pltpu.roll(ka, SWA_HEAD_DIM, 1)
    low = lax.broadcasted_iota(jnp.int32, ka.shape, 1) < SWA_HEAD_DIM
    kk_ref[:, 0:KA_W] = jnp.where(low, ka, ka_sw).astype(BF16)
    kk_ref[:, KA_W:KK_W] = jnp.where(low, ka_sw, ka).astype(BF16)

    for c in range(GATE_W // _INPROJ_TN):
        cols = slice(c * _INPROJ_TN, (c + 1) * _INPROJ_TN)
        r = jnp.dot(h, wr_ref[:, cols], preferred_element_type=F32)
        gate_ref[:, cols] = (r * (1.0 / (1.0 + jnp.exp(-r)))).astype(BF16)
    for c in range(A_Q // _INPROJ_TN):
        rows = slice(c * _INPROJ_TN, (c + 1) * _INPROJ_TN)
        r = lax.dot_general(wc_ref[rows, :], h, NT_DIMS, preferred_element_type=F32)
        qat_ref[rows, :] = (r * SWA_QSCALE).astype(BF16)
    vat_ref[...] = lax.dot_general(wc_ref[A_Q:COL_W, :], h, NT_DIMS,
                                   preferred_element_type=F32).astype(BF16)

    cqn = _rms(cq, gq_ref[...]).astype(BF16)
    ckvn = _rms(ckv, gkv_ref[...]).astype(BF16)
    krot = kslab * cos_ref[...] + pltpu.roll(kslab, MLA_ROPE_DIM, 1) * sin_ref[...]
    krot = jnp.where(low, krot, 0.0).astype(BF16)
    knope = jnp.dot(ckvn, wk_ref[...], preferred_element_type=F32)
    for hd in range(MLA_HEADS):
        k_ref[0, hd, :, 0:128] = knope[:, hd * 128:(hd + 1) * 128].astype(BF16)
        k_ref[0, hd, :, 128:256] = krot

    vt = lax.dot_general(wvt_ref[...], ckvn, NT_DIMS, preferred_element_type=F32)
    ones = jnp.ones((_VT_AUG - MLA_V_DIM, _VT_TILE), BF16)
    for hd in range(MLA_HEADS):
        for c in range(tm // _VT_TILE):
            vt_ref[0, hd, c, 0:MLA_V_DIM] = vt[hd * 128:(hd + 1) * 128, c * _VT_TILE:(c + 1) * _VT_TILE].astype(BF16)
            vt_ref[0, hd, c, MLA_V_DIM:_VT_AUG] = ones

    qt = lax.dot_general(wqt_ref[...], cqn, NT_DIMS, preferred_element_type=F32)
    cost = cost_ref[...]
    sint = sint_ref[...]
    pad = jnp.zeros((MLA_QK_PAD - MLA_QK_DIM, tm), BF16)
    for hd in range(MLA_HEADS):
        r0 = 1024 + hd * MLA_ROPE_DIM
        qr = (qt[r0:r0 + MLA_ROPE_DIM] * cost + qt[r0 + 512:r0 + 512 + MLA_ROPE_DIM] * sint) * MLA_QSCALE
        qt_ref[0, hd, 0, 0:128, :] = (qt[hd * 128:(hd + 1) * 128] * MLA_QSCALE).astype(BF16)
        qt_ref[0, hd, 0, 128:192, :] = qr.astype(BF16)
        qt_ref[0, hd, 0, 192:256, :] = pad


def _inproj(x2d, g, w_row, w_col, gq, gkv, wqt, wk, wvt, cos2, sin2, cost, sint, l, b, s):
    t = x2d.shape[0]
    tm = _INPROJ_TM
    nt = s // tm
    nvt = tm // _VT_TILE
    return pl.pallas_call(
        _inproj_kernel,
        grid=(t // tm,),
        in_specs=[
            pl.BlockSpec((tm, D_MODEL), lambda i: (i, 0)),
            _layer(g, l),
            _layer(w_row, l),
            _layer(w_col, l),
            _layer(gq, l),
            _layer(gkv, l),
            _layer(wqt, l),
            _layer(wk, l),
            _layer(wvt, l),
            pl.BlockSpec((tm, 128), lambda i: (i % nt, 0)),
            pl.BlockSpec((tm, 128), lambda i: (i % nt, 0)),
            pl.BlockSpec((MLA_ROPE_DIM, tm), lambda i: (0, i % nt)),
            pl.BlockSpec((MLA_ROPE_DIM, tm), lambda i: (0, i % nt)),
        ],
        out_specs=[
            pl.BlockSpec((tm, GATE_W), lambda i: (i, 0)),
            pl.BlockSpec((tm, KK_W), lambda i: (i, 0)),
            pl.BlockSpec((A_Q, tm), lambda i: (0, i)),
            pl.BlockSpec((VT_ROWS, tm), lambda i: (0, i)),
            pl.BlockSpec((1, MLA_HEADS, 1, MLA_QK_PAD, tm), lambda i: (i // nt, 0, i % nt, 0, 0)),
            pl.BlockSpec((1, MLA_HEADS, tm, MLA_QK_PAD), lambda i: (i // nt, 0, i % nt, 0)),
            pl.BlockSpec((1, MLA_HEADS, nvt, _VT_AUG, _VT_TILE), lambda i: (i // nt, 0, i % nt, 0, 0)),
        ],
        out_shape=[
            jax.ShapeDtypeStruct((t, GATE_W), BF16),
            jax.ShapeDtypeStruct((t, KK_W), BF16),
            jax.ShapeDtypeStruct((A_Q, t), BF16),
            jax.ShapeDtypeStruct((VT_ROWS, t), BF16),
            jax.ShapeDtypeStruct((b, MLA_HEADS, nt, MLA_QK_PAD, tm), BF16),
            jax.ShapeDtypeStruct((b, MLA_HEADS, s, MLA_QK_PAD), BF16),
            jax.ShapeDtypeStruct((b, MLA_HEADS, s // _VT_TILE, _VT_AUG, _VT_TILE), BF16),
        ],
        compiler_params=_cparams("parallel"),
        name="inproj",
    )(x2d, g, w_row, w_col, gq, gkv, wqt, wk, wvt, cos2, sin2, cost, sint)


_MLA_TQ = 512
_MLA_HP = 2
_MLA_LOOP_STAGES = 4


def _mla_tile_order(nq):
    off_diag = [(qi, j) for qi in range(nq) for j in range(qi)]
    return off_diag, [(qi, qi) for qi in range(nq)]


def _mla_attn_kernel(order_ref, qt_ref, k_ref, vt_ref, gate_ref, o_ref, *scratch):
    m_refs = scratch[0:_MLA_HP]
    acc_refs = scratch[_MLA_HP:2 * _MLA_HP]
    st_ref, mx_ref = scratch[2 * _MLA_HP:]
    nq, tq = qt_ref.shape[2], qt_ref.shape[4]
    for hh in range(_MLA_HP):
        m_refs[hh][...] = jnp.full_like(m_refs[hh], NEG)
        acc_refs[hh][...] = jnp.zeros_like(acc_refs[hh])
    nsub = tq // _VT_TILE

    def scores(hh, qi, j, slot):
        kt = k_ref[0, hh, pl.ds(pl.multiple_of(j * tq, tq), tq), :]
        st = jnp.dot(kt, qt_ref[0, hh, qi], preferred_element_type=F32)
        st_ref[slot, hh] = st
        mx_ref[slot, hh] = jnp.max(st, axis=0, keepdims=True)

    def update(hh, qi, j, slot):
        m_old = m_refs[hh][qi]
        m_new = jnp.maximum(m_old, mx_ref[slot, hh])
        alpha = jnp.exp2(m_old - m_new)
        pb = jnp.exp2(st_ref[slot, hh] - m_new).astype(BF16)
        m_refs[hh][qi] = m_new
        pv = jnp.dot(vt_ref[0, hh, j * nsub], pb[0:_VT_TILE], preferred_element_type=F32)
        for c in range(1, nsub):
            pv += jnp.dot(vt_ref[0, hh, j * nsub + c], pb[c * _VT_TILE:(c + 1) * _VT_TILE],
                          preferred_element_type=F32)
        acc_refs[hh][qi] = alpha * acc_refs[hh][qi] + pv

    hq = _VT_TILE
    assert tq == 2 * hq

    def causal(block):
        kpos = lax.broadcasted_iota(jnp.int32, block.shape, 0)
        qpos = lax.broadcasted_iota(jnp.int32, block.shape, 1)
        return jnp.where(kpos <= qpos, block, NEG)

    def scores_diag(hh, qi, slot):
        base = qi * tq
        qt = qt_ref[0, hh, qi]
        top = jnp.dot(k_ref[0, hh, base:base + hq, :], qt, preferred_element_type=F32)
        top = jnp.concatenate([causal(top[:, 0:hq]), top[:, hq:tq]], axis=1)
        bot = causal(jnp.dot(k_ref[0, hh, base + hq:base + tq, :], qt[:, hq:tq],
                             preferred_element_type=F32))
        st_ref[slot, hh, 0:hq, :] = top
        st_ref[slot, hh, hq:tq, hq:tq] = bot
        mtop = jnp.max(top, axis=0, keepdims=True)
        mbot = jnp.max(bot, axis=0, keepdims=True)
        mx_ref[slot, hh] = jnp.concatenate([mtop[:, 0:hq], jnp.maximum(mtop[:, hq:tq], mbot)], axis=1)

    def update_diag(hh, qi, slot):
        m_old = m_refs[hh][qi]
        m_new = jnp.maximum(m_old, mx_ref[slot, hh])
        alpha = jnp.exp2(m_old - m_new)
        m_refs[hh][qi] = m_new
        p_top = jnp.exp2(st_ref[slot, hh, 0:hq, :] - m_new).astype(BF16)
        p_bot = jnp.exp2(st_ref[slot, hh, hq:tq, hq:tq] - m_new[:, hq:tq]).astype(BF16)
        pv = jnp.dot(vt_ref[0, hh, qi * nsub], p_top, preferred_element_type=F32)
        pv_r = jnp.dot(vt_ref[0, hh, qi * nsub + 1], p_bot, preferred_element_type=F32)
        acc = acc_refs[hh]
        acc[qi, :, 0:hq] = alpha[:, 0:hq] * acc[qi, :, 0:hq] + pv[:, 0:hq]
        acc[qi, :, hq:tq] = alpha[:, hq:tq] * acc[qi, :, hq:tq] + (pv[:, hq:tq] + pv_r)

    def stage(cur, nxt, slot, diag_cur, diag_next):
        for hh in range(_MLA_HP):
            if nxt is not None and diag_next:
                scores_diag(hh, nxt[0], 1 - slot)
            elif nxt is not None:
                scores(hh, nxt[0], nxt[1], 1 - slot)
            if diag_cur:
                update_diag(hh, cur[0], slot)
            else:
                update(hh, cur[0], cur[1], slot)

    off_diag, diag = _mla_tile_order(nq)
    n_loop = (len(off_diag) - 1) // _MLA_LOOP_STAGES * _MLA_LOOP_STAGES
    for hh in range(_MLA_HP):
        scores(hh, off_diag[0][0], off_diag[0][1], 0)

    def body(t, carry):
        for u in range(_MLA_LOOP_STAGES):
            g = t * _MLA_LOOP_STAGES + u
            cur = (order_ref[0, g], order_ref[1, g])
            nxt = (order_ref[0, g + 1], order_ref[1, g + 1])
            stage(cur, nxt, u % 2, False, False)
        return carry

    lax.fori_loop(0, n_loop // _MLA_LOOP_STAGES, body, 0)
    tail = off_diag[n_loop:] + diag
    for i, cur in enumerate(tail):
        g = n_loop + i
        nxt = tail[i + 1] if i + 1 < len(tail) else None
        stage(cur, nxt, g % 2, cur[0] == cur[1], nxt is not None and nxt[0] == nxt[1])

    for qi in range(nq):
        for hh in range(_MLA_HP):
            out = acc_refs[hh][qi, 0:MLA_V_DIM] / acc_refs[hh][qi, MLA_V_DIM:MLA_V_DIM + 1]
            rows, cols = slice(qi * tq, (qi + 1) * tq), slice(hh * MLA_V_DIM, (hh + 1) * MLA_V_DIM)
            o_ref[0, rows, cols] = out.T.astype(BF16) * gate_ref[0, rows, cols]


def _mla_attn(qt, k, vt, gate):
    b, h, nq, _, tq = qt.shape
    s = nq * tq
    hp = _MLA_HP
    off_diag, _ = _mla_tile_order(nq)
    order = jnp.asarray(list(zip(*off_diag)), jnp.int32)
    return pl.pallas_call(
        _mla_attn_kernel,
        grid=(b, h // hp),
        in_specs=[
            pl.BlockSpec(memory_space=pltpu.SMEM),
            pl.BlockSpec((1, hp, nq, MLA_QK_PAD, tq), lambda bi, hi: (bi, hi, 0, 0, 0)),
            pl.BlockSpec((1, hp, s, MLA_QK_PAD), lambda bi, hi: (bi, hi, 0, 0)),
            pl.BlockSpec((1, hp, s // _VT_TILE, _VT_AUG, _VT_TILE), lambda bi, hi: (bi, hi, 0, 0, 0)),
            pl.BlockSpec((1, s, hp * MLA_V_DIM), lambda bi, hi: (bi, 0, SWA_WIDTH // (hp * MLA_V_DIM) + hi)),
        ],
        out_specs=pl.BlockSpec((1, s, hp * MLA_V_DIM), lambda bi, hi: (bi, 0, hi)),
        out_shape=jax.ShapeDtypeStruct((b, s, MLA_WIDTH), BF16),
        scratch_shapes=(
            [pltpu.VMEM((nq, 1, tq), F32) for _ in range(hp)]
            + [pltpu.VMEM((nq, _VT_AUG, tq), F32) for _ in range(hp)]
            + [pltpu.VMEM((2, hp, tq, tq), F32), pltpu.VMEM((2, hp, 1, tq), F32)]
        ),
        compiler_params=_cparams("parallel", "parallel"),
        name="mla_attn",
    )(order, qt, k, vt, gate)


_SWA_TQ = 512
_SWA_NB = _SWA_TQ // BLOCK
_SWA_GW = SWA_GROUP * BLOCK


def _swa_kernel(sink_ref, bias_ref, pmask_ref, qt_ref, kc_ref, kp_ref, vc_ref, vp_ref, gate_ref,
                o_ref, kbuf_ref, vbuf_ref):
    tq = kc_ref.shape[1]
    kbuf_ref[0:BLOCK] = kp_ref[0]
    kbuf_ref[BLOCK:BLOCK + tq] = kc_ref[0]
    vbuf_ref[:, 0:BLOCK] = vp_ref[...]
    vbuf_ref[:, BLOCK:BLOCK + tq] = vc_ref[...]
    first_variant = jnp.where(pl.program_id(1) == 0, 1, 0)
    half = SWA_HEAD_DIM
    zeros = jnp.zeros((half, BLOCK), BF16)
    ones = jnp.ones((16, 2 * BLOCK), BF16)
    r_io =lax.broadcasted_iota(jnp.int32, (BLOCK, _SWA_GW), 0)
    t_io = lax.broadcasted_iota(jnp.int32, (BLOCK, _SWA_GW), 1) & (BLOCK - 1)
    prev_visible = t_io < r_io

    def scores(blk, hk):
        r0 = blk * BLOCK
        kwin = kbuf_ref[r0:r0 + 2 * BLOCK, hk * BLOCK:(hk + 1) * BLOCK]
        cols = []
        for p in range(SWA_GROUP // 2):
            row = hk * (SWA_GROUP * half) + p * BLOCK
            qt = qt_ref[row:row + BLOCK, r0:r0 + BLOCK]
            cols.append(jnp.concatenate([qt[0:half], zeros], axis=0))
            cols.append(jnp.concatenate([zeros, qt[half:BLOCK]], axis=0))
        qz = jnp.concatenate(cols, axis=1)
        st = jnp.dot(kwin, qz, preferred_element_type=F32)
        variant = first_variant if blk == 0 else 0
        return jnp.where(prev_visible, st[0:BLOCK], st[BLOCK:2 * BLOCK]) + bias_ref[variant, hk]

    def finish(blk, hk, s2):
        r0 = blk * BLOCK
        sink = sink_ref[hk]
        m = jnp.maximum(jnp.max(s2, axis=0, keepdims=True), sink)
        pb = jnp.exp2(s2 - m).astype(BF16)
        p_prev = pb * pmask_ref[...]
        pstack = jnp.concatenate([p_prev, pb - p_prev], axis=0)
        vwin = jnp.concatenate([vbuf_ref[hk * half:(hk + 1) * half, r0:r0 + 2 * BLOCK], ones], axis=0)
        ot = jnp.dot(vwin, pstack, preferred_element_type=F32)
        l = ot[half:half + 1] + jnp.exp2(sink - m)
        ot = ot[0:half] * (1.0 / l)
        for pr in range(SWA_GROUP // 2):
            c0 = 2 * pr * BLOCK
            pair_t = jnp.concatenate([ot[:, c0:c0 + BLOCK], ot[:, c0 + BLOCK:c0 + 2 * BLOCK]], axis=0)
            col = hk * (SWA_GROUP * half) + pr * BLOCK
            gated = pair_t.T.astype(BF16) * gate_ref[r0:r0 + BLOCK, col:col + BLOCK]
            o_ref[0, r0:r0 + BLOCK, col:col + BLOCK] = gated

    chains = [(blk, hk) for blk in range(tq // BLOCK) for hk in range(SWA_KV_HEADS)]
    s_next = scores(*chains[0])
    for c, chain in enumerate(chains):
        s_cur = s_next
        if c + 1 < len(chains):
            s_next = scores(*chains[c + 1])
        finish(*chain, s_cur)


def _swa_attn(sink_rows, bias, pmask, qat, kk, vt, gate, b, s):
    tq = _SWA_TQ
    nq = s // tq
    nb = _SWA_NB

    def prev_block(i):
        return jnp.maximum(i * nb - 1, 0)

    return pl.pallas_call(
        _swa_kernel,
        grid=(b, nq),
        in_specs=[
            _resident(sink_rows.shape),
            _resident(bias.shape),
            _resident(pmask.shape),
            pl.BlockSpec((A_Q, tq), lambda bi, i: (0, bi * nq + i)),
            pl.BlockSpec((1, tq, KK_W), lambda bi, i: (bi, i, 0)),
            pl.BlockSpec((1, BLOCK, KK_W), lambda bi, i: (bi, prev_block(i), 0)),
            pl.BlockSpec((VT_ROWS, tq), lambda bi, i: (0, bi * nq + i)),
            pl.BlockSpec((VT_ROWS, BLOCK), lambda bi, i: (0, bi * nq * nb + prev_block(i))),
            pl.BlockSpec((tq, SWA_WIDTH), lambda bi, i: (bi * nq + i, 0)),
        ],
        out_specs=pl.BlockSpec((1, tq, SWA_WIDTH), lambda bi, i: (bi, i, 0)),
        out_shape=jax.ShapeDtypeStruct((b, s, SWA_WIDTH), BF16),
        scratch_shapes=[
            pltpu.VMEM((tq + BLOCK, KK_W), BF16),
            pltpu.VMEM((VT_ROWS, tq + BLOCK), BF16),
        ],
        compiler_params=_cparams("parallel", "parallel"),
        name="swa_attn",
    )(sink_rows, bias, pmask, qat, kk.reshape(b, s, KK_W), kk.reshape(b, s, KK_W), vt, vt, gate)


def _swa_tables(swa_sinks):
    row = jnp.arange(BLOCK, dtype=jnp.int32)[:, None]
    tcol = jnp.arange(BLOCK, dtype=jnp.int32)[None, :]
    delta = jnp.where(tcol < row, BLOCK + tcol - row, tcol - row)
    slopes = jnp.exp2(-8.0 * jnp.arange(1, SWA_Q_HEADS + 1, dtype=F32) / SWA_Q_HEADS)
    slopes = slopes.reshape(SWA_KV_HEADS, SWA_GROUP)
    has_prev = -(slopes * LOG2E)[:, :, None, None] * delta.astype(F32)[None, None]
    no_prev = jnp.where((tcol < row)[None, None], NEG, has_prev)
    bias = jnp.stack([has_prev, no_prev])
    bias = bias.transpose(0, 1, 3, 2, 4).reshape(2, SWA_KV_HEADS, BLOCK, _SWA_GW)
    pmask = jnp.tile((tcol < row).astype(BF16), (1, SWA_GROUP))
    sinks2 = swa_sinks.astype(F32) * LOG2E
    sink_rows = jnp.repeat(sinks2.reshape(DEPTH, SWA_KV_HEADS, 1, SWA_GROUP), BLOCK, axis=-1)
    return bias, pmask, sink_rows


_OUTPROJ_TM = 512
_OUTPROJ_TN = 512


def _outproj_kernel(*refs, final):
    if final:
        ya_ref, yb_ref, x_ref, w_ref, gfin_ref, o_ref = refs
    else:
        ya_ref, yb_ref, x_ref, w_ref, o_ref = refs
    gated = jnp.concatenate([ya_ref[...], yb_ref[...]], axis=-1)
    for c in range(D_MODEL // _OUTPROJ_TN):
        sl = slice(c * _OUTPROJ_TN, (c + 1) * _OUTPROJ_TN)
        o_ref[:, sl] = x_ref[:, sl] + jnp.dot(gated, w_ref[:, sl], preferred_element_type=F32)
    if final:
        o_ref[...] = _rms(o_ref[...], gfin_ref[...])


def _outproj(ya, yb, x2d, w, l, final_g=None):
    t = x2d.shape[0]
    tm = _OUTPROJ_TM
    final = final_g is not None
    in_specs = [
        pl.BlockSpec((tm, SWA_WIDTH), lambda i: (i, 0)),
        pl.BlockSpec((tm, MLA_WIDTH), lambda i: (i, 0)),
        pl.BlockSpec((tm, D_MODEL), lambda i: (i, 0)),
        _layer(w, l),
    ]
    args = [ya, yb, x2d, w]
    if final:
        in_specs.append(_resident(final_g.shape))
        args.append(final_g)
    return pl.pallas_call(
        functools.partial(_outproj_kernel, final=final),
        grid=(t // tm,),
        in_specs=in_specs,
        out_specs=pl.BlockSpec((tm, D_MODEL), lambda i: (i, 0)),
        out_shape=jax.ShapeDtypeStruct((t, D_MODEL), F32),
        compiler_params=_cparams("parallel"),
        name="outproj_final" if final else "outproj",
    )(*args)


def _swap_halves(w):
    half = w.shape[-1] // 2
    return jnp.concatenate([w[..., half:], w[..., :half]], axis=-1)


_PREP_TR = 256


def _prep_w_in_kernel(w_ref, row_ref, col_ref):
    def rows(lo, n):
        return w_ref[0, lo:lo + n, :]

    o = 0
    qa = rows(o, A_Q); o += A_Q
    ka = rows(o, KA_W); o += KA_W
    va = rows(o, VT_ROWS); o += VT_ROWS
    ga = rows(o, 1024); o += 1024
    cq = rows(o, Q_LORA_RANK); o += Q_LORA_RANK
    ckv = rows(o, KV_LORA_RANK); o += KV_LORA_RANK
    half = MLA_ROPE_DIM // 2
    kr = rows(o, MLA_ROPE_DIM)
    kr_sw = jnp.concatenate([rows(o + half, half), rows(o, half)], axis=0)
    o += MLA_ROPE_DIM
    gb = rows(o, 1024)
    col_ref[0, 0:A_Q, :] = qa.astype(BF16)
    col_ref[0, A_Q:COL_W, :] = va.astype(BF16)
    off = 0
    for piece in (ga, gb, cq, ckv, jnp.concatenate([kr, kr_sw, ka], axis=0)):
        n = piece.shape[0]
        row_ref[0, :, off:off + n] = piece.T.astype(BF16)
        off += n


def _prep_w_in(w_in):
    tr = _PREP_TR
    w_t = jnp.swapaxes(w_in, 1, 2)
    return pl.pallas_call(
        _prep_w_in_kernel,
        grid=(DEPTH, D_MODEL // tr),
        in_specs=[pl.BlockSpec((1, IN_WIDTH, tr), lambda l, i: (l, 0, i))],
        out_specs=[
            pl.BlockSpec((1, tr, ROW_W), lambda l, i: (l, i, 0)),
            pl.BlockSpec((1, COL_W, tr), lambda l, i: (l, 0, i)),
        ],
        out_shape=[
            jax.ShapeDtypeStruct((DEPTH, D_MODEL, ROW_W), BF16),
            jax.ShapeDtypeStruct((DEPTH, COL_W, D_MODEL), BF16),
        ],
        compiler_params=_cparams("parallel", "parallel"),
        name="prep_w_in",
    )(w_t)


def _prep_w_q(w_q_b):
    w = w_q_b.reshape(DEPTH, Q_LORA_RANK, MLA_HEADS, MLA_QK_DIM)
    nope = w[..., :MLA_NOPE_DIM].reshape(DEPTH, Q_LORA_RANK, MLA_HEADS * MLA_NOPE_DIM)
    rope = w[..., MLA_NOPE_DIM:]
    rope_sw = _swap_halves(rope).reshape(DEPTH, Q_LORA_RANK, MLA_HEADS * MLA_ROPE_DIM)
    rope = rope.reshape(DEPTH, Q_LORA_RANK, MLA_HEADS * MLA_ROPE_DIM)
    return jnp.swapaxes(jnp.concatenate([nope, rope, rope_sw], axis=-1), 1, 2).astype(BF16)


def _prep_w_kv(w_kv_b):
    w = w_kv_b.reshape(DEPTH, KV_LORA_RANK, MLA_HEADS, MLA_NOPE_DIM + MLA_V_DIM)
    wk = w[..., :MLA_NOPE_DIM].reshape(DEPTH, KV_LORA_RANK, MLA_HEADS * MLA_NOPE_DIM)
    wv = w[..., MLA_NOPE_DIM:].reshape(DEPTH, KV_LORA_RANK, MLA_HEADS * MLA_V_DIM)
    return wk.astype(BF16), jnp.swapaxes(wv, 1, 2).astype(BF16)


def _rope_tables(s):
    pos = jnp.arange(s, dtype=F32)
    inv_freq = ROPE_THETA ** (-jnp.arange(0, MLA_ROPE_DIM, 2, dtype=F32) / MLA_ROPE_DIM)
    ang = pos[:, None] * inv_freq[None, :]
    cos, sin = jnp.cos(ang), jnp.sin(ang)
    cc = jnp.concatenate([cos, cos], axis=-1)
    ss = jnp.concatenate([-sin, sin], axis=-1)
    return jnp.tile(cc, (1, 2)), jnp.tile(ss, (1, 2)), cc.T, ss.T


def kernel(x, attn_norm_g, w_in, swa_sinks, q_a_norm_g, kv_a_norm_g, w_q_b, w_kv_b, w_out, final_norm_g):
    b, s, d = x.shape
    t = b * s
    w_row, w_col = _prep_w_in(w_in)
    w_q_p = _prep_w_q(w_q_b)
    w_k_p, w_vt_p = _prep_w_kv(w_kv_b)
    w_out_p = w_out.astype(BF16)
    cos2, sin2, cost, sint = _rope_tables(s)
    swa_bias, swa_pmask, sink_rows = _swa_tables(swa_sinks)
    g_attn = attn_norm_g[:, None, :]
    g_q = q_a_norm_g[:, None, :]
    g_kv = kv_a_norm_g[:, None, :]

    x2d = x.reshape(t, d)
    for l in range(DEPTH):
        gate, kk, qat, vat, qt, k, vt = _inproj(x2d, g_attn, w_row, w_col, g_q, g_kv, w_q_p, w_k_p, w_vt_p,
                                                cos2, sin2, cost, sint, l, b, s)
        ya = _swa_attn(sink_rows[l], swa_bias, swa_pmask, qat, kk, vat, gate, b, s)
        yb = _mla_attn(qt, k, vt, gate.reshape(b, s, GATE_W))
        final_g = final_norm_g[None] if l == DEPTH - 1 else None
        x2d = _outproj(ya.reshape(t, SWA_WIDTH), yb.reshape(t, MLA_WIDTH), x2d, w_out_p, l, final_g)
    return x2d.reshape(b, s, d)
```

```python
import functools
import math

import jax
import jax.numpy as jnp
from jax import lax
from jax.experimental import pallas as pl
from jax.experimental.pallas import tpu as pltpu

F32 = jnp.float32
BF16 = jnp.bfloat16

D_MODEL = 2048
DEPTH = 4
EPS = 1e-6
BLOCK = 128
WINDOW = 128
NEG = -1e30
LOG2E = math.log2(math.e)

SWA_WIDTH = 1024
SWA_HEAD_DIM = 64
SWA_Q_HEADS = 16
SWA_KV_HEADS = 2
SWA_GROUP = 8

MLA_WIDTH = 1024
MLA_V_DIM = 128
MLA_HEADS = 8
MLA_NOPE_DIM = 128
MLA_ROPE_DIM = 64
MLA_QK_DIM = 192
MLA_QK_PAD = 256
Q_LORA_RANK = 384
KV_LORA_RANK = 256
ROPE_THETA = 10000.0

A_Q = 1024
GATE_W = 2048
IN_WIDTH = 4032
LAT_W = Q_LORA_RANK + KV_LORA_RANK + 2 * MLA_ROPE_DIM
KA_W = SWA_KV_HEADS * SWA_HEAD_DIM
KK_W = 2 * KA_W
VT_ROWS = SWA_KV_HEADS * SWA_HEAD_DIM
ROW_W = GATE_W + LAT_W + KA_W
COL_W = A_Q + VT_ROWS

SWA_QSCALE = SWA_HEAD_DIM ** -0.5 * LOG2E
MLA_QSCALE = MLA_QK_DIM ** -0.5 * LOG2E

VMEM_LIMIT = 56 * 1024 * 1024
NT_DIMS = (((1,), (1,)), ((), ()))


def _cparams(*sem):
    return pltpu.CompilerParams(dimension_semantics=sem, vmem_limit_bytes=VMEM_LIMIT)


def _resident(shape):
    nd = len(shape)
    return pl.BlockSpec(shape, lambda *_: (0,) * nd, pipeline_mode=pl.Buffered(1))


def _layer(stacked, l):
    tail = stacked.shape[1:]
    return pl.BlockSpec((None,) + tail, lambda *_: (l,) + (0,) * len(tail),
                        pipeline_mode=pl.Buffered(1))


def _rms(x, g):
    return x * lax.rsqrt(jnp.mean(x * x, axis=-1, keepdims=True) + EPS) * g


_INPROJ_TM = 512
_INPROJ_TN = 512
_VT_TILE = 256
_VT_AUG = MLA_V_DIM + 16
_LAT0 = GATE_W


def _inproj_kernel(x_ref, g_ref, wr_ref, wc_ref, gq_ref, gkv_ref, wqt_ref, wk_ref, wvt_ref,
                   cos_ref, sin_ref, cost_ref, sint_ref,
                   gate_ref, kk_ref, qat_ref, vat_ref, qt_ref, k_ref, vt_ref):
    tm = x_ref.shape[0]
    h = _rms(x_ref[...], g_ref[...]).astype(BF16)

    lat_a = jnp.dot(h, wr_ref[:, _LAT0:_LAT0 + 512], preferred_element_type=F32)
    lat_b = jnp.dot(h, wr_ref[:, _LAT0 + 512:ROW_W], preferred_element_type=F32)
    cq = lat_a[:, 0:Q_LORA_RANK]
    ckv = jnp.concatenate([lat_a[:, Q_LORA_RANK:512], lat_b[:, 0:128]], axis=1)
    kslab = lat_b[:, 128:256]
    ka = lat_b[:, 256:384]
    ka_sw = pltpu.roll(ka, SWA_HEAD_DIM, 1)
    low = lax.broadcasted_iota(jnp.int32, ka.shape, 1) < SWA_HEAD_DIM
    kk_ref[:, 0:KA_W] = jnp.where(low, ka, ka_sw).astype(BF16)
    kk_ref[:, KA_W:KK_W] = jnp.where(low, ka_sw, ka).astype(BF16)

    for c in range(GATE_W // _INPROJ_TN):
        cols = slice(c * _INPROJ_TN, (c + 1) * _INPROJ_TN)
        r = jnp.dot(h, wr_ref[:, cols], preferred_element_type=F32)
        gate_ref[:, cols] = (r * (1.0 / (1.0 + jnp.exp(-r)))).astype(BF16)
    for c in range(A_Q // _INPROJ_TN):
        rows = slice(c * _INPROJ_TN, (c + 1) * _INPROJ_TN)
        r = lax.dot_general(wc_ref[rows, :], h, NT_DIMS, preferred_element_type=F32)
        qat_ref[rows, :] = (r * SWA_QSCALE).astype(BF16)
    vat_ref[...] = lax.dot_general(wc_ref[A_Q:COL_W, :], h, NT_DIMS,
                                   preferred_element_type=F32).astype(BF16)

    cqn = _rms(cq, gq_ref[...]).astype(BF16)
    ckvn = _rms(ckv, gkv_ref[...]).astype(BF16)
    krot = kslab * cos_ref[...] + pltpu.roll(kslab, MLA_ROPE_DIM, 1) * sin_ref[...]
    krot = jnp.where(low, krot, 0.0).astype(BF16)
    knope = jnp.dot(ckvn, wk_ref[...], preferred_element_type=F32)
    for hd in range(MLA_HEADS):
        k_ref[0, hd, :, 0:128] = knope[:, hd * 128:(hd + 1) * 128].astype(BF16)
        k_ref[0, hd, :, 128:256] = krot

    vt = lax.dot_general(wvt_ref[...], ckvn, NT_DIMS, preferred_element_type=F32)
    ones = jnp.ones((_VT_AUG - MLA_V_DIM, _VT_TILE), BF16)
    for hd in range(MLA_HEADS):
        for c in range(tm // _VT_TILE):
            vt_ref[0, hd, c, 0:MLA_V_DIM] = vt[hd * 128:(hd + 1) * 128, c * _VT_TILE:(c + 1) * _VT_TILE].astype(BF16)
            vt_ref[0, hd, c, MLA_V_DIM:_VT_AUG] = ones

    qt = lax.dot_general(wqt_ref[...], cqn, NT_DIMS, preferred_element_type=F32)
    cost = cost_ref[...]
    sint = sint_ref[...]
    pad = jnp.zeros((MLA_QK_PAD - MLA_QK_DIM, tm), BF16)
    hr = MLA_ROPE_DIM // 2
    for hd in range(MLA_HEADS):
        r0 = 1024 + hd * MLA_ROPE_DIM
        rope = qt[r0:r0 + MLA_ROPE_DIM]
        rope_sw = jnp.concatenate([rope[hr:], rope[:hr]], axis=0)
        qr = (rope * cost + rope_sw * sint) * MLA_QSCALE
        qt_ref[0, hd, 0, 0:128, :] = (qt[hd * 128:(hd + 1) * 128] * MLA_QSCALE).astype(BF16)
        qt_ref[0, hd, 0, 128:192, :] = qr.astype(BF16)
        qt_ref[0, hd, 0, 192:256, :] = pad


def _inproj(x2d, g, w_row, w_col, gq, gkv, wqt, wk, wvt, cos2, sin2, cost, sint, l, b, s):
    t = x2d.shape[0]
    tm = _INPROJ_TM
    nt = s // tm
    nvt = tm // _VT_TILE
    return pl.pallas_call(
        _inproj_kernel,
        grid=(t // tm,),
        in_specs=[
            pl.BlockSpec((tm, D_MODEL), lambda i: (i, 0)),
            _layer(g, l),
            _layer(w_row, l),
            _layer(w_col, l),
            _layer(gq, l),
            _layer(gkv, l),
            _layer(wqt, l),
            _layer(wk, l),
            _layer(wvt, l),
            pl.BlockSpec((tm, 128), lambda i: (i % nt, 0)),
            pl.BlockSpec((tm, 128), lambda i: (i % nt, 0)),
            pl.BlockSpec((MLA_ROPE_DIM, tm), lambda i: (0, i % nt)),
            pl.BlockSpec((MLA_ROPE_DIM, tm), lambda i: (0, i % nt)),
        ],
        out_specs=[
            pl.BlockSpec((tm, GATE_W), lambda i: (i, 0)),
            pl.BlockSpec((tm, KK_W), lambda i: (i, 0)),
            pl.BlockSpec((A_Q, tm), lambda i: (0, i)),
            pl.BlockSpec((VT_ROWS, tm), lambda i: (0, i)),
            pl.BlockSpec((1, MLA_HEADS, 1, MLA_QK_PAD, tm), lambda i: (i // nt, 0, i % nt, 0, 0)),
            pl.BlockSpec((1, MLA_HEADS, tm, MLA_QK_PAD), lambda i: (i // nt, 0, i % nt, 0)),
            pl.BlockSpec((1, MLA_HEADS, nvt, _VT_AUG, _VT_TILE), lambda i: (i // nt, 0, i % nt, 0, 0)),
        ],
        out_shape=[
            jax.ShapeDtypeStruct((t, GATE_W), BF16),
            jax.ShapeDtypeStruct((t, KK_W), BF16),
            jax.ShapeDtypeStruct((A_Q, t), BF16),
            jax.ShapeDtypeStruct((VT_ROWS, t), BF16),
            jax.ShapeDtypeStruct((b, MLA_HEADS, nt, MLA_QK_PAD, tm), BF16),
            jax.ShapeDtypeStruct((b, MLA_HEADS, s, MLA_QK_PAD), BF16),
            jax.ShapeDtypeStruct((b, MLA_HEADS, s // _VT_TILE, _VT_AUG, _VT_TILE), BF16),
        ],
        compiler_params=_cparams("parallel"),
        name="inproj",
    )(x2d, g, w_row, w_col, gq, gkv, wqt, wk, wvt, cos2, sin2, cost, sint)


_MLA_TQ = 512
_MLA_HP = 2
_MLA_LOOP_STAGES = 8


def _mla_tile_order(nq):
    off_diag = [(qi, j) for qi in range(nq) for j in range(qi)]
    return off_diag, [(qi, qi) for qi in range(nq)]


def _mla_attn_kernel(order_ref, qt_ref, k_ref, vt_ref, gate_ref, o_ref, *scratch):
    m_refs = scratch[0:_MLA_HP]
    acc_refs = scratch[_MLA_HP:2 * _MLA_HP]
    st_ref, mx_ref = scratch[2 * _MLA_HP:]
    nq, tq = qt_ref.shape[2], qt_ref.shape[4]
    for hh in range(_MLA_HP):
        m_refs[hh][...] = jnp.full_like(m_refs[hh], NEG)
        acc_refs[hh][...] = jnp.zeros_like(acc_refs[hh])
    nsub = tq // _VT_TILE

    def scores(hh, qi, j, slot):
        kt = k_ref[0, hh, pl.ds(pl.multiple_of(j * tq, tq), tq), :]
        st = jnp.dot(kt, qt_ref[0, hh, qi], preferred_element_type=F32)
        st_ref[slot, hh] = st
        mx_ref[slot, hh] = jnp.max(st, axis=0, keepdims=True)

    def update(hh, qi, j, slot):
        m_old = m_refs[hh][qi]
        m_new = jnp.maximum(m_old, mx_ref[slot, hh])
        alpha = jnp.exp2(m_old - m_new)
        pb = jnp.exp2(st_ref[slot, hh] - m_new).astype(BF16)
        m_refs[hh][qi] = m_new
        pv = jnp.dot(vt_ref[0, hh, j * nsub], pb[0:_VT_TILE], preferred_element_type=F32)
        for c in range(1, nsub):
            pv += jnp.dot(vt_ref[0, hh, j * nsub + c], pb[c * _VT_TILE:(c + 1) * _VT_TILE],
                          preferred_element_type=F32)
        acc_refs[hh][qi] = alpha * acc_refs[hh][qi] + pv

    hq = _VT_TILE
    assert tq == 2 * hq

    def causal(block):
        kpos = lax.broadcasted_iota(jnp.int32, block.shape, 0)
        qpos = lax.broadcasted_iota(jnp.int32, block.shape, 1)
        return jnp.where(kpos <= qpos, block, NEG)

    def scores_diag(hh, qi, slot):
        base = qi * tq
        qt = qt_ref[0, hh, qi]
        top = jnp.dot(k_ref[0, hh, base:base + hq, :], qt, preferred_element_type=F32)
        top = jnp.concatenate([causal(top[:, 0:hq]), top[:, hq:tq]], axis=1)
        bot = causal(jnp.dot(k_ref[0, hh, base + hq:base + tq, :], qt[:, hq:tq],
                             preferred_element_type=F32))
        st_ref[slot, hh, 0:hq, :] = top
        st_ref[slot, hh, hq:tq, hq:tq] = bot
        mtop = jnp.max(top, axis=0, keepdims=True)
        mbot = jnp.max(bot, axis=0, keepdims=True)
        mx_ref[slot, hh] = jnp.concatenate([mtop[:, 0:hq], jnp.maximum(mtop[:, hq:tq], mbot)], axis=1)

    def update_diag(hh, qi, slot):
        m_old = m_refs[hh][qi]
        m_new = jnp.maximum(m_old, mx_ref[slot, hh])
        alpha = jnp.exp2(m_old - m_new)
        m_refs[hh][qi] = m_new
        p_top = jnp.exp2(st_ref[slot, hh, 0:hq, :] - m_new).astype(BF16)
        p_bot = jnp.exp2(st_ref[slot, hh, hq:tq, hq:tq] - m_new[:, hq:tq]).astype(BF16)
        pv = jnp.dot(vt_ref[0, hh, qi * nsub], p_top, preferred_element_type=F32)
        pv_r = jnp.dot(vt_ref[0, hh, qi * nsub + 1], p_bot, preferred_element_type=F32)
        acc = acc_refs[hh]
        acc[qi, :, 0:hq] = alpha[:, 0:hq] * acc[qi, :, 0:hq] + pv[:, 0:hq]
        acc[qi, :, hq:tq] = alpha[:, hq:tq] * acc[qi, :, hq:tq] + (pv[:, hq:tq] + pv_r)

    def stage(cur, nxt, slot, diag_cur, diag_next):
        for hh in range(_MLA_HP):
            if nxt is not None and diag_next:
                scores_diag(hh, nxt[0], 1 - slot)
            elif nxt is not None:
                scores(hh, nxt[0], nxt[1], 1 - slot)
            if diag_cur:
                update_diag(hh, cur[0], slot)
            else:
                update(hh, cur[0], cur[1], slot)

    off_diag, diag = _mla_tile_order(nq)
    n_loop = (len(off_diag) - 1) // _MLA_LOOP_STAGES * _MLA_LOOP_STAGES
    for hh in range(_MLA_HP):
        scores(hh, off_diag[0][0], off_diag[0][1], 0)

    def body(t, carry):
        for u in range(_MLA_LOOP_STAGES):
            g = t * _MLA_LOOP_STAGES + u
            cur = (order_ref[0, g], order_ref[1, g])
            nxt = (order_ref[0, g + 1], order_ref[1, g + 1])
            stage(cur, nxt, u % 2, False, False)
        return carry

    lax.fori_loop(0, n_loop // _MLA_LOOP_STAGES, body, 0)
    def finalize(qi):
        for hh in range(_MLA_HP):
            out = acc_refs[hh][qi, 0:MLA_V_DIM] / acc_refs[hh][qi, MLA_V_DIM:MLA_V_DIM + 1]
            rows, cols = slice(qi * tq, (qi + 1) * tq), slice(hh * MLA_V_DIM, (hh + 1) * MLA_V_DIM)
            o_ref[0, rows, cols] = out.T.astype(BF16) * gate_ref[0, rows, cols]

    tail = off_diag[n_loop:] + diag
    for i, cur in enumerate(tail):
        g = n_loop + i
        nxt = tail[i + 1] if i + 1 < len(tail) else None
        stage(cur, nxt, g % 2, cur[0] == cur[1], nxt is not None and nxt[0] == nxt[1])
        if cur[0] == cur[1]:
            finalize(cur[0])


def _mla_attn(qt, k, vt, gate):
    b, h, nq, _, tq = qt.shape
    s = nq * tq
    hp = _MLA_HP
    off_diag, _ = _mla_tile_order(nq)
    order = jnp.asarray(list(zip(*off_diag)), jnp.int32)
    return pl.pallas_call(
        _mla_attn_kernel,
        grid=(b, h // hp),
        in_specs=[
            pl.BlockSpec(memory_space=pltpu.SMEM),
            pl.BlockSpec((1, hp, nq, MLA_QK_PAD, tq), lambda bi, hi: (bi, hi, 0, 0, 0)),
            pl.BlockSpec((1, hp, s, MLA_QK_PAD), lambda bi, hi: (bi, hi, 0, 0)),
            pl.BlockSpec((1, hp, s // _VT_TILE, _VT_AUG, _VT_TILE), lambda bi, hi: (bi, hi, 0, 0, 0)),
            pl.BlockSpec((1, s, hp * MLA_V_DIM), lambda bi, hi: (bi, 0, SWA_WIDTH // (hp * MLA_V_DIM) + hi)),
        ],
        out_specs=pl.BlockSpec((1, s, hp * MLA_V_DIM), lambda bi, hi: (bi, 0, hi)),
        out_shape=jax.ShapeDtypeStruct((b, s, MLA_WIDTH), BF16),
        scratch_shapes=(
            [pltpu.VMEM((nq, 1, tq), F32) for _ in range(hp)]
            + [pltpu.VMEM((nq, _VT_AUG, tq), F32) for _ in range(hp)]
            + [pltpu.VMEM((2, hp, tq, tq), F32), pltpu.VMEM((2, hp, 1, tq), F32)]
        ),
        compiler_params=_cparams("parallel", "parallel"),
        name="mla_attn",
    )(order, qt, k, vt, gate)


_SWA_TQ = 1024
_SWA_NB = _SWA_TQ // BLOCK
_SWA_GW = SWA_GROUP * BLOCK


def _swa_kernel(sink_ref, bias_ref, pmask_ref, qt_ref, kc_ref, kp_ref, vc_ref, vp_ref, gate_ref,
                o_ref, kbuf_ref, vbuf_ref):
    tq = kc_ref.shape[1]
    kbuf_ref[0:BLOCK] = kp_ref[0]
    kbuf_ref[BLOCK:BLOCK + tq] = kc_ref[0]
    vbuf_ref[:, 0:BLOCK] = vp_ref[...]
    vbuf_ref[:, BLOCK:BLOCK + tq] = vc_ref[...]
    first_variant = jnp.where(pl.program_id(1) == 0, 1, 0)
    half = SWA_HEAD_DIM
    zeros = jnp.zeros((half, BLOCK), BF16)
    ones = jnp.ones((16, 2 * BLOCK), BF16)
    r_io =lax.broadcasted_iota(jnp.int32, (BLOCK, _SWA_GW), 0)
    t_io = lax.broadcasted_iota(jnp.int32, (BLOCK, _SWA_GW), 1) & (BLOCK - 1)
    prev_visible = t_io < r_io

    def scores(blk, hk):
        r0 = blk * BLOCK
        kwin = kbuf_ref[r0:r0 + 2 * BLOCK, hk * BLOCK:(hk + 1) * BLOCK]
        cols = []
        for p in range(SWA_GROUP // 2):
            row = hk * (SWA_GROUP * half) + p * BLOCK
            qt = qt_ref[row:row + BLOCK, r0:r0 + BLOCK]
            cols.append(jnp.concatenate([qt[0:half], zeros], axis=0))
            cols.append(jnp.concatenate([zeros, qt[half:BLOCK]], axis=0))
        qz = jnp.concatenate(cols, axis=1)
        st = jnp.dot(kwin, qz, preferred_element_type=F32)
        variant = first_variant if blk == 0 else 0
        return jnp.where(prev_visible, st[0:BLOCK], st[BLOCK:2 * BLOCK]) + bias_ref[variant, hk]

    def finish(blk, hk, s2):
        r0 = blk * BLOCK
        sink = sink_ref[hk]
        m = jnp.maximum(jnp.max(s2, axis=0, keepdims=True), sink)
        pb = jnp.exp2(s2 - m).astype(BF16)
        p_prev = pb * pmask_ref[...]
        pstack = jnp.concatenate([p_prev, pb - p_prev], axis=0)
        vwin = jnp.concatenate([vbuf_ref[hk * half:(hk + 1) * half, r0:r0 + 2 * BLOCK], ones], axis=0)
        ot = jnp.dot(vwin, pstack, preferred_element_type=F32)
        l = ot[half:half + 1] + jnp.exp2(sink - m)
        ot = ot[0:half] * (1.0 / l)
        for pr in range(SWA_GROUP // 2):
            c0 = 2 * pr * BLOCK
            pair_t = jnp.concatenate([ot[:, c0:c0 + BLOCK], ot[:, c0 + BLOCK:c0 + 2 * BLOCK]], axis=0)
            col = hk * (SWA_GROUP * half) + pr * BLOCK
            gated = pair_t.T.astype(BF16) * gate_ref[r0:r0 + BLOCK, col:col + BLOCK]
            o_ref[0, r0:r0 + BLOCK, col:col + BLOCK] = gated

    chains = [(blk, hk) for blk in range(tq // BLOCK) for hk in range(SWA_KV_HEADS)]
    s_next = scores(*chains[0])
    for c, chain in enumerate(chains):
        s_cur = s_next
        if c + 1 < len(chains):
            s_next = scores(*chains[c + 1])
        finish(*chain, s_cur)


def _swa_attn(sink_rows, bias, pmask, qat, kk, vt, gate, b, s):
    tq = _SWA_TQ
    nq = s // tq
    nb = _SWA_NB

    def prev_block(i):
        return jnp.maximum(i * nb - 1, 0)

    return pl.pallas_call(
        _swa_kernel,
        grid=(b, nq),
        in_specs=[
            _resident(sink_rows.shape),
            _resident(bias.shape),
            _resident(pmask.shape),
            pl.BlockSpec((A_Q, tq), lambda bi, i: (0, bi * nq + i)),
            pl.BlockSpec((1, tq, KK_W), lambda bi, i: (bi, i, 0)),
            pl.BlockSpec((1, BLOCK, KK_W), lambda bi, i: (bi, prev_block(i), 0)),
            pl.BlockSpec((VT_ROWS, tq), lambda bi, i: (0, bi * nq + i)),
            pl.BlockSpec((VT_ROWS, BLOCK), lambda bi, i: (0, bi * nq * nb + prev_block(i))),
            pl.BlockSpec((tq, SWA_WIDTH), lambda bi, i: (bi * nq + i, 0)),
        ],
        out_specs=pl.BlockSpec((1, tq, SWA_WIDTH), lambda bi, i: (bi, i, 0)),
        out_shape=jax.ShapeDtypeStruct((b, s, SWA_WIDTH), BF16),
        scratch_shapes=[
            pltpu.VMEM((tq + BLOCK, KK_W), BF16),
            pltpu.VMEM((VT_ROWS, tq + BLOCK), BF16),
        ],
        compiler_params=_cparams("parallel", "parallel"),
        name="swa_attn",
    )(sink_rows, bias, pmask, qat, kk.reshape(b, s, KK_W), kk.reshape(b, s, KK_W), vt, vt, gate)


def _swa_tables(swa_sinks):
    row = jnp.arange(BLOCK, dtype=jnp.int32)[:, None]
    tcol = jnp.arange(BLOCK, dtype=jnp.int32)[None, :]
    delta = jnp.where(tcol < row, BLOCK + tcol - row, tcol - row)
    slopes = jnp.exp2(-8.0 * jnp.arange(1, SWA_Q_HEADS + 1, dtype=F32) / SWA_Q_HEADS)
    slopes = slopes.reshape(SWA_KV_HEADS, SWA_GROUP)
    has_prev = -(slopes * LOG2E)[:, :, None, None] * delta.astype(F32)[None, None]
    no_prev = jnp.where((tcol < row)[None, None], NEG, has_prev)
    bias = jnp.stack([has_prev, no_prev])
    bias = bias.transpose(0, 1, 3, 2, 4).reshape(2, SWA_KV_HEADS, BLOCK, _SWA_GW)
    pmask = jnp.tile((tcol < row).astype(BF16), (1, SWA_GROUP))
    sinks2 = swa_sinks.astype(F32) * LOG2E
    sink_rows = jnp.repeat(sinks2.reshape(DEPTH, SWA_KV_HEADS, 1, SWA_GROUP), BLOCK, axis=-1)
    return bias, pmask, sink_rows


_OUTPROJ_TM = 512
_OUTPROJ_TN = 512


def _outproj_kernel(*refs, final):
    if final:
        ya_ref, yb_ref, x_ref, w_ref, gfin_ref, o_ref = refs
    else:
        ya_ref, yb_ref, x_ref, w_ref, o_ref = refs
    gated = jnp.concatenate([ya_ref[...], yb_ref[...]], axis=-1)
    for c in range(D_MODEL // _OUTPROJ_TN):
        sl = slice(c * _OUTPROJ_TN, (c + 1) * _OUTPROJ_TN)
        o_ref[:, sl] = x_ref[:, sl] + jnp.dot(gated, w_ref[:, sl], preferred_element_type=F32)
    if final:
        o_ref[...] = _rms(o_ref[...], gfin_ref[...])


def _outproj(ya, yb, x2d, w, l, final_g=None):
    t = x2d.shape[0]
    tm = _OUTPROJ_TM
    final = final_g is not None
    in_specs = [
        pl.BlockSpec((tm, SWA_WIDTH), lambda i: (i, 0)),
        pl.BlockSpec((tm, MLA_WIDTH), lambda i: (i, 0)),
        pl.BlockSpec((tm, D_MODEL), lambda i: (i, 0)),
        _layer(w, l),
    ]
    args = [ya, yb, x2d, w]
    if final:
        in_specs.append(_resident(final_g.shape))
        args.append(final_g)
    return pl.pallas_call(
        functools.partial(_outproj_kernel, final=final),
        grid=(t // tm,),
        in_specs=in_specs,
        out_specs=pl.BlockSpec((tm, D_MODEL), lambda i: (i, 0)),
        out_shape=jax.ShapeDtypeStruct((t, D_MODEL), F32),
        compiler_params=_cparams("parallel"),
        name="outproj_final" if final else "outproj",
    )(*args)


def _swap_halves(w):
    half = w.shape[-1] // 2
    return jnp.concatenate([w[..., half:], w[..., :half]], axis=-1)


_PREP_TR = 256


def _prep_w_in_kernel(w_ref, row_ref, col_ref):
    def rows(lo, n):
        return w_ref[0, lo:lo + n, :]

    o = 0
    qa = rows(o, A_Q); o += A_Q
    ka = rows(o, KA_W); o += KA_W
    va = rows(o, VT_ROWS); o += VT_ROWS
    ga = rows(o, 1024); o += 1024
    cq = rows(o, Q_LORA_RANK); o += Q_LORA_RANK
    ckv = rows(o, KV_LORA_RANK); o += KV_LORA_RANK
    half = MLA_ROPE_DIM // 2
    kr = rows(o, MLA_ROPE_DIM)
    kr_sw = jnp.concatenate([rows(o + half, half), rows(o, half)], axis=0)
    o += MLA_ROPE_DIM
    gb = rows(o, 1024)
    col_ref[0, 0:A_Q, :] = qa.astype(BF16)
    col_ref[0, A_Q:COL_W, :] = va.astype(BF16)
    off = 0
    for piece in (ga, gb, cq, ckv, jnp.concatenate([kr, kr_sw, ka], axis=0)):
        n = piece.shape[0]
        row_ref[0, :, off:off + n] = piece.T.astype(BF16)
        off += n


def _prep_w_in(w_in):
    tr = _PREP_TR
    w_t = jnp.swapaxes(w_in, 1, 2)
    return pl.pallas_call(
        _prep_w_in_kernel,
        grid=(DEPTH, D_MODEL // tr),
        in_specs=[pl.BlockSpec((1, IN_WIDTH, tr), lambda l, i: (l, 0, i))],
        out_specs=[
            pl.BlockSpec((1, tr, ROW_W), lambda l, i: (l, i, 0)),
            pl.BlockSpec((1, COL_W, tr), lambda l, i: (l, 0, i)),
        ],
        out_shape=[
            jax.ShapeDtypeStruct((DEPTH, D_MODEL, ROW_W), BF16),
            jax.ShapeDtypeStruct((DEPTH, COL_W, D_MODEL), BF16),
        ],
        compiler_params=_cparams("parallel", "parallel"),
        name="prep_w_in",
    )(w_t)


def _prep_w_q(w_q_b):
    w = w_q_b.reshape(DEPTH, Q_LORA_RANK, MLA_HEADS, MLA_QK_DIM)
    nope = w[..., :MLA_NOPE_DIM].reshape(DEPTH, Q_LORA_RANK, MLA_HEADS * MLA_NOPE_DIM)
    rope = w[..., MLA_NOPE_DIM:].reshape(DEPTH, Q_LORA_RANK, MLA_HEADS * MLA_ROPE_DIM)
    return jnp.swapaxes(jnp.concatenate([nope, rope], axis=-1), 1, 2).astype(BF16)


def _prep_w_kv(w_kv_b):
    w = w_kv_b.reshape(DEPTH, KV_LORA_RANK, MLA_HEADS, MLA_NOPE_DIM + MLA_V_DIM)
    wk = w[..., :MLA_NOPE_DIM].reshape(DEPTH, KV_LORA_RANK, MLA_HEADS * MLA_NOPE_DIM)
    wv = w[..., MLA_NOPE_DIM:].reshape(DEPTH, KV_LORA_RANK, MLA_HEADS * MLA_V_DIM)
    return wk.astype(BF16), jnp.swapaxes(wv, 1, 2).astype(BF16)


def _rope_tables(s):
    pos = jnp.arange(s, dtype=F32)
    inv_freq = ROPE_THETA ** (-jnp.arange(0, MLA_ROPE_DIM, 2, dtype=F32) / MLA_ROPE_DIM)
    ang = pos[:, None] * inv_freq[None, :]
    cos, sin = jnp.cos(ang), jnp.sin(ang)
    cc = jnp.concatenate([cos, cos], axis=-1)
    ss = jnp.concatenate([-sin, sin], axis=-1)
    return jnp.tile(cc, (1, 2)), jnp.tile(ss, (1, 2)), cc.T, ss.T


def kernel(x, attn_norm_g, w_in, swa_sinks, q_a_norm_g, kv_a_norm_g, w_q_b, w_kv_b, w_out, final_norm_g):
    b, s, d = x.shape
    t = b * s
    w_row, w_col = _prep_w_in(w_in)
    w_q_p = _prep_w_q(w_q_b)
    w_k_p, w_vt_p = _prep_w_kv(w_kv_b)
    w_out_p = w_out.astype(BF16)
    cos2, sin2, cost, sint = _rope_tables(s)
    swa_bias, swa_pmask, sink_rows = _swa_tables(swa_sinks)
    g_attn = attn_norm_g[:, None, :]
    g_q = q_a_norm_g[:, None, :]
    g_kv = kv_a_norm_g[:, None, :]

    x2d = x.reshape(t, d)
    for l in range(DEPTH):
        gate, kk, qat, vat, qt, k, vt = _inproj(x2d, g_attn, w_row, w_col, g_q, g_kv, w_q_p, w_k_p, w_vt_p,
                                                cos2, sin2, cost, sint, l, b, s)
        ya = _swa_attn(sink_rows[l], swa_bias, swa_pmask, qat, kk, vat, gate, b, s)
        yb = _mla_attn(qt, k, vt, gate.reshape(b, s, GATE_W))
        final_g = final_norm_g[None] if l == DEPTH - 1 else None
        x2d = _outproj(ya.reshape(t, SWA_WIDTH), yb.reshape(t, MLA_WIDTH), x2d, w_out_p, l, final_g)
    return x2d.reshape(b, s, d)
```

```python
import functools
import math

import jax
import jax.numpy as jnp
from jax import lax
from jax.experimental import pallas as pl
from jax.experimental.pallas import tpu as pltpu

F32 = jnp.float32
BF16 = jnp.bfloat16

D_MODEL = 2048
DEPTH = 4
EPS = 1e-6
BLOCK = 128
WINDOW = 128
NEG = -1e30
LOG2E = math.log2(math.e)

SWA_WIDTH = 1024
SWA_HEAD_DIM = 64
SWA_Q_HEADS = 16
SWA_KV_HEADS = 2
SWA_GROUP = 8

MLA_WIDTH = 1024
MLA_V_DIM = 128
MLA_HEADS = 8
MLA_NOPE_DIM = 128
MLA_ROPE_DIM = 64
MLA_QK_DIM = 192
MLA_QK_PAD = 256
Q_LORA_RANK = 384
KV_LORA_RANK = 256
ROPE_THETA = 10000.0

A_Q = 1024
GATE_W = 2048
IN_WIDTH = 4032
LAT_W = Q_LORA_RANK + KV_LORA_RANK + 2 * MLA_ROPE_DIM
KA_W = SWA_KV_HEADS * SWA_HEAD_DIM
KK_W = 2 * KA_W
VT_ROWS = SWA_KV_HEADS * SWA_HEAD_DIM
ROW_W = GATE_W + LAT_W + KA_W
COL_W = A_Q + VT_ROWS

SWA_QSCALE = SWA_HEAD_DIM ** -0.5 * LOG2E
MLA_QSCALE = MLA_QK_DIM ** -0.5 * LOG2E

VMEM_LIMIT = 56 * 1024 * 1024
NT_DIMS = (((1,), (1,)), ((), ()))


def _cparams(*sem):
    return pltpu.CompilerParams(dimension_semantics=sem, vmem_limit_bytes=VMEM_LIMIT)


def _resident(shape):
    nd = len(shape)
    return pl.BlockSpec(shape, lambda *_: (0,) * nd, pipeline_mode=pl.Buffered(1))


def _layer(stacked, l):
    tail = stacked.shape[1:]
    return pl.BlockSpec((None,) + tail, lambda *_: (l,) + (0,) * len(tail),
                        pipeline_mode=pl.Buffered(1))


def _rms(x, g):
    return x * lax.rsqrt(jnp.mean(x * x, axis=-1, keepdims=True) + EPS) * g


_INPROJ_TM = 512
_INPROJ_TN = 512
_VT_TILE = 256
_VT_AUG = MLA_V_DIM + 16
_LAT0 = GATE_W


def _inproj_kernel(x_ref, g_ref, wr_ref, wc_ref, gq_ref, gkv_ref, wqt_ref, wk_ref, wvt_ref,
                   cos_ref, sin_ref, cost_ref, sint_ref,
                   gate_ref, kk_ref, qat_ref, vat_ref, qt_ref, k_ref, vt_ref):
    tm = x_ref.shape[0]
    h = _rms(x_ref[...], g_ref[...]).astype(BF16)

    lat_a = jnp.dot(h, wr_ref[:, _LAT0:_LAT0 + 512], preferred_element_type=F32)
    lat_b = jnp.dot(h, wr_ref[:, _LAT0 + 512:ROW_W], preferred_element_type=F32)
    cq = lat_a[:, 0:Q_LORA_RANK]
    ckv = jnp.concatenate([lat_a[:, Q_LORA_RANK:512], lat_b[:, 0:128]], axis=1)
    kslab = lat_b[:, 128:256]
    ka = lat_b[:, 256:384]
    ka_sw = pltpu.roll(ka, SWA_HEAD_DIM, 1)
    low = lax.broadcasted_iota(jnp.int32, ka.shape, 1) < SWA_HEAD_DIM
    kk_ref[:, 0:KA_W] = jnp.where(low, ka, ka_sw).astype(BF16)
    kk_ref[:, KA_W:KK_W] = jnp.where(low, ka_sw, ka).astype(BF16)

    for c in range(GATE_W // _INPROJ_TN):
        cols = slice(c * _INPROJ_TN, (c + 1) * _INPROJ_TN)
        r = jnp.dot(h, wr_ref[:, cols], preferred_element_type=F32)
        gate_ref[:, cols] = (r * (1.0 / (1.0 + jnp.exp(-r)))).astype(BF16)
    for c in range(A_Q // _INPROJ_TN):
        rows = slice(c * _INPROJ_TN, (c + 1) * _INPROJ_TN)
        r = lax.dot_general(wc_ref[rows, :], h, NT_DIMS, preferred_element_type=F32)
        qat_ref[rows, :] = (r * SWA_QSCALE).astype(BF16)
    vat_ref[...] = lax.dot_general(wc_ref[A_Q:COL_W, :], h, NT_DIMS,
                                   preferred_element_type=F32).astype(BF16)

    cqn = _rms(cq, gq_ref[...]).astype(BF16)
    ckvn = _rms(ckv, gkv_ref[...]).astype(BF16)
    krot = kslab * cos_ref[...] + pltpu.roll(kslab, MLA_ROPE_DIM, 1) * sin_ref[...]
    krot = jnp.where(low, krot, 0.0).astype(BF16)
    knope = jnp.dot(ckvn, wk_ref[...], preferred_element_type=F32)
    for hd in range(MLA_HEADS):
        k_ref[0, hd, :, 0:128] = knope[:, hd * 128:(hd + 1) * 128].astype(BF16)
        k_ref[0, hd, :, 128:256] = krot

    vt = lax.dot_general(wvt_ref[...], ckvn, NT_DIMS, preferred_element_type=F32)
    ones = jnp.ones((_VT_AUG - MLA_V_DIM, _VT_TILE), BF16)
    for hd in range(MLA_HEADS):
        for c in range(tm // _VT_TILE):
            vt_ref[0, hd, c, 0:MLA_V_DIM] = vt[hd * 128:(hd + 1) * 128, c * _VT_TILE:(c + 1) * _VT_TILE].astype(BF16)
            vt_ref[0, hd, c, MLA_V_DIM:_VT_AUG] = ones

    qt = lax.dot_general(wqt_ref[...], cqn, NT_DIMS, preferred_element_type=F32)
    cost = cost_ref[...]
    sint = sint_ref[...]
    pad = jnp.zeros((MLA_QK_PAD - MLA_QK_DIM, tm), BF16)
    hr = MLA_ROPE_DIM // 2
    for hd in range(MLA_HEADS):
        r0 = 1024 + hd * MLA_ROPE_DIM
        rope = qt[r0:r0 + MLA_ROPE_DIM]
        rope_sw = jnp.concatenate([rope[hr:], rope[:hr]], axis=0)
        qr = (rope * cost + rope_sw * sint) * MLA_QSCALE
        qt_ref[0, hd, 0, 0:128, :] = (qt[hd * 128:(hd + 1) * 128] * MLA_QSCALE).astype(BF16)
        qt_ref[0, hd, 0, 128:192, :] = qr.astype(BF16)
        qt_ref[0, hd, 0, 192:256, :] = pad


def _inproj(x2d, g, w_row, w_col, gq, gkv, wqt, wk, wvt, cos2, sin2, cost, sint, l, b, s):
    t = x2d.shape[0]
    tm = _INPROJ_TM
    nt = s // tm
    nvt = tm // _VT_TILE
    return pl.pallas_call(
        _inproj_kernel,
        grid=(t // tm,),
        in_specs=[
            pl.BlockSpec((tm, D_MODEL), lambda i: (i, 0)),
            _layer(g, l),
            _layer(w_row, l),
            _layer(w_col, l),
            _layer(gq, l),
            _layer(gkv, l),
            _layer(wqt, l),
            _layer(wk, l),
            _layer(wvt, l),
            pl.BlockSpec((tm, 128), lambda i: (i % nt, 0)),
            pl.BlockSpec((tm, 128), lambda i: (i % nt, 0)),
            pl.BlockSpec((MLA_ROPE_DIM, tm), lambda i: (0, i % nt)),
            pl.BlockSpec((MLA_ROPE_DIM, tm), lambda i: (0, i % nt)),
        ],
        out_specs=[
            pl.BlockSpec((tm, GATE_W), lambda i: (i, 0)),
            pl.BlockSpec((tm, KK_W), lambda i: (i, 0)),
            pl.BlockSpec((A_Q, tm), lambda i: (0, i)),
            pl.BlockSpec((VT_ROWS, tm), lambda i: (0, i)),
            pl.BlockSpec((1, MLA_HEADS, 1, MLA_QK_PAD, tm), lambda i: (i // nt, 0, i % nt, 0, 0)),
            pl.BlockSpec((1, MLA_HEADS, tm, MLA_QK_PAD), lambda i: (i // nt, 0, i % nt, 0)),
            pl.BlockSpec((1, MLA_HEADS, nvt, _VT_AUG, _VT_TILE), lambda i: (i // nt, 0, i % nt, 0, 0)),
        ],
        out_shape=[
            jax.ShapeDtypeStruct((t, GATE_W), BF16),
            jax.ShapeDtypeStruct((t, KK_W), BF16),
            jax.ShapeDtypeStruct((A_Q, t), BF16),
            jax.ShapeDtypeStruct((VT_ROWS, t), BF16),
            jax.ShapeDtypeStruct((b, MLA_HEADS, nt, MLA_QK_PAD, tm), BF16),
            jax.ShapeDtypeStruct((b, MLA_HEADS, s, MLA_QK_PAD), BF16),
            jax.ShapeDtypeStruct((b, MLA_HEADS, s // _VT_TILE, _VT_AUG, _VT_TILE), BF16),
        ],
        compiler_params=_cparams("parallel"),
        name="inproj",
    )(x2d, g, w_row, w_col, gq, gkv, wqt, wk, wvt, cos2, sin2, cost, sint)


_MLA_TQ = 512
_MLA_HP = 2
_MLA_LOOP_STAGES = 12


def _mla_tile_order(nq):
    off_diag = [(qi, j) for qi in range(nq) for j in range(qi)]
    return off_diag, [(qi, qi) for qi in range(nq)]


def _mla_attn_kernel(order_ref, qt_ref, k_ref, vt_ref, gate_ref, o_ref, *scratch):
    m_refs = scratch[0:_MLA_HP]
    acc_refs = scratch[_MLA_HP:2 * _MLA_HP]
    st_ref, mx_ref = scratch[2 * _MLA_HP:]
    nq, tq = qt_ref.shape[2], qt_ref.shape[4]
    for hh in range(_MLA_HP):
        m_refs[hh][...] = jnp.full_like(m_refs[hh], NEG)
        acc_refs[hh][...] = jnp.zeros_like(acc_refs[hh])
    nsub = tq // _VT_TILE

    def scores(hh, qi, j, slot):
        kt = k_ref[0, hh, pl.ds(pl.multiple_of(j * tq, tq), tq), :]
        st = jnp.dot(kt, qt_ref[0, hh, qi], preferred_element_type=F32)
        st_ref[slot, hh] = st
        mx_ref[slot, hh] = jnp.max(st, axis=0, keepdims=True)

    def update(hh, qi, j, slot):
        m_old = m_refs[hh][qi]
        m_new = jnp.maximum(m_old, mx_ref[slot, hh])
        alpha = jnp.exp2(m_old - m_new)
        pb = jnp.exp2(st_ref[slot, hh] - m_new).astype(BF16)
        m_refs[hh][qi] = m_new
        pv = jnp.dot(vt_ref[0, hh, j * nsub], pb[0:_VT_TILE], preferred_element_type=F32)
        for c in range(1, nsub):
            pv += jnp.dot(vt_ref[0, hh, j * nsub + c], pb[c * _VT_TILE:(c + 1) * _VT_TILE],
                          preferred_element_type=F32)
        acc_refs[hh][qi] = alpha * acc_refs[hh][qi] + pv

    hq = _VT_TILE
    assert tq == 2 * hq

    def causal(block):
        kpos = lax.broadcasted_iota(jnp.int32, block.shape, 0)
        qpos = lax.broadcasted_iota(jnp.int32, block.shape, 1)
        return jnp.where(kpos <= qpos, block, NEG)

    def scores_diag(hh, qi, slot):
        base = qi * tq
        qt = qt_ref[0, hh, qi]
        top = jnp.dot(k_ref[0, hh, base:base + hq, :], qt, preferred_element_type=F32)
        top = jnp.concatenate([causal(top[:, 0:hq]), top[:, hq:tq]], axis=1)
        bot = causal(jnp.dot(k_ref[0, hh, base + hq:base + tq, :], qt[:, hq:tq],
                             preferred_element_type=F32))
        st_ref[slot, hh, 0:hq, :] = top
        st_ref[slot, hh, hq:tq, hq:tq] = bot
        mtop = jnp.max(top, axis=0, keepdims=True)
        mbot = jnp.max(bot, axis=0, keepdims=True)
        mx_ref[slot, hh] = jnp.concatenate([mtop[:, 0:hq], jnp.maximum(mtop[:, hq:tq], mbot)], axis=1)

    def update_diag(hh, qi, slot):
        m_old = m_refs[hh][qi]
        m_new = jnp.maximum(m_old, mx_ref[slot, hh])
        alpha = jnp.exp2(m_old - m_new)
        m_refs[hh][qi] = m_new
        p_top = jnp.exp2(st_ref[slot, hh, 0:hq, :] - m_new).astype(BF16)
        p_bot = jnp.exp2(st_ref[slot, hh, hq:tq, hq:tq] - m_new[:, hq:tq]).astype(BF16)
        pv = jnp.dot(vt_ref[0, hh, qi * nsub], p_top, preferred_element_type=F32)
        pv_r = jnp.dot(vt_ref[0, hh, qi * nsub + 1], p_bot, preferred_element_type=F32)
        acc = acc_refs[hh]
        acc[qi, :, 0:hq] = alpha[:, 0:hq] * acc[qi, :, 0:hq] + pv[:, 0:hq]
        acc[qi, :, hq:tq] = alpha[:, hq:tq] * acc[qi, :, hq:tq] + (pv[:, hq:tq] + pv_r)

    def stage(cur, nxt, slot, diag_cur, diag_next):
        for hh in range(_MLA_HP):
            if nxt is not None and diag_next:
                scores_diag(hh, nxt[0], 1 - slot)
            elif nxt is not None:
                scores(hh, nxt[0], nxt[1], 1 - slot)
            if diag_cur:
                update_diag(hh, cur[0], slot)
            else:
                update(hh, cur[0], cur[1], slot)

    off_diag, diag = _mla_tile_order(nq)
    n_loop = (len(off_diag) - 1) // _MLA_LOOP_STAGES * _MLA_LOOP_STAGES
    for hh in range(_MLA_HP):
        scores(hh, off_diag[0][0], off_diag[0][1], 0)

    def body(t, carry):
        for u in range(_MLA_LOOP_STAGES):
            g = t * _MLA_LOOP_STAGES + u
            cur = (order_ref[0, g], order_ref[1, g])
            nxt = (order_ref[0, g + 1], order_ref[1, g + 1])
            stage(cur, nxt, u % 2, False, False)
        return carry

    lax.fori_loop(0, n_loop // _MLA_LOOP_STAGES, body, 0)
    def finalize(qi):
        for hh in range(_MLA_HP):
            out = acc_refs[hh][qi, 0:MLA_V_DIM] / acc_refs[hh][qi, MLA_V_DIM:MLA_V_DIM + 1]
            rows, cols = slice(qi * tq, (qi + 1) * tq), slice(hh * MLA_V_DIM, (hh + 1) * MLA_V_DIM)
            o_ref[0, rows, cols] = out.T.astype(BF16) * gate_ref[0, rows, cols]

    tail = off_diag[n_loop:] + diag
    for i, cur in enumerate(tail):
        g = n_loop + i
        nxt = tail[i + 1] if i + 1 < len(tail) else None
        stage(cur, nxt, g % 2, cur[0] == cur[1], nxt is not None and nxt[0] == nxt[1])
        if cur[0] == cur[1]:
            finalize(cur[0])


def _mla_attn(qt, k, vt, gate):
    b, h, nq, _, tq = qt.shape
    s = nq * tq
    hp = _MLA_HP
    off_diag, _ = _mla_tile_order(nq)
    order = jnp.asarray(list(zip(*off_diag)), jnp.int32)
    return pl.pallas_call(
        _mla_attn_kernel,
        grid=(b, h // hp),
        in_specs=[
            pl.BlockSpec(memory_space=pltpu.SMEM),
            pl.BlockSpec((1, hp, nq, MLA_QK_PAD, tq), lambda bi, hi: (bi, hi, 0, 0, 0)),
            pl.BlockSpec((1, hp, s, MLA_QK_PAD), lambda bi, hi: (bi, hi, 0, 0)),
            pl.BlockSpec((1, hp, s // _VT_TILE, _VT_AUG, _VT_TILE), lambda bi, hi: (bi, hi, 0, 0, 0)),
            pl.BlockSpec((1, s, hp * MLA_V_DIM), lambda bi, hi: (bi, 0, SWA_WIDTH // (hp * MLA_V_DIM) + hi)),
        ],
        out_specs=pl.BlockSpec((1, s, hp * MLA_V_DIM), lambda bi, hi: (bi, 0, hi)),
        out_shape=jax.ShapeDtypeStruct((b, s, MLA_WIDTH), BF16),
        scratch_shapes=(
            [pltpu.VMEM((nq, 1, tq), F32) for _ in range(hp)]
            + [pltpu.VMEM((nq, _VT_AUG, tq), F32) for _ in range(hp)]
            + [pltpu.VMEM((2, hp, tq, tq), F32), pltpu.VMEM((2, hp, 1, tq), F32)]
        ),
        compiler_params=_cparams("parallel", "parallel"),
        name="mla_attn",
    )(order, qt, k, vt, gate)


_SWA_TQ = 2048
_SWA_NB = _SWA_TQ // BLOCK
_SWA_GW = SWA_GROUP * BLOCK


def _swa_kernel(sink_ref, bias_ref, pmask_ref, qt_ref, kc_ref, kp_ref, vc_ref, vp_ref, gate_ref,
                o_ref, kbuf_ref, vbuf_ref):
    tq = kc_ref.shape[1]
    kbuf_ref[0:BLOCK] = kp_ref[0]
    kbuf_ref[BLOCK:BLOCK + tq] = kc_ref[0]
    vbuf_ref[:, 0:BLOCK] = vp_ref[...]
    vbuf_ref[:, BLOCK:BLOCK + tq] = vc_ref[...]
    first_variant = jnp.where(pl.program_id(1) == 0, 1, 0)
    half = SWA_HEAD_DIM
    zeros = jnp.zeros((half, BLOCK), BF16)
    ones = jnp.ones((16, 2 * BLOCK), BF16)
    r_io =lax.broadcasted_iota(jnp.int32, (BLOCK, _SWA_GW), 0)
    t_io = lax.broadcasted_iota(jnp.int32, (BLOCK, _SWA_GW), 1) & (BLOCK - 1)
    prev_visible = t_io < r_io

    def scores(blk, hk):
        r0 = blk * BLOCK
        kwin = kbuf_ref[r0:r0 + 2 * BLOCK, hk * BLOCK:(hk + 1) * BLOCK]
        cols = []
        for p in range(SWA_GROUP // 2):
            row = hk * (SWA_GROUP * half) + p * BLOCK
            qt = qt_ref[row:row + BLOCK, r0:r0 + BLOCK]
            cols.append(jnp.concatenate([qt[0:half], zeros], axis=0))
            cols.append(jnp.concatenate([zeros, qt[half:BLOCK]], axis=0))
        qz = jnp.concatenate(cols, axis=1)
        st = jnp.dot(kwin, qz, preferred_element_type=F32)
        variant = first_variant if blk == 0 else 0
        return jnp.where(prev_visible, st[0:BLOCK], st[BLOCK:2 * BLOCK]) + bias_ref[variant, hk]

    def finish(blk, hk, s2):
        r0 = blk * BLOCK
        sink = sink_ref[hk]
        m = jnp.maximum(jnp.max(s2, axis=0, keepdims=True), sink)
        pb = jnp.exp2(s2 - m).astype(BF16)
        p_prev = pb * pmask_ref[...]
        pstack = jnp.concatenate([p_prev, pb - p_prev], axis=0)
        vwin = jnp.concatenate([vbuf_ref[hk * half:(hk + 1) * half, r0:r0 + 2 * BLOCK], ones], axis=0)
        ot = jnp.dot(vwin, pstack, preferred_element_type=F32)
        l = ot[half:half + 1] + jnp.exp2(sink - m)
        ot = ot[0:half] * (1.0 / l)
        for pr in range(SWA_GROUP // 2):
            c0 = 2 * pr * BLOCK
            pair_t = jnp.concatenate([ot[:, c0:c0 + BLOCK], ot[:, c0 + BLOCK:c0 + 2 * BLOCK]], axis=0)
            col = hk * (SWA_GROUP * half) + pr * BLOCK
            gated = pair_t.T.astype(BF16) * gate_ref[r0:r0 + BLOCK, col:col + BLOCK]
            o_ref[0, r0:r0 + BLOCK, col:col + BLOCK] = gated

    chains = [(blk, hk) for blk in range(tq // BLOCK) for hk in range(SWA_KV_HEADS)]
    s_next = scores(*chains[0])
    for c, chain in enumerate(chains):
        s_cur = s_next
        if c + 1 < len(chains):
            s_next = scores(*chains[c + 1])
        finish(*chain, s_cur)


def _swa_attn(sink_rows, bias, pmask, qat, kk, vt, gate, b, s):
    tq = _SWA_TQ
    nq = s // tq
    nb = _SWA_NB

    def prev_block(i):
        return jnp.maximum(i * nb - 1, 0)

    return pl.pallas_call(
        _swa_kernel,
        grid=(b, nq),
        in_specs=[
            _resident(sink_rows.shape),
            _resident(bias.shape),
            _resident(pmask.shape),
            pl.BlockSpec((A_Q, tq), lambda bi, i: (0, bi * nq + i)),
            pl.BlockSpec((1, tq, KK_W), lambda bi, i: (bi, i, 0)),
            pl.BlockSpec((1, BLOCK, KK_W), lambda bi, i: (bi, prev_block(i), 0)),
            pl.BlockSpec((VT_ROWS, tq), lambda bi, i: (0, bi * nq + i)),
            pl.BlockSpec((VT_ROWS, BLOCK), lambda bi, i: (0, bi * nq * nb + prev_block(i))),
            pl.BlockSpec((tq, SWA_WIDTH), lambda bi, i: (bi * nq + i, 0)),
        ],
        out_specs=pl.BlockSpec((1, tq, SWA_WIDTH), lambda bi, i: (bi, i, 0)),
        out_shape=jax.ShapeDtypeStruct((b, s, SWA_WIDTH), BF16),
        scratch_shapes=[
            pltpu.VMEM((tq + BLOCK, KK_W), BF16),
            pltpu.VMEM((VT_ROWS, tq + BLOCK), BF16),
        ],
        compiler_params=_cparams("parallel", "parallel"),
        name="swa_attn",
    )(sink_rows, bias, pmask, qat, kk.reshape(b, s, KK_W), kk.reshape(b, s, KK_W), vt, vt, gate)


def _swa_tables(swa_sinks):
    row = jnp.arange(BLOCK, dtype=jnp.int32)[:, None]
    tcol = jnp.arange(BLOCK, dtype=jnp.int32)[None, :]
    delta = jnp.where(tcol < row, BLOCK + tcol - row, tcol - row)
    slopes = jnp.exp2(-8.0 * jnp.arange(1, SWA_Q_HEADS + 1, dtype=F32) / SWA_Q_HEADS)
    slopes = slopes.reshape(SWA_KV_HEADS, SWA_GROUP)
    has_prev = -(slopes * LOG2E)[:, :, None, None] * delta.astype(F32)[None, None]
    no_prev = jnp.where((tcol < row)[None, None], NEG, has_prev)
    bias = jnp.stack([has_prev, no_prev])
    bias = bias.transpose(0, 1, 3, 2, 4).reshape(2, SWA_KV_HEADS, BLOCK, _SWA_GW)
    pmask = jnp.tile((tcol < row).astype(BF16), (1, SWA_GROUP))
    sinks2 = swa_sinks.astype(F32) * LOG2E
    sink_rows = jnp.repeat(sinks2.reshape(DEPTH, SWA_KV_HEADS, 1, SWA_GROUP), BLOCK, axis=-1)
    return bias, pmask, sink_rows


_OUTPROJ_TM = 1024
_OUTPROJ_TN = 512


def _outproj_kernel(*refs, final):
    if final:
        ya_ref, yb_ref, x_ref, w_ref, gfin_ref, o_ref = refs
    else:
        ya_ref, yb_ref, x_ref, w_ref, o_ref = refs
    gated = jnp.concatenate([ya_ref[...], yb_ref[...]], axis=-1)
    for c in range(D_MODEL // _OUTPROJ_TN):
        sl = slice(c * _OUTPROJ_TN, (c + 1) * _OUTPROJ_TN)
        o_ref[:, sl] = x_ref[:, sl] + jnp.dot(gated, w_ref[:, sl], preferred_element_type=F32)
    if final:
        o_ref[...] = _rms(o_ref[...], gfin_ref[...])


def _outproj(ya, yb, x2d, w, l, final_g=None):
    t = x2d.shape[0]
    tm = _OUTPROJ_TM
    final = final_g is not None
    in_specs = [
        pl.BlockSpec((tm, SWA_WIDTH), lambda i: (i, 0)),
        pl.BlockSpec((tm, MLA_WIDTH), lambda i: (i, 0)),
        pl.BlockSpec((tm, D_MODEL), lambda i: (i, 0)),
        _layer(w, l),
    ]
    args = [ya, yb, x2d, w]
    if final:
        in_specs.append(_resident(final_g.shape))
        args.append(final_g)
    return pl.pallas_call(
        functools.partial(_outproj_kernel, final=final),
        grid=(t // tm,),
        in_specs=in_specs,
        out_specs=pl.BlockSpec((tm, D_MODEL), lambda i: (i, 0)),
        out_shape=jax.ShapeDtypeStruct((t, D_MODEL), F32),
        compiler_params=_cparams("parallel"),
        name="outproj_final" if final else "outproj",
    )(*args)


def _swap_halves(w):
    half = w.shape[-1] // 2
    return jnp.concatenate([w[..., half:], w[..., :half]], axis=-1)


_PREP_TR = 256


def _prep_w_in_kernel(w_ref, row_ref, col_ref):
    def rows(lo, n):
        return w_ref[0, lo:lo + n, :]

    o = 0
    qa = rows(o, A_Q); o += A_Q
    ka = rows(o, KA_W); o += KA_W
    va = rows(o, VT_ROWS); o += VT_ROWS
    ga = rows(o, 1024); o += 1024
    cq = rows(o, Q_LORA_RANK); o += Q_LORA_RANK
    ckv = rows(o, KV_LORA_RANK); o += KV_LORA_RANK
    half = MLA_ROPE_DIM // 2
    kr = rows(o, MLA_ROPE_DIM)
    kr_sw = jnp.concatenate([rows(o + half, half), rows(o, half)], axis=0)
    o += MLA_ROPE_DIM
    gb = rows(o, 1024)
    col_ref[0, 0:A_Q, :] = qa.astype(BF16)
    col_ref[0, A_Q:COL_W, :] = va.astype(BF16)
    off = 0
    for piece in (ga, gb, cq, ckv, jnp.concatenate([kr, kr_sw, ka], axis=0)):
        n = piece.shape[0]
        row_ref[0, :, off:off + n] = piece.T.astype(BF16)
        off += n


def _prep_w_in(w_in):
    tr = _PREP_TR
    w_t = jnp.swapaxes(w_in, 1, 2)
    return pl.pallas_call(
        _prep_w_in_kernel,
        grid=(DEPTH, D_MODEL // tr),
        in_specs=[pl.BlockSpec((1, IN_WIDTH, tr), lambda l, i: (l, 0, i))],
        out_specs=[
            pl.BlockSpec((1, tr, ROW_W), lambda l, i: (l, i, 0)),
            pl.BlockSpec((1, COL_W, tr), lambda l, i: (l, 0, i)),
        ],
        out_shape=[
            jax.ShapeDtypeStruct((DEPTH, D_MODEL, ROW_W), BF16),
            jax.ShapeDtypeStruct((DEPTH, COL_W, D_MODEL), BF16),
        ],
        compiler_params=_cparams("parallel", "parallel"),
        name="prep_w_in",
    )(w_t)


def _prep_w_q(w_q_b):
    w = w_q_b.reshape(DEPTH, Q_LORA_RANK, MLA_HEADS, MLA_QK_DIM)
    nope = w[..., :MLA_NOPE_DIM].reshape(DEPTH, Q_LORA_RANK, MLA_HEADS * MLA_NOPE_DIM)
    rope = w[..., MLA_NOPE_DIM:].reshape(DEPTH, Q_LORA_RANK, MLA_HEADS * MLA_ROPE_DIM)
    return jnp.swapaxes(jnp.concatenate([nope, rope], axis=-1), 1, 2).astype(BF16)


def _prep_w_kv(w_kv_b):
    w = w_kv_b.reshape(DEPTH, KV_LORA_RANK, MLA_HEADS, MLA_NOPE_DIM + MLA_V_DIM)
    wk = w[..., :MLA_NOPE_DIM].reshape(DEPTH, KV_LORA_RANK, MLA_HEADS * MLA_NOPE_DIM)
    wv = w[..., MLA_NOPE_DIM:].reshape(DEPTH, KV_LORA_RANK, MLA_HEADS * MLA_V_DIM)
    return wk.astype(BF16), jnp.swapaxes(wv, 1, 2).astype(BF16)


def _rope_tables(s):
    pos = jnp.arange(s, dtype=F32)
    inv_freq = ROPE_THETA ** (-jnp.arange(0, MLA_ROPE_DIM, 2, dtype=F32) / MLA_ROPE_DIM)
    ang = pos[:, None] * inv_freq[None, :]
    cos, sin = jnp.cos(ang), jnp.sin(ang)
    cc = jnp.concatenate([cos, cos], axis=-1)
    ss = jnp.concatenate([-sin, sin], axis=-1)
    return jnp.tile(cc, (1, 2)), jnp.tile(ss, (1, 2)), cc.T, ss.T


def kernel(x, attn_norm_g, w_in, swa_sinks, q_a_norm_g, kv_a_norm_g, w_q_b, w_kv_b, w_out, final_norm_g):
    b, s, d = x.shape
    t = b * s
    w_row, w_col = _prep_w_in(w_in)
    w_q_p = _prep_w_q(w_q_b)
    w_k_p, w_vt_p = _prep_w_kv(w_kv_b)
    w_out_p = w_out.astype(BF16)
    cos2, sin2, cost, sint = _rope_tables(s)
    swa_bias, swa_pmask, sink_rows = _swa_tables(swa_sinks)
    g_attn = attn_norm_g[:, None, :]
    g_q = q_a_norm_g[:, None, :]
    g_kv = kv_a_norm_g[:, None, :]

    x2d = x.reshape(t, d)
    for l in range(DEPTH):
        gate, kk, qat, vat, qt, k, vt = _inproj(x2d, g_attn, w_row, w_col, g_q, g_kv, w_q_p, w_k_p, w_vt_p,
                                                cos2, sin2, cost, sint, l, b, s)
        ya = _swa_attn(sink_rows[l], swa_bias, swa_pmask, qat, kk, vat, gate, b, s)
        yb = _mla_attn(qt, k, vt, gate.reshape(b, s, GATE_W))
        final_g = final_norm_g[None] if l == DEPTH - 1 else None
        x2d = _outproj(ya.reshape(t, SWA_WIDTH), yb.reshape(t, MLA_WIDTH), x2d, w_out_p, l, final_g)
    return x2d.reshape(b, s, d)
```

```python
import functools
import math

import jax
import jax.numpy as jnp
from jax import lax
from jax.experimental import pallas as pl
from jax.experimental.pallas import tpu as pltpu

F32 = jnp.float32
BF16 = jnp.bfloat16

D_MODEL = 2048
DEPTH = 4
EPS = 1e-6
BLOCK = 128
WINDOW = 128
NEG = -1e30
LOG2E = math.log2(math.e)

SWA_WIDTH = 1024
SWA_HEAD_DIM = 64
SWA_Q_HEADS = 16
SWA_KV_HEADS = 2
SWA_GROUP = 8

MLA_WIDTH = 1024
MLA_V_DIM = 128
MLA_HEADS = 8
MLA_NOPE_DIM = 128
MLA_ROPE_DIM = 64
MLA_QK_DIM = 192
MLA_QK_PAD = 256
Q_LORA_RANK = 384
KV_LORA_RANK = 256
ROPE_THETA = 10000.0

A_Q = 1024
GATE_W = 2048
IN_WIDTH = 4032
LAT_W = Q_LORA_RANK + KV_LORA_RANK + 2 * MLA_ROPE_DIM
KA_W = SWA_KV_HEADS * SWA_HEAD_DIM
KK_W = 2 * KA_W
VT_ROWS = SWA_KV_HEADS * SWA_HEAD_DIM
ROW_W = GATE_W + LAT_W + KA_W
COL_W = A_Q + VT_ROWS

SWA_QSCALE = SWA_HEAD_DIM ** -0.5 * LOG2E
MLA_QSCALE = MLA_QK_DIM ** -0.5 * LOG2E

VMEM_LIMIT = 56 * 1024 * 1024
NT_DIMS = (((1,), (1,)), ((), ()))


def _cparams(*sem):
    return pltpu.CompilerParams(dimension_semantics=sem, vmem_limit_bytes=VMEM_LIMIT)


def _resident(shape):
    nd = len(shape)
    return pl.BlockSpec(shape, lambda *_: (0,) * nd, pipeline_mode=pl.Buffered(1))


def _layer(stacked, l):
    tail = stacked.shape[1:]
    return pl.BlockSpec((None,) + tail, lambda *_: (l,) + (0,) * len(tail),
                        pipeline_mode=pl.Buffered(1))


def _rms(x, g):
    return x * lax.rsqrt(jnp.mean(x * x, axis=-1, keepdims=True) + EPS) * g


_INPROJ_TM = 512
_INPROJ_TN = 512
_VT_TILE = 256
_VT_AUG = MLA_V_DIM + 16
_LAT0 = GATE_W


def _inproj_kernel(x_ref, g_ref, wr_ref, wc_ref, gq_ref, gkv_ref, wqt_ref, wk_ref, wvt_ref,
                   cos_ref, sin_ref, cost_ref, sint_ref,
                   gate_ref, kk_ref, qat_ref, vat_ref, qt_ref, k_ref, vt_ref):
    tm = x_ref.shape[0]
    h = _rms(x_ref[...], g_ref[...]).astype(BF16)

    lat_a = jnp.dot(h, wr_ref[:, _LAT0:_LAT0 + 512], preferred_element_type=F32)
    lat_b = jnp.dot(h, wr_ref[:, _LAT0 + 512:ROW_W], preferred_element_type=F32)
    cq = lat_a[:, 0:Q_LORA_RANK]
    ckv = jnp.concatenate([lat_a[:, Q_LORA_RANK:512], lat_b[:, 0:128]], axis=1)
    kslab = lat_b[:, 128:256]
    ka = lat_b[:, 256:384]
    ka_sw = pltpu.roll(ka, SWA_HEAD_DIM, 1)
    low = lax.broadcasted_iota(jnp.int32, ka.shape, 1) < SWA_HEAD_DIM
    kk_ref[:, 0:KA_W] = jnp.where(low, ka, ka_sw).astype(BF16)
    kk_ref[:, KA_W:KK_W] = jnp.where(low, ka_sw, ka).astype(BF16)

    for c in range(GATE_W // _INPROJ_TN):
        cols = slice(c * _INPROJ_TN, (c + 1) * _INPROJ_TN)
        r = jnp.dot(h, wr_ref[:, cols], preferred_element_type=F32)
        gate_ref[:, cols] = (r * (1.0 / (1.0 + jnp.exp(-r)))).astype(BF16)
    for c in range(A_Q // _INPROJ_TN):
        rows = slice(c * _INPROJ_TN, (c + 1) * _INPROJ_TN)
        r = lax.dot_general(wc_ref[rows, :], h, NT_DIMS, preferred_element_type=F32)
        qat_ref[rows, :] = (r * SWA_QSCALE).astype(BF16)
    vat_ref[...] = lax.dot_general(wc_ref[A_Q:COL_W, :], h, NT_DIMS,
                                   preferred_element_type=F32).astype(BF16)

    cqn = _rms(cq, gq_ref[...]).astype(BF16)
    ckvn = _rms(ckv, gkv_ref[...]).astype(BF16)
    krot = kslab * cos_ref[...] + pltpu.roll(kslab, MLA_ROPE_DIM, 1) * sin_ref[...]
    krot = jnp.where(low, krot, 0.0).astype(BF16)
    knope = jnp.dot(ckvn, wk_ref[...], preferred_element_type=F32)
    for hd in range(MLA_HEADS):
        k_ref[0, hd, :, 0:128] = knope[:, hd * 128:(hd + 1) * 128].astype(BF16)
        k_ref[0, hd, :, 128:256] = krot

    vt = lax.dot_general(wvt_ref[...], ckvn, NT_DIMS, preferred_element_type=F32)
    ones = jnp.ones((_VT_AUG - MLA_V_DIM, _VT_TILE), BF16)
    for hd in range(MLA_HEADS):
        for c in range(tm // _VT_TILE):
            vt_ref[0, hd, c, 0:MLA_V_DIM] = vt[hd * 128:(hd + 1) * 128, c * _VT_TILE:(c + 1) * _VT_TILE].astype(BF16)
            vt_ref[0, hd, c, MLA_V_DIM:_VT_AUG] = ones

    qt = lax.dot_general(wqt_ref[...], cqn, NT_DIMS, preferred_element_type=F32)
    cost = cost_ref[...]
    sint = sint_ref[...]
    pad = jnp.zeros((MLA_QK_PAD - MLA_QK_DIM, tm), BF16)
    hr = MLA_ROPE_DIM // 2
    for hd in range(MLA_HEADS):
        r0 = 1024 + hd * MLA_ROPE_DIM
        rope = qt[r0:r0 + MLA_ROPE_DIM]
        rope_sw = jnp.concatenate([rope[hr:], rope[:hr]], axis=0)
        qr = (rope * cost + rope_sw * sint) * MLA_QSCALE
        qt_ref[0, hd, 0, 0:128, :] = (qt[hd * 128:(hd + 1) * 128] * MLA_QSCALE).astype(BF16)
        qt_ref[0, hd, 0, 128:192, :] = qr.astype(BF16)
        qt_ref[0, hd, 0, 192:256, :] = pad


def _inproj(x2d, g, w_row, w_col, gq, gkv, wqt, wk, wvt, cos2, sin2, cost, sint, l, b, s):
    t = x2d.shape[0]
    tm = _INPROJ_TM
    nt = s // tm
    nvt = tm // _VT_TILE
    return pl.pallas_call(
        _inproj_kernel,
        grid=(t // tm,),
        in_specs=[
            pl.BlockSpec((tm, D_MODEL), lambda i: (i, 0)),
            _layer(g, l),
            _layer(w_row, l),
            _layer(w_col, l),
            _layer(gq, l),
            _layer(gkv, l),
            _layer(wqt, l),
            _layer(wk, l),
            _layer(wvt, l),
            pl.BlockSpec((tm, 128), lambda i: (i % nt, 0)),
            pl.BlockSpec((tm, 128), lambda i: (i % nt, 0)),
            pl.BlockSpec((MLA_ROPE_DIM, tm), lambda i: (0, i % nt)),
            pl.BlockSpec((MLA_ROPE_DIM, tm), lambda i: (0, i % nt)),
        ],
        out_specs=[
            pl.BlockSpec((tm, GATE_W), lambda i: (i, 0)),
            pl.BlockSpec((tm, KK_W), lambda i: (i, 0)),
            pl.BlockSpec((A_Q, tm), lambda i: (0, i)),
            pl.BlockSpec((VT_ROWS, tm), lambda i: (0, i)),
            pl.BlockSpec((1, MLA_HEADS, 1, MLA_QK_PAD, tm), lambda i: (i // nt, 0, i % nt, 0, 0)),
            pl.BlockSpec((1, MLA_HEADS, tm, MLA_QK_PAD), lambda i: (i // nt, 0, i % nt, 0)),
            pl.BlockSpec((1, MLA_HEADS, nvt, _VT_AUG, _VT_TILE), lambda i: (i // nt, 0, i % nt, 0, 0)),
        ],
        out_shape=[
            jax.ShapeDtypeStruct((t, GATE_W), BF16),
            jax.ShapeDtypeStruct((t, KK_W), BF16),
            jax.ShapeDtypeStruct((A_Q, t), BF16),
            jax.ShapeDtypeStruct((VT_ROWS, t), BF16),
            jax.ShapeDtypeStruct((b, MLA_HEADS, nt, MLA_QK_PAD, tm), BF16),
            jax.ShapeDtypeStruct((b, MLA_HEADS, s, MLA_QK_PAD), BF16),
            jax.ShapeDtypeStruct((b, MLA_HEADS, s // _VT_TILE, _VT_AUG, _VT_TILE), BF16),
        ],
        compiler_params=_cparams("parallel"),
        name="inproj",
    )(x2d, g, w_row, w_col, gq, gkv, wqt, wk, wvt, cos2, sin2, cost, sint)


_MLA_TQ = 512
_MLA_HP = 2
_MLA_LOOP_STAGES = 12


def _mla_tile_order(nq):
    off_diag = [(qi, j) for qi in range(nq) for j in range(qi)]
    return off_diag, [(qi, qi) for qi in range(nq)]


def _mla_attn_kernel(order_ref, qt_ref, k_ref, vt_ref, gate_ref, o_ref, *scratch):
    m_refs = scratch[0:_MLA_HP]
    acc_refs = scratch[_MLA_HP:2 * _MLA_HP]
    st_ref, mx_ref = scratch[2 * _MLA_HP:]
    nq, tq = qt_ref.shape[2], qt_ref.shape[4]
    nsub = tq // _VT_TILE

    def scores(hh, qi, j, slot):
        kt = k_ref[0, hh, pl.ds(pl.multiple_of(j * tq, tq), tq), :]
        st = jnp.dot(kt, qt_ref[0, hh, qi], preferred_element_type=F32)
        st_ref[slot, hh] = st
        mx_ref[slot, hh] = jnp.max(st, axis=0, keepdims=True)

    def update(hh, qi, j, slot):
        m_old = m_refs[hh][qi]
        m_new = jnp.maximum(m_old, mx_ref[slot, hh])
        alpha = jnp.exp2(m_old - m_new)
        pb = jnp.exp2(st_ref[slot, hh] - m_new).astype(BF16)
        m_refs[hh][qi] = m_new
        pv = jnp.dot(vt_ref[0, hh, j * nsub], pb[0:_VT_TILE], preferred_element_type=F32)
        for c in range(1, nsub):
            pv += jnp.dot(vt_ref[0, hh, j * nsub + c], pb[c * _VT_TILE:(c + 1) * _VT_TILE],
                          preferred_element_type=F32)
        acc_refs[hh][qi] = alpha * acc_refs[hh][qi] + pv

    hq = _VT_TILE
    assert tq == 2 * hq

    def causal(block):
        kpos = lax.broadcasted_iota(jnp.int32, block.shape, 0)
        qpos = lax.broadcasted_iota(jnp.int32, block.shape, 1)
        return jnp.where(kpos <= qpos, block, NEG)

    def scores_diag(hh, qi, slot):
        base = qi * tq
        qt = qt_ref[0, hh, qi]
        top = jnp.dot(k_ref[0, hh, base:base + hq, :], qt, preferred_element_type=F32)
        top = jnp.concatenate([causal(top[:, 0:hq]), top[:, hq:tq]], axis=1)
        bot = causal(jnp.dot(k_ref[0, hh, base + hq:base + tq, :], qt[:, hq:tq],
                             preferred_element_type=F32))
        st_ref[slot, hh, 0:hq, :] = top
        st_ref[slot, hh, hq:tq, hq:tq] = bot
        mtop = jnp.max(top, axis=0, keepdims=True)
        mbot = jnp.max(bot, axis=0, keepdims=True)
        mx_ref[slot, hh] = jnp.concatenate([mtop[:, 0:hq], jnp.maximum(mtop[:, hq:tq], mbot)], axis=1)

    def update_diag(hh, qi, slot):
        m_new = mx_ref[slot, hh]
        m_refs[hh][qi] = m_new
        p_top = jnp.exp2(st_ref[slot, hh, 0:hq, :] - m_new).astype(BF16)
        p_bot = jnp.exp2(st_ref[slot, hh, hq:tq, hq:tq] - m_new[:, hq:tq]).astype(BF16)
        pv = jnp.dot(vt_ref[0, hh, qi * nsub], p_top, preferred_element_type=F32)
        pv_r = jnp.dot(vt_ref[0, hh, qi * nsub + 1], p_bot, preferred_element_type=F32)
        acc = acc_refs[hh]
        acc[qi, :, 0:hq] = pv[:, 0:hq]
        acc[qi, :, hq:tq] = pv[:, hq:tq] + pv_r

    def stage(cur, nxt, slot, diag_cur, diag_next):
        for hh in range(_MLA_HP):
            if nxt is not None and diag_next:
                scores_diag(hh, nxt[0], 1 - slot)
            elif nxt is not None:
                scores(hh, nxt[0], nxt[1], 1 - slot)
            if diag_cur:
                update_diag(hh, cur[0], slot)
            else:
                update(hh, cur[0], cur[1], slot)

    off_diag, diag = _mla_tile_order(nq)
    assert len(diag) % 2 == 0 and _MLA_LOOP_STAGES % 2 == 0
    n_loop = (len(off_diag) - 1) // _MLA_LOOP_STAGES * _MLA_LOOP_STAGES

    def static_stages(tiles, g0, then):
        for i, cur in enumerate(tiles):
            nxt = tiles[i + 1] if i + 1 < len(tiles) else then
            stage(cur, nxt, (g0 + i) % 2, cur[0] == cur[1], nxt is not None and nxt[0] == nxt[1])

    for hh in range(_MLA_HP):
        scores_diag(hh, diag[0][0], 0)
    static_stages(diag, 0, off_diag[0])

    def body(t, carry):
        for u in range(_MLA_LOOP_STAGES):
            n = t * _MLA_LOOP_STAGES + u
            cur = (order_ref[0, n], order_ref[1, n])
            nxt = (order_ref[0, n + 1], order_ref[1, n + 1])
            stage(cur, nxt, u % 2, False, False)
        return carry

    lax.fori_loop(0, n_loop // _MLA_LOOP_STAGES, body, 0)
    static_stages(off_diag[n_loop:], len(diag) + n_loop, None)

    for qi in range(nq):
        for hh in range(_MLA_HP):
            out = acc_refs[hh][qi, 0:MLA_V_DIM] / acc_refs[hh][qi, MLA_V_DIM:MLA_V_DIM + 1]
            rows, cols = slice(qi * tq, (qi + 1) * tq), slice(hh * MLA_V_DIM, (hh + 1) * MLA_V_DIM)
            o_ref[0, rows, cols] = out.T.astype(BF16) * gate_ref[0, rows, cols]


def _mla_attn(qt, k, vt, gate):
    b, h, nq, _, tq = qt.shape
    s = nq * tq
    hp = _MLA_HP
    off_diag, _ = _mla_tile_order(nq)
    order = jnp.asarray(list(zip(*off_diag)), jnp.int32)
    return pl.pallas_call(
        _mla_attn_kernel,
        grid=(b, h // hp),
        in_specs=[
            pl.BlockSpec(memory_space=pltpu.SMEM),
            pl.BlockSpec((1, hp, nq, MLA_QK_PAD, tq), lambda bi, hi: (bi, hi, 0, 0, 0)),
            pl.BlockSpec((1, hp, s, MLA_QK_PAD), lambda bi, hi: (bi, hi, 0, 0)),
            pl.BlockSpec((1, hp, s // _VT_TILE, _VT_AUG, _VT_TILE), lambda bi, hi: (bi, hi, 0, 0, 0)),
            pl.BlockSpec((1, s, hp * MLA_V_DIM), lambda bi, hi: (bi, 0, SWA_WIDTH // (hp * MLA_V_DIM) + hi)),
        ],
        out_specs=pl.BlockSpec((1, s, hp * MLA_V_DIM), lambda bi, hi: (bi, 0, hi)),
        out_shape=jax.ShapeDtypeStruct((b, s, MLA_WIDTH), BF16),
        scratch_shapes=(
            [pltpu.VMEM((nq, 1, tq), F32) for _ in range(hp)]
            + [pltpu.VMEM((nq, _VT_AUG, tq), F32) for _ in range(hp)]
            + [pltpu.VMEM((2, hp, tq, tq), F32), pltpu.VMEM((2, hp, 1, tq), F32)]
        ),
        compiler_params=_cparams("parallel", "parallel"),
        name="mla_attn",
    )(order, qt, k, vt, gate)


_SWA_TQ = 2048
_SWA_NB = _SWA_TQ // BLOCK
_SWA_GW = SWA_GROUP * BLOCK


def _swa_kernel(sink_ref, bias_ref, pmask_ref, qt_ref, kc_ref, kp_ref, vc_ref, vp_ref, gate_ref,
                o_ref, kbuf_ref, vbuf_ref):
    tq = kc_ref.shape[1]
    kbuf_ref[0:BLOCK] = kp_ref[0]
    kbuf_ref[BLOCK:BLOCK + tq] = kc_ref[0]
    vbuf_ref[:, 0:BLOCK] = vp_ref[...]
    vbuf_ref[:, BLOCK:BLOCK + tq] = vc_ref[...]
    first_variant = jnp.where(pl.program_id(1) == 0, 1, 0)
    half = SWA_HEAD_DIM
    zeros = jnp.zeros((half, BLOCK), BF16)
    ones = jnp.ones((16, 2 * BLOCK), BF16)
    r_io =lax.broadcasted_iota(jnp.int32, (BLOCK, _SWA_GW), 0)
    t_io = lax.broadcasted_iota(jnp.int32, (BLOCK, _SWA_GW), 1) & (BLOCK - 1)
    prev_visible = t_io < r_io

    def scores(blk, hk):
        r0 = blk * BLOCK
        kwin = kbuf_ref[r0:r0 + 2 * BLOCK, hk * BLOCK:(hk + 1) * BLOCK]
        cols = []
        for p in range(SWA_GROUP // 2):
            row = hk * (SWA_GROUP * half) + p * BLOCK
            qt = qt_ref[row:row + BLOCK, r0:r0 + BLOCK]
            cols.append(jnp.concatenate([qt[0:half], zeros], axis=0))
            cols.append(jnp.concatenate([zeros, qt[half:BLOCK]], axis=0))
        qz = jnp.concatenate(cols, axis=1)
        st = jnp.dot(kwin, qz, preferred_element_type=F32)
        variant = first_variant if blk == 0 else 0
        return jnp.where(prev_visible, st[0:BLOCK], st[BLOCK:2 * BLOCK]) + bias_ref[variant, hk]

    def finish(blk, hk, s2):
        r0 = blk * BLOCK
        sink = sink_ref[hk]
        m = jnp.maximum(jnp.max(s2, axis=0, keepdims=True), sink)
        pb = jnp.exp2(s2 - m).astype(BF16)
        p_prev = pb * pmask_ref[...]
        pstack = jnp.concatenate([p_prev, pb - p_prev], axis=0)
        vwin = jnp.concatenate([vbuf_ref[hk * half:(hk + 1) * half, r0:r0 + 2 * BLOCK], ones], axis=0)
        ot = jnp.dot(vwin, pstack, preferred_element_type=F32)
        l = ot[half:half + 1] + jnp.exp2(sink - m)
        ot = ot[0:half] * (1.0 / l)
        for pr in range(SWA_GROUP // 2):
            c0 = 2 * pr * BLOCK
            pair_t = jnp.concatenate([ot[:, c0:c0 + BLOCK], ot[:, c0 + BLOCK:c0 + 2 * BLOCK]], axis=0)
            col = hk * (SWA_GROUP * half) + pr * BLOCK
            gated = pair_t.T.astype(BF16) * gate_ref[r0:r0 + BLOCK, col:col + BLOCK]
            o_ref[0, r0:r0 + BLOCK, col:col + BLOCK] = gated

    chains = [(blk, hk) for blk in range(tq // BLOCK) for hk in range(SWA_KV_HEADS)]
    s_next = scores(*chains[0])
    for c, chain in enumerate(chains):
        s_cur = s_next
        if c + 1 < len(chains):
            s_next = scores(*chains[c + 1])
        finish(*chain, s_cur)


def _swa_attn(sink_rows, bias, pmask, qat, kk, vt, gate, b, s):
    tq = _SWA_TQ
    nq = s // tq
    nb = _SWA_NB

    def prev_block(i):
        return jnp.maximum(i * nb - 1, 0)

    return pl.pallas_call(
        _swa_kernel,
        grid=(b, nq),
        in_specs=[
            _resident(sink_rows.shape),
            _resident(bias.shape),
            _resident(pmask.shape),
            pl.BlockSpec((A_Q, tq), lambda bi, i: (0, bi * nq + i)),
            pl.BlockSpec((1, tq, KK_W), lambda bi, i: (bi, i, 0)),
            pl.BlockSpec((1, BLOCK, KK_W), lambda bi, i: (bi, prev_block(i), 0)),
            pl.BlockSpec((VT_ROWS, tq), lambda bi, i: (0, bi * nq + i)),
            pl.BlockSpec((VT_ROWS, BLOCK), lambda bi, i: (0, bi * nq * nb + prev_block(i))),
            pl.BlockSpec((tq, SWA_WIDTH), lambda bi, i: (bi * nq + i, 0)),
        ],
        out_specs=pl.BlockSpec((1, tq, SWA_WIDTH), lambda bi, i: (bi, i, 0)),
        out_shape=jax.ShapeDtypeStruct((b, s, SWA_WIDTH), BF16),
        scratch_shapes=[
            pltpu.VMEM((tq + BLOCK, KK_W), BF16),
            pltpu.VMEM((VT_ROWS, tq + BLOCK), BF16),
        ],
        compiler_params=_cparams("parallel", "parallel"),
        name="swa_attn",
    )(sink_rows, bias, pmask, qat, kk.reshape(b, s, KK_W), kk.reshape(b, s, KK_W), vt, vt, gate)


def _swa_tables(swa_sinks):
    row = jnp.arange(BLOCK, dtype=jnp.int32)[:, None]
    tcol = jnp.arange(BLOCK, dtype=jnp.int32)[None, :]
    delta = jnp.where(tcol < row, BLOCK + tcol - row, tcol - row)
    slopes = jnp.exp2(-8.0 * jnp.arange(1, SWA_Q_HEADS + 1, dtype=F32) / SWA_Q_HEADS)
    slopes = slopes.reshape(SWA_KV_HEADS, SWA_GROUP)
    has_prev = -(slopes * LOG2E)[:, :, None, None] * delta.astype(F32)[None, None]
    no_prev = jnp.where((tcol < row)[None, None], NEG, has_prev)
    bias = jnp.stack([has_prev, no_prev])
    bias = bias.transpose(0, 1, 3, 2, 4).reshape(2, SWA_KV_HEADS, BLOCK, _SWA_GW)
    pmask = jnp.tile((tcol < row).astype(BF16), (1, SWA_GROUP))
    sinks2 = swa_sinks.astype(F32) * LOG2E
    sink_rows = jnp.repeat(sinks2.reshape(DEPTH, SWA_KV_HEADS, 1, SWA_GROUP), BLOCK, axis=-1)
    return bias, pmask, sink_rows


_OUTPROJ_TM = 1024
_OUTPROJ_TN = 512


def _outproj_kernel(*refs, final):
    if final:
        ya_ref, yb_ref, x_ref, w_ref, gfin_ref, o_ref = refs
    else:
        ya_ref, yb_ref, x_ref, w_ref, o_ref = refs
    gated = jnp.concatenate([ya_ref[...], yb_ref[...]], axis=-1)
    for c in range(D_MODEL // _OUTPROJ_TN):
        sl = slice(c * _OUTPROJ_TN, (c + 1) * _OUTPROJ_TN)
        o_ref[:, sl] = x_ref[:, sl] + jnp.dot(gated, w_ref[:, sl], preferred_element_type=F32)
    if final:
        o_ref[...] = _rms(o_ref[...], gfin_ref[...])


def _outproj(ya, yb, x2d, w, l, final_g=None):
    t = x2d.shape[0]
    tm = _OUTPROJ_TM
    final = final_g is not None
    in_specs = [
        pl.BlockSpec((tm, SWA_WIDTH), lambda i: (i, 0)),
        pl.BlockSpec((tm, MLA_WIDTH), lambda i: (i, 0)),
        pl.BlockSpec((tm, D_MODEL), lambda i: (i, 0)),
        _layer(w, l),
    ]
    args = [ya, yb, x2d, w]
    if final:
        in_specs.append(_resident(final_g.shape))
        args.append(final_g)
    return pl.pallas_call(
        functools.partial(_outproj_kernel, final=final),
        grid=(t // tm,),
        in_specs=in_specs,
        out_specs=pl.BlockSpec((tm, D_MODEL), lambda i: (i, 0)),
        out_shape=jax.ShapeDtypeStruct((t, D_MODEL), F32),
        compiler_params=_cparams("parallel"),
        name="outproj_final" if final else "outproj",
    )(*args)


def _swap_halves(w):
    half = w.shape[-1] // 2
    return jnp.concatenate([w[..., half:], w[..., :half]], axis=-1)


_PREP_TR = 256


def _prep_w_in_kernel(w_ref, row_ref, col_ref):
    def rows(lo, n):
        return w_ref[0, lo:lo + n, :]

    o = 0
    qa = rows(o, A_Q); o += A_Q
    ka = rows(o, KA_W); o += KA_W
    va = rows(o, VT_ROWS); o += VT_ROWS
    ga = rows(o, 1024); o += 1024
    cq = rows(o, Q_LORA_RANK); o += Q_LORA_RANK
    ckv = rows(o, KV_LORA_RANK); o += KV_LORA_RANK
    half = MLA_ROPE_DIM // 2
    kr = rows(o, MLA_ROPE_DIM)
    kr_sw = jnp.concatenate([rows(o + half, half), rows(o, half)], axis=0)
    o += MLA_ROPE_DIM
    gb = rows(o, 1024)
    col_ref[0, 0:A_Q, :] = qa.astype(BF16)
    col_ref[0, A_Q:COL_W, :] = va.astype(BF16)
    off = 0
    for piece in (ga, gb, cq, ckv, jnp.concatenate([kr, kr_sw, ka], axis=0)):
        n = piece.shape[0]
        row_ref[0, :, off:off + n] = piece.T.astype(BF16)
        off += n


def _prep_w_in(w_in):
    tr = _PREP_TR
    w_t = jnp.swapaxes(w_in, 1, 2)
    return pl.pallas_call(
        _prep_w_in_kernel,
        grid=(DEPTH, D_MODEL // tr),
        in_specs=[pl.BlockSpec((1, IN_WIDTH, tr), lambda l, i: (l, 0, i))],
        out_specs=[
            pl.BlockSpec((1, tr, ROW_W), lambda l, i: (l, i, 0)),
            pl.BlockSpec((1, COL_W, tr), lambda l, i: (l, 0, i)),
        ],
        out_shape=[
            jax.ShapeDtypeStruct((DEPTH, D_MODEL, ROW_W), BF16),
            jax.ShapeDtypeStruct((DEPTH, COL_W, D_MODEL), BF16),
        ],
        compiler_params=_cparams("parallel", "parallel"),
        name="prep_w_in",
    )(w_t)


def _prep_w_q(w_q_b):
    w = w_q_b.reshape(DEPTH, Q_LORA_RANK, MLA_HEADS, MLA_QK_DIM)
    nope = w[..., :MLA_NOPE_DIM].reshape(DEPTH, Q_LORA_RANK, MLA_HEADS * MLA_NOPE_DIM)
    rope = w[..., MLA_NOPE_DIM:].reshape(DEPTH, Q_LORA_RANK, MLA_HEADS * MLA_ROPE_DIM)
    return jnp.swapaxes(jnp.concatenate([nope, rope], axis=-1), 1, 2).astype(BF16)


def _prep_w_kv(w_kv_b):
    w = w_kv_b.reshape(DEPTH, KV_LORA_RANK, MLA_HEADS, MLA_NOPE_DIM + MLA_V_DIM)
    wk = w[..., :MLA_NOPE_DIM].reshape(DEPTH, KV_LORA_RANK, MLA_HEADS * MLA_NOPE_DIM)
    wv = w[..., MLA_NOPE_DIM:].reshape(DEPTH, KV_LORA_RANK, MLA_HEADS * MLA_V_DIM)
    return wk.astype(BF16), jnp.swapaxes(wv, 1, 2).astype(BF16)


def _rope_tables(s):
    pos = jnp.arange(s, dtype=F32)
    inv_freq = ROPE_THETA ** (-jnp.arange(0, MLA_ROPE_DIM, 2, dtype=F32) / MLA_ROPE_DIM)
    ang = pos[:, None] * inv_freq[None, :]
    cos, sin = jnp.cos(ang), jnp.sin(ang)
    cc = jnp.concatenate([cos, cos], axis=-1)
    ss = jnp.concatenate([-sin, sin], axis=-1)
    return jnp.tile(cc, (1, 2)), jnp.tile(ss, (1, 2)), cc.T, ss.T


def kernel(x, attn_norm_g, w_in, swa_sinks, q_a_norm_g, kv_a_norm_g, w_q_b, w_kv_b, w_out, final_norm_g):
    b, s, d = x.shape
    t = b * s
    w_row, w_col = _prep_w_in(w_in)
    w_q_p = _prep_w_q(w_q_b)
    w_k_p, w_vt_p = _prep_w_kv(w_kv_b)
    w_out_p = w_out.astype(BF16)
    cos2, sin2, cost, sint = _rope_tables(s)
    swa_bias, swa_pmask, sink_rows = _swa_tables(swa_sinks)
    g_attn = attn_norm_g[:, None, :]
    g_q = q_a_norm_g[:, None, :]
    g_kv = kv_a_norm_g[:, None, :]

    x2d = x.reshape(t, d)
    for l in range(DEPTH):
        gate, kk, qat, vat, qt, k, vt = _inproj(x2d, g_attn, w_row, w_col, g_q, g_kv, w_q_p, w_k_p, w_vt_p,
                                                cos2, sin2, cost, sint, l, b, s)
        ya = _swa_attn(sink_rows[l], swa_bias, swa_pmask, qat, kk, vat, gate, b, s)
        yb = _mla_attn(qt, k, vt, gate.reshape(b, s, GATE_W))
        final_g = final_norm_g[None] if l == DEPTH - 1 else None
        x2d = _outproj(ya.reshape(t, SWA_WIDTH), yb.reshape(t, MLA_WIDTH), x2d, w_out_p, l, final_g)
    return x2d.reshape(b, s, d)
```

```python
import functools
import math

import jax
import jax.numpy as jnp
from jax import lax
from jax.experimental import pallas as pl
from jax.experimental.pallas import tpu as pltpu

F32 = jnp.float32
BF16 = jnp.bfloat16

D_MODEL = 2048
DEPTH = 4
EPS = 1e-6
BLOCK = 128
WINDOW = 128
NEG = -1e30
LOG2E = math.log2(math.e)

SWA_WIDTH = 1024
SWA_HEAD_DIM = 64
SWA_Q_HEADS = 16
SWA_KV_HEADS = 2
SWA_GROUP = 8

MLA_WIDTH = 1024
MLA_V_DIM = 128
MLA_HEADS = 8
MLA_NOPE_DIM = 128
MLA_ROPE_DIM = 64
MLA_QK_DIM = 192
MLA_QK_PAD = 256
Q_LORA_RANK = 384
KV_LORA_RANK = 256
ROPE_THETA = 10000.0

A_Q = 1024
GATE_W = 2048
IN_WIDTH = 4032
LAT_W = Q_LORA_RANK + KV_LORA_RANK + 2 * MLA_ROPE_DIM
KA_W = SWA_KV_HEADS * SWA_HEAD_DIM
KK_W = 2 * KA_W
VT_ROWS = SWA_KV_HEADS * SWA_HEAD_DIM
ROW_W = GATE_W + LAT_W + KA_W
COL_W = A_Q + VT_ROWS

SWA_QSCALE = SWA_HEAD_DIM ** -0.5 * LOG2E
MLA_QSCALE = MLA_QK_DIM ** -0.5 * LOG2E

VMEM_LIMIT = 56 * 1024 * 1024
NT_DIMS = (((1,), (1,)), ((), ()))


def _cparams(*sem):
    return pltpu.CompilerParams(dimension_semantics=sem, vmem_limit_bytes=VMEM_LIMIT)


def _resident(shape):
    nd = len(shape)
    return pl.BlockSpec(shape, lambda *_: (0,) * nd, pipeline_mode=pl.Buffered(1))


def _layer(stacked, l):
    tail = stacked.shape[1:]
    return pl.BlockSpec((None,) + tail, lambda *_: (l,) + (0,) * len(tail),
                        pipeline_mode=pl.Buffered(1))


def _rms(x, g):
    return x * lax.rsqrt(jnp.mean(x * x, axis=-1, keepdims=True) + EPS) * g


_INPROJ_TM = 512
_INPROJ_TN = 512
_VT_TILE = 256
_VT_AUG = MLA_V_DIM + 16
_LAT0 = GATE_W


def _inproj_kernel(x_ref, g_ref, wr_ref, wc_ref, gq_ref, gkv_ref, wqt_ref, wk_ref, wvt_ref,
                   cos_ref, sin_ref, cost_ref, sint_ref,
                   gate_ref, kk_ref, qat_ref, vat_ref, qt_ref, k_ref, vt_ref):
    tm = x_ref.shape[0]
    x = x_ref[...]
    h = (x * g_ref[...]).astype(BF16)
    r = lax.rsqrt(jnp.mean(x * x, axis=-1, keepdims=True) + EPS)
    r_row = jnp.broadcast_to(r, (tm, 128)).T[0:1]

    lat_a = r * jnp.dot(h, wr_ref[:, _LAT0:_LAT0 + 512], preferred_element_type=F32)
    lat_b = r * jnp.dot(h, wr_ref[:, _LAT0 + 512:ROW_W], preferred_element_type=F32)
    cq = lat_a[:, 0:Q_LORA_RANK]
    ckv = jnp.concatenate([lat_a[:, Q_LORA_RANK:512], lat_b[:, 0:128]], axis=1)
    kslab = lat_b[:, 128:256]
    ka = lat_b[:, 256:384]
    ka_sw = pltpu.roll(ka, SWA_HEAD_DIM, 1)
    low = lax.broadcasted_iota(jnp.int32, ka.shape, 1) < SWA_HEAD_DIM
    kk_ref[:, 0:KA_W] = jnp.where(low, ka, ka_sw).astype(BF16)
    kk_ref[:, KA_W:KK_W] = jnp.where(low, ka_sw, ka).astype(BF16)

    for c in range(GATE_W // _INPROJ_TN):
        cols = slice(c * _INPROJ_TN, (c + 1) * _INPROJ_TN)
        a = r * jnp.dot(h, wr_ref[:, cols], preferred_element_type=F32)
        gate_ref[:, cols] = (a * (1.0 / (1.0 + jnp.exp(-a)))).astype(BF16)
    q_scale = r_row * SWA_QSCALE
    for c in range(A_Q // _INPROJ_TN):
        rows = slice(c * _INPROJ_TN, (c + 1) * _INPROJ_TN)
        a = lax.dot_general(wc_ref[rows, :], h, NT_DIMS, preferred_element_type=F32)
        qat_ref[rows, :] = (a * q_scale).astype(BF16)
    vat_ref[...] = (r_row * lax.dot_general(wc_ref[A_Q:COL_W, :], h, NT_DIMS,
                                            preferred_element_type=F32)).astype(BF16)

    cqn = _rms(cq, gq_ref[...]).astype(BF16)
    ckvn = _rms(ckv, gkv_ref[...]).astype(BF16)
    krot = kslab * cos_ref[...] + pltpu.roll(kslab, MLA_ROPE_DIM, 1) * sin_ref[...]
    krot = jnp.where(low, krot, 0.0).astype(BF16)
    knope = jnp.dot(ckvn, wk_ref[...], preferred_element_type=F32)
    for hd in range(MLA_HEADS):
        k_ref[0, hd, :, 0:128] = knope[:, hd * 128:(hd + 1) * 128].astype(BF16)
        k_ref[0, hd, :, 128:256] = krot

    vt = lax.dot_general(wvt_ref[...], ckvn, NT_DIMS, preferred_element_type=F32)
    ones = jnp.ones((_VT_AUG - MLA_V_DIM, _VT_TILE), BF16)
    for hd in range(MLA_HEADS):
        for c in range(tm // _VT_TILE):
            vt_ref[0, hd, c, 0:MLA_V_DIM] = vt[hd * 128:(hd + 1) * 128, c * _VT_TILE:(c + 1) * _VT_TILE].astype(BF16)
            vt_ref[0, hd, c, MLA_V_DIM:_VT_AUG] = ones

    qt = lax.dot_general(wqt_ref[...], cqn, NT_DIMS, preferred_element_type=F32)
    cost = cost_ref[...]
    sint = sint_ref[...]
    pad = jnp.zeros((MLA_QK_PAD - MLA_QK_DIM, tm), BF16)
    hr = MLA_ROPE_DIM // 2
    for hd in range(MLA_HEADS):
        r0 = 1024 + hd * MLA_ROPE_DIM
        rope = qt[r0:r0 + MLA_ROPE_DIM]
        rope_sw = jnp.concatenate([rope[hr:], rope[:hr]], axis=0)
        qr = (rope * cost + rope_sw * sint) * MLA_QSCALE
        qt_ref[0, hd, 0, 0:128, :] = (qt[hd * 128:(hd + 1) * 128] * MLA_QSCALE).astype(BF16)
        qt_ref[0, hd, 0, 128:192, :] = qr.astype(BF16)
        qt_ref[0, hd, 0, 192:256, :] = pad


def _inproj(x2d, g, w_row, w_col, gq, gkv, wqt, wk, wvt, cos2, sin2, cost, sint, l, b, s):
    t = x2d.shape[0]
    tm = _INPROJ_TM
    nt = s // tm
    nvt = tm // _VT_TILE
    return pl.pallas_call(
        _inproj_kernel,
        grid=(t // tm,),
        in_specs=[
            pl.BlockSpec((tm, D_MODEL), lambda i: (i, 0)),
            _layer(g, l),
            _layer(w_row, l),
            _layer(w_col, l),
            _layer(gq, l),
            _layer(gkv, l),
            _layer(wqt, l),
            _layer(wk, l),
            _layer(wvt, l),
            pl.BlockSpec((tm, 128), lambda i: (i % nt, 0)),
            pl.BlockSpec((tm, 128), lambda i: (i % nt, 0)),
            pl.BlockSpec((MLA_ROPE_DIM, tm), lambda i: (0, i % nt)),
            pl.BlockSpec((MLA_ROPE_DIM, tm), lambda i: (0, i % nt)),
        ],
        out_specs=[
            pl.BlockSpec((tm, GATE_W), lambda i: (i, 0)),
            pl.BlockSpec((tm, KK_W), lambda i: (i, 0)),
            pl.BlockSpec((A_Q, tm), lambda i: (0, i)),
            pl.BlockSpec((VT_ROWS, tm), lambda i: (0, i)),
            pl.BlockSpec((1, MLA_HEADS, 1, MLA_QK_PAD, tm), lambda i: (i // nt, 0, i % nt, 0, 0)),
            pl.BlockSpec((1, MLA_HEADS, tm, MLA_QK_PAD), lambda i: (i // nt, 0, i % nt, 0)),
            pl.BlockSpec((1, MLA_HEADS, nvt, _VT_AUG, _VT_TILE), lambda i: (i // nt, 0, i % nt, 0, 0)),
        ],
        out_shape=[
            jax.ShapeDtypeStruct((t, GATE_W), BF16),
            jax.ShapeDtypeStruct((t, KK_W), BF16),
            jax.ShapeDtypeStruct((A_Q, t), BF16),
            jax.ShapeDtypeStruct((VT_ROWS, t), BF16),
            jax.ShapeDtypeStruct((b, MLA_HEADS, nt, MLA_QK_PAD, tm), BF16),
            jax.ShapeDtypeStruct((b, MLA_HEADS, s, MLA_QK_PAD), BF16),
            jax.ShapeDtypeStruct((b, MLA_HEADS, s // _VT_TILE, _VT_AUG, _VT_TILE), BF16),
        ],
        compiler_params=_cparams("parallel"),
        name="inproj",
    )(x2d, g, w_row, w_col, gq, gkv, wqt, wk, wvt, cos2, sin2, cost, sint)


_MLA_TQ = 512
_MLA_HP = 2
_MLA_LOOP_STAGES = 12


def _mla_tile_order(nq):
    off_diag = [(qi, j) for qi in range(nq) for j in range(qi)]
    return off_diag, [(qi, qi) for qi in range(nq)]


def _mla_attn_kernel(order_ref, qt_ref, k_ref, vt_ref, gate_ref, o_ref, *scratch):
    m_refs = scratch[0:_MLA_HP]
    acc_refs = scratch[_MLA_HP:2 * _MLA_HP]
    st_ref, mx_ref = scratch[2 * _MLA_HP:]
    nq, tq = qt_ref.shape[2], qt_ref.shape[4]
    nsub = tq // _VT_TILE

    def scores(hh, qi, j, slot):
        kt = k_ref[0, hh, pl.ds(pl.multiple_of(j * tq, tq), tq), :]
        st = jnp.dot(kt, qt_ref[0, hh, qi], preferred_element_type=F32)
        st_ref[slot, hh] = st
        mx_ref[slot, hh] = jnp.max(st, axis=0, keepdims=True)

    def update(hh, qi, j, slot):
        m_old = m_refs[hh][qi]
        m_new = jnp.maximum(m_old, mx_ref[slot, hh])
        alpha = jnp.exp2(m_old - m_new)
        pb = jnp.exp2(st_ref[slot, hh] - m_new).astype(BF16)
        m_refs[hh][qi] = m_new
        pv = jnp.dot(vt_ref[0, hh, j * nsub], pb[0:_VT_TILE], preferred_element_type=F32)
        for c in range(1, nsub):
            pv += jnp.dot(vt_ref[0, hh, j * nsub + c], pb[c * _VT_TILE:(c + 1) * _VT_TILE],
                          preferred_element_type=F32)
        acc_refs[hh][qi] = alpha * acc_refs[hh][qi] + pv

    hq = _VT_TILE
    assert tq == 2 * hq

    def causal(block):
        kpos = lax.broadcasted_iota(jnp.int32, block.shape, 0)
        qpos = lax.broadcasted_iota(jnp.int32, block.shape, 1)
        return jnp.where(kpos <= qpos, block, NEG)

    def scores_diag(hh, qi, slot):
        base = qi * tq
        qt = qt_ref[0, hh, qi]
        top = jnp.dot(k_ref[0, hh, base:base + hq, :], qt, preferred_element_type=F32)
        top = jnp.concatenate([causal(top[:, 0:hq]), top[:, hq:tq]], axis=1)
        bot = causal(jnp.dot(k_ref[0, hh, base + hq:base + tq, :], qt[:, hq:tq],
                             preferred_element_type=F32))
        st_ref[slot, hh, 0:hq, :] = top
        st_ref[slot, hh, hq:tq, hq:tq] = bot
        mtop = jnp.max(top, axis=0, keepdims=True)
        mbot = jnp.max(bot, axis=0, keepdims=True)
        mx_ref[slot, hh] = jnp.concatenate([mtop[:, 0:hq], jnp.maximum(mtop[:, hq:tq], mbot)], axis=1)

    def update_diag(hh, qi, slot):
        m_new = mx_ref[slot, hh]
        m_refs[hh][qi] = m_new
        p_top = jnp.exp2(st_ref[slot, hh, 0:hq, :] - m_new).astype(BF16)
        p_bot = jnp.exp2(st_ref[slot, hh, hq:tq, hq:tq] - m_new[:, hq:tq]).astype(BF16)
        pv = jnp.dot(vt_ref[0, hh, qi * nsub], p_top, preferred_element_type=F32)
        pv_r = jnp.dot(vt_ref[0, hh, qi * nsub + 1], p_bot, preferred_element_type=F32)
        acc = acc_refs[hh]
        acc[qi, :, 0:hq] = pv[:, 0:hq]
        acc[qi, :, hq:tq] = pv[:, hq:tq] + pv_r

    def stage(cur, nxt, slot, diag_cur, diag_next):
        for hh in range(_MLA_HP):
            if nxt is not None and diag_next:
                scores_diag(hh, nxt[0], 1 - slot)
            elif nxt is not None:
                scores(hh, nxt[0], nxt[1], 1 - slot)
            if diag_cur:
                update_diag(hh, cur[0], slot)
            else:
                update(hh, cur[0], cur[1], slot)

    off_diag, diag = _mla_tile_order(nq)
    assert len(diag) % 2 == 0 and _MLA_LOOP_STAGES % 2 == 0
    n_loop = (len(off_diag) - 1) // _MLA_LOOP_STAGES * _MLA_LOOP_STAGES

    def static_stages(tiles, g0, then):
        for i, cur in enumerate(tiles):
            nxt = tiles[i + 1] if i + 1 < len(tiles) else then
            stage(cur, nxt, (g0 + i) % 2, cur[0] == cur[1], nxt is not None and nxt[0] == nxt[1])

    for hh in range(_MLA_HP):
        scores_diag(hh, diag[0][0], 0)
    static_stages(diag, 0, off_diag[0])

    def body(t, carry):
        for u in range(_MLA_LOOP_STAGES):
            n = t * _MLA_LOOP_STAGES + u
            cur = (order_ref[0, n], order_ref[1, n])
            nxt = (order_ref[0, n + 1], order_ref[1, n + 1])
            stage(cur, nxt, u % 2, False, False)
        return carry

    lax.fori_loop(0, n_loop // _MLA_LOOP_STAGES, body, 0)
    static_stages(off_diag[n_loop:], len(diag) + n_loop, None)

    for qi in range(nq):
        for hh in range(_MLA_HP):
            out = acc_refs[hh][qi, 0:MLA_V_DIM] / acc_refs[hh][qi, MLA_V_DIM:MLA_V_DIM + 1]
            rows, cols = slice(qi * tq, (qi + 1) * tq), slice(hh * MLA_V_DIM, (hh + 1) * MLA_V_DIM)
            o_ref[0, rows, cols] = out.T.astype(BF16) * gate_ref[0, rows, cols]


def _mla_attn(qt, k, vt, gate):
    b, h, nq, _, tq = qt.shape
    s = nq * tq
    hp = _MLA_HP
    off_diag, _ = _mla_tile_order(nq)
    order = jnp.asarray(list(zip(*off_diag)), jnp.int32)
    return pl.pallas_call(
        _mla_attn_kernel,
        grid=(b, h // hp),
        in_specs=[
            pl.BlockSpec(memory_space=pltpu.SMEM),
            pl.BlockSpec((1, hp, nq, MLA_QK_PAD, tq), lambda bi, hi: (bi, hi, 0, 0, 0)),
            pl.BlockSpec((1, hp, s, MLA_QK_PAD), lambda bi, hi: (bi, hi, 0, 0)),
            pl.BlockSpec((1, hp, s // _VT_TILE, _VT_AUG, _VT_TILE), lambda bi, hi: (bi, hi, 0, 0, 0)),
            pl.BlockSpec((1, s, hp * MLA_V_DIM), lambda bi, hi: (bi, 0, SWA_WIDTH // (hp * MLA_V_DIM) + hi)),
        ],
        out_specs=pl.BlockSpec((1, s, hp * MLA_V_DIM), lambda bi, hi: (bi, 0, hi)),
        out_shape=jax.ShapeDtypeStruct((b, s, MLA_WIDTH), BF16),
        scratch_shapes=(
            [pltpu.VMEM((nq, 1, tq), F32) for _ in range(hp)]
            + [pltpu.VMEM((nq, _VT_AUG, tq), F32) for _ in range(hp)]
            + [pltpu.VMEM((2, hp, tq, tq), F32), pltpu.VMEM((2, hp, 1, tq), F32)]
        ),
        compiler_params=_cparams("parallel", "parallel"),
        name="mla_attn",
    )(order, qt, k, vt, gate)


_SWA_TQ = 2048
_SWA_NB = _SWA_TQ // BLOCK
_SWA_GW = SWA_GROUP * BLOCK


def _swa_kernel(sink_ref, bias_ref, pmask_ref, qt_ref, kc_ref, kp_ref, vc_ref, vp_ref, gate_ref,
                o_ref, kbuf_ref, vbuf_ref):
    tq = kc_ref.shape[1]
    kbuf_ref[0:BLOCK] = kp_ref[0]
    kbuf_ref[BLOCK:BLOCK + tq] = kc_ref[0]
    vbuf_ref[:, 0:BLOCK] = vp_ref[...]
    vbuf_ref[:, BLOCK:BLOCK + tq] = vc_ref[...]
    first_variant = jnp.where(pl.program_id(1) == 0, 1, 0)
    half = SWA_HEAD_DIM
    zeros = jnp.zeros((half, BLOCK), BF16)
    ones = jnp.ones((16, 2 * BLOCK), BF16)
    r_io =lax.broadcasted_iota(jnp.int32, (BLOCK, _SWA_GW), 0)
    t_io = lax.broadcasted_iota(jnp.int32, (BLOCK, _SWA_GW), 1) & (BLOCK - 1)
    prev_visible = t_io < r_io

    def scores(blk, hk):
        r0 = blk * BLOCK
        kwin = kbuf_ref[r0:r0 + 2 * BLOCK, hk * BLOCK:(hk + 1) * BLOCK]
        cols = []
        for p in range(SWA_GROUP // 2):
            row = hk * (SWA_GROUP * half) + p * BLOCK
            qt = qt_ref[row:row + BLOCK, r0:r0 + BLOCK]
            cols.append(jnp.concatenate([qt[0:half], zeros], axis=0))
            cols.append(jnp.concatenate([zeros, qt[half:BLOCK]], axis=0))
        qz = jnp.concatenate(cols, axis=1)
        st = jnp.dot(kwin, qz, preferred_element_type=F32)
        variant = first_variant if blk == 0 else 0
        return jnp.where(prev_visible, st[0:BLOCK], st[BLOCK:2 * BLOCK]) + bias_ref[variant, hk]

    def finish(blk, hk, s2):
        r0 = blk * BLOCK
        sink = sink_ref[hk]
        m = jnp.maximum(jnp.max(s2, axis=0, keepdims=True), sink)
        pb = jnp.exp2(s2 - m).astype(BF16)
        p_prev = pb * pmask_ref[...]
        pstack = jnp.concatenate([p_prev, pb - p_prev], axis=0)
        vwin = jnp.concatenate([vbuf_ref[hk * half:(hk + 1) * half, r0:r0 + 2 * BLOCK], ones], axis=0)
        ot = jnp.dot(vwin, pstack, preferred_element_type=F32)
        l = ot[half:half + 1] + jnp.exp2(sink - m)
        ot = ot[0:half] * (1.0 / l)
        for pr in range(SWA_GROUP // 2):
            c0 = 2 * pr * BLOCK
            pair_t = jnp.concatenate([ot[:, c0:c0 + BLOCK], ot[:, c0 + BLOCK:c0 + 2 * BLOCK]], axis=0)
            col = hk * (SWA_GROUP * half) + pr * BLOCK
            gated = pair_t.T.astype(BF16) * gate_ref[r0:r0 + BLOCK, col:col + BLOCK]
            o_ref[0, r0:r0 + BLOCK, col:col + BLOCK] = gated

    chains = [(blk, hk) for blk in range(tq // BLOCK) for hk in range(SWA_KV_HEADS)]
    s_next = scores(*chains[0])
    for c, chain in enumerate(chains):
        s_cur = s_next
        if c + 1 < len(chains):
            s_next = scores(*chains[c + 1])
        finish(*chain, s_cur)


def _swa_attn(sink_rows, bias, pmask, qat, kk, vt, gate, b, s):
    tq = _SWA_TQ
    nq = s // tq
    nb = _SWA_NB

    def prev_block(i):
        return jnp.maximum(i * nb - 1, 0)

    return pl.pallas_call(
        _swa_kernel,
        grid=(b, nq),
        in_specs=[
            _resident(sink_rows.shape),
            _resident(bias.shape),
            _resident(pmask.shape),
            pl.BlockSpec((A_Q, tq), lambda bi, i: (0, bi * nq + i)),
            pl.BlockSpec((1, tq, KK_W), lambda bi, i: (bi, i, 0)),
            pl.BlockSpec((1, BLOCK, KK_W), lambda bi, i: (bi, prev_block(i), 0)),
            pl.BlockSpec((VT_ROWS, tq), lambda bi, i: (0, bi * nq + i)),
            pl.BlockSpec((VT_ROWS, BLOCK), lambda bi, i: (0, bi * nq * nb + prev_block(i))),
            pl.BlockSpec((tq, SWA_WIDTH), lambda bi, i: (bi * nq + i, 0)),
        ],
        out_specs=pl.BlockSpec((1, tq, SWA_WIDTH), lambda bi, i: (bi, i, 0)),
        out_shape=jax.ShapeDtypeStruct((b, s, SWA_WIDTH), BF16),
        scratch_shapes=[
            pltpu.VMEM((tq + BLOCK, KK_W), BF16),
            pltpu.VMEM((VT_ROWS, tq + BLOCK), BF16),
        ],
        compiler_params=_cparams("parallel", "parallel"),
        name="swa_attn",
    )(sink_rows, bias, pmask, qat, kk.reshape(b, s, KK_W), kk.reshape(b, s, KK_W), vt, vt, gate)


def _swa_tables(swa_sinks):
    row = jnp.arange(BLOCK, dtype=jnp.int32)[:, None]
    tcol = jnp.arange(BLOCK, dtype=jnp.int32)[None, :]
    delta = jnp.where(tcol < row, BLOCK + tcol - row, tcol - row)
    slopes = jnp.exp2(-8.0 * jnp.arange(1, SWA_Q_HEADS + 1, dtype=F32) / SWA_Q_HEADS)
    slopes = slopes.reshape(SWA_KV_HEADS, SWA_GROUP)
    has_prev = -(slopes * LOG2E)[:, :, None, None] * delta.astype(F32)[None, None]
    no_prev = jnp.where((tcol < row)[None, None], NEG, has_prev)
    bias = jnp.stack([has_prev, no_prev])
    bias = bias.transpose(0, 1, 3, 2, 4).reshape(2, SWA_KV_HEADS, BLOCK, _SWA_GW)
    pmask = jnp.tile((tcol < row).astype(BF16), (1, SWA_GROUP))
    sinks2 = swa_sinks.astype(F32) * LOG2E
    sink_rows = jnp.repeat(sinks2.reshape(DEPTH, SWA_KV_HEADS, 1, SWA_GROUP), BLOCK, axis=-1)
    return bias, pmask, sink_rows


_OUTPROJ_TM = 1024
_OUTPROJ_TN = 512


def _outproj_kernel(*refs, final):
    if final:
        ya_ref, yb_ref, x_ref, w_ref, gfin_ref, o_ref = refs
    else:
        ya_ref, yb_ref, x_ref, w_ref, o_ref = refs
    gated = jnp.concatenate([ya_ref[...], yb_ref[...]], axis=-1)
    for c in range(D_MODEL // _OUTPROJ_TN):
        sl = slice(c * _OUTPROJ_TN, (c + 1) * _OUTPROJ_TN)
        o_ref[:, sl] = x_ref[:, sl] + jnp.dot(gated, w_ref[:, sl], preferred_element_type=F32)
    if final:
        o_ref[...] = _rms(o_ref[...], gfin_ref[...])


def _outproj(ya, yb, x2d, w, l, final_g=None):
    t = x2d.shape[0]
    tm = _OUTPROJ_TM
    final = final_g is not None
    in_specs = [
        pl.BlockSpec((tm, SWA_WIDTH), lambda i: (i, 0)),
        pl.BlockSpec((tm, MLA_WIDTH), lambda i: (i, 0)),
        pl.BlockSpec((tm, D_MODEL), lambda i: (i, 0)),
        _layer(w, l),
    ]
    args = [ya, yb, x2d, w]
    if final:
        in_specs.append(_resident(final_g.shape))
        args.append(final_g)
    return pl.pallas_call(
        functools.partial(_outproj_kernel, final=final),
        grid=(t // tm,),
        in_specs=in_specs,
        out_specs=pl.BlockSpec((tm, D_MODEL), lambda i: (i, 0)),
        out_shape=jax.ShapeDtypeStruct((t, D_MODEL), F32),
        compiler_params=_cparams("parallel"),
        name="outproj_final" if final else "outproj",
    )(*args)


def _swap_halves(w):
    half = w.shape[-1] // 2
    return jnp.concatenate([w[..., half:], w[..., :half]], axis=-1)


_PREP_TR = 256


def _prep_w_in_kernel(w_ref, row_ref, col_ref):
    def rows(lo, n):
        return w_ref[0, lo:lo + n, :]

    o = 0
    qa = rows(o, A_Q); o += A_Q
    ka = rows(o, KA_W); o += KA_W
    va = rows(o, VT_ROWS); o += VT_ROWS
    ga = rows(o, 1024); o += 1024
    cq = rows(o, Q_LORA_RANK); o += Q_LORA_RANK
    ckv = rows(o, KV_LORA_RANK); o += KV_LORA_RANK
    half = MLA_ROPE_DIM // 2
    kr = rows(o, MLA_ROPE_DIM)
    kr_sw = jnp.concatenate([rows(o + half, half), rows(o, half)], axis=0)
    o += MLA_ROPE_DIM
    gb = rows(o, 1024)
    col_ref[0, 0:A_Q, :] = qa.astype(BF16)
    col_ref[0, A_Q:COL_W, :] = va.astype(BF16)
    off = 0
    for piece in (ga, gb, cq, ckv, jnp.concatenate([kr, kr_sw, ka], axis=0)):
        n = piece.shape[0]
        row_ref[0, :, off:off + n] = piece.T.astype(BF16)
        off += n


def _prep_w_in(w_in):
    tr = _PREP_TR
    w_t = jnp.swapaxes(w_in, 1, 2)
    return pl.pallas_call(
        _prep_w_in_kernel,
        grid=(DEPTH, D_MODEL // tr),
        in_specs=[pl.BlockSpec((1, IN_WIDTH, tr), lambda l, i: (l, 0, i))],
        out_specs=[
            pl.BlockSpec((1, tr, ROW_W), lambda l, i: (l, i, 0)),
            pl.BlockSpec((1, COL_W, tr), lambda l, i: (l, 0, i)),
        ],
        out_shape=[
            jax.ShapeDtypeStruct((DEPTH, D_MODEL, ROW_W), BF16),
            jax.ShapeDtypeStruct((DEPTH, COL_W, D_MODEL), BF16),
        ],
        compiler_params=_cparams("parallel", "parallel"),
        name="prep_w_in",
    )(w_t)


def _prep_w_q(w_q_b):
    w = w_q_b.reshape(DEPTH, Q_LORA_RANK, MLA_HEADS, MLA_QK_DIM)
    nope = w[..., :MLA_NOPE_DIM].reshape(DEPTH, Q_LORA_RANK, MLA_HEADS * MLA_NOPE_DIM)
    rope = w[..., MLA_NOPE_DIM:].reshape(DEPTH, Q_LORA_RANK, MLA_HEADS * MLA_ROPE_DIM)
    return jnp.swapaxes(jnp.concatenate([nope, rope], axis=-1), 1, 2).astype(BF16)


def _prep_w_kv(w_kv_b):
    w = w_kv_b.reshape(DEPTH, KV_LORA_RANK, MLA_HEADS, MLA_NOPE_DIM + MLA_V_DIM)
    wk = w[..., :MLA_NOPE_DIM].reshape(DEPTH, KV_LORA_RANK, MLA_HEADS * MLA_NOPE_DIM)
    wv = w[..., MLA_NOPE_DIM:].reshape(DEPTH, KV_LORA_RANK, MLA_HEADS * MLA_V_DIM)
    return wk.astype(BF16), jnp.swapaxes(wv, 1, 2).astype(BF16)


def _rope_tables(s):
    pos = jnp.arange(s, dtype=F32)
    inv_freq = ROPE_THETA ** (-jnp.arange(0, MLA_ROPE_DIM, 2, dtype=F32) / MLA_ROPE_DIM)
    ang = pos[:, None] * inv_freq[None, :]
    cos, sin = jnp.cos(ang), jnp.sin(ang)
    cc = jnp.concatenate([cos, cos], axis=-1)
    ss = jnp.concatenate([-sin, sin], axis=-1)
    return jnp.tile(cc, (1, 2)), jnp.tile(ss, (1, 2)), cc.T, ss.T


def kernel(x, attn_norm_g, w_in, swa_sinks, q_a_norm_g, kv_a_norm_g, w_q_b, w_kv_b, w_out, final_norm_g):
    b, s, d = x.shape
    t = b * s
    w_row, w_col = _prep_w_in(w_in)
    w_q_p = _prep_w_q(w_q_b)
    w_k_p, w_vt_p = _prep_w_kv(w_kv_b)
    w_out_p = w_out.astype(BF16)
    cos2, sin2, cost, sint = _rope_tables(s)
    swa_bias, swa_pmask, sink_rows = _swa_tables(swa_sinks)
    g_attn = attn_norm_g[:, None, :]
    g_q = q_a_norm_g[:, None, :]
    g_kv = kv_a_norm_g[:, None, :]

    x2d = x.reshape(t, d)
    for l in range(DEPTH):
        gate, kk, qat, vat, qt, k, vt = _inproj(x2d, g_attn, w_row, w_col, g_q, g_kv, w_q_p, w_k_p, w_vt_p,
                                                cos2, sin2, cost, sint, l, b, s)
        ya = _swa_attn(sink_rows[l], swa_bias, swa_pmask, qat, kk, vat, gate, b, s)
        yb = _mla_attn(qt, k, vt, gate.reshape(b, s, GATE_W))
        final_g = final_norm_g[None] if l == DEPTH - 1 else None
        x2d = _outproj(ya.reshape(t, SWA_WIDTH), yb.reshape(t, MLA_WIDTH), x2d, w_out_p, l, final_g)
    return x2d.reshape(b, s, d)
```

```python
import functools
import math

import jax
import jax.numpy as jnp
from jax import lax
from jax.experimental import pallas as pl
from jax.experimental.pallas import tpu as pltpu

F32 = jnp.float32
BF16 = jnp.bfloat16

D_MODEL = 2048
DEPTH = 4
EPS = 1e-6
BLOCK = 128
WINDOW = 128
NEG = -1e30
LOG2E = math.log2(math.e)

SWA_WIDTH = 1024
SWA_HEAD_DIM = 64
SWA_Q_HEADS = 16
SWA_KV_HEADS = 2
SWA_GROUP = 8

MLA_WIDTH = 1024
MLA_V_DIM = 128
MLA_HEADS = 8
MLA_NOPE_DIM = 128
MLA_ROPE_DIM = 64
MLA_QK_DIM = 192
MLA_QK_PAD = 256
Q_LORA_RANK = 384
KV_LORA_RANK = 256
ROPE_THETA = 10000.0

A_Q = 1024
GATE_W = 2048
IN_WIDTH = 4032
LAT_W = Q_LORA_RANK + KV_LORA_RANK + 2 * MLA_ROPE_DIM
KA_W = SWA_KV_HEADS * SWA_HEAD_DIM
KK_W = 2 * KA_W
VT_ROWS = SWA_KV_HEADS * SWA_HEAD_DIM
ROW_W = GATE_W + LAT_W + KA_W
COL_W = A_Q + VT_ROWS

SWA_QSCALE = SWA_HEAD_DIM ** -0.5 * LOG2E
MLA_QSCALE = MLA_QK_DIM ** -0.5 * LOG2E

VMEM_LIMIT = 56 * 1024 * 1024
NT_DIMS = (((1,), (1,)), ((), ()))


def _cparams(*sem):
    return pltpu.CompilerParams(dimension_semantics=sem, vmem_limit_bytes=VMEM_LIMIT)


def _resident(shape):
    nd = len(shape)
    return pl.BlockSpec(shape, lambda *_: (0,) * nd, pipeline_mode=pl.Buffered(1))


def _layer(stacked, l):
    tail = stacked.shape[1:]
    return pl.BlockSpec((None,) + tail, lambda *_: (l,) + (0,) * len(tail),
                        pipeline_mode=pl.Buffered(1))


def _rms(x, g):
    return x * lax.rsqrt(jnp.mean(x * x, axis=-1, keepdims=True) + EPS) * g


_INPROJ_TM = 512
_INPROJ_TN = 512
_VT_TILE = 256
_VT_AUG = MLA_V_DIM + 16
_LAT0 = GATE_W


def _inproj_kernel(x_ref, g_ref, wr_ref, wc_ref, gq_ref, gkv_ref, wqt_ref, wk_ref, wvt_ref,
                   cos_ref, sin_ref, cost_ref, sint_ref,
                   gate_ref, kk_ref, qat_ref, vat_ref, qt_ref, k_ref, vt_ref):
    tm = x_ref.shape[0]
    x = x_ref[...]
    h = (x * g_ref[...]).astype(BF16)
    r = lax.rsqrt(jnp.mean(x * x, axis=-1, keepdims=True) + EPS)
    r_row = jnp.broadcast_to(r, (tm, 128)).T[0:1]

    lat_a = r * jnp.dot(h, wr_ref[:, _LAT0:_LAT0 + 512], preferred_element_type=F32)
    lat_b = r * jnp.dot(h, wr_ref[:, _LAT0 + 512:ROW_W], preferred_element_type=F32)
    cq = lat_a[:, 0:Q_LORA_RANK]
    ckv = jnp.concatenate([lat_a[:, Q_LORA_RANK:512], lat_b[:, 0:128]], axis=1)
    kslab = lat_b[:, 128:256]
    ka = lat_b[:, 256:384]
    ka_sw = pltpu.roll(ka, SWA_HEAD_DIM, 1)
    low = lax.broadcasted_iota(jnp.int32, ka.shape, 1) < SWA_HEAD_DIM
    kk_ref[:, 0:KA_W] = jnp.where(low, ka, ka_sw).astype(BF16)
    kk_ref[:, KA_W:KK_W] = jnp.where(low, ka_sw, ka).astype(BF16)

    for c in range(GATE_W // _INPROJ_TN):
        cols = slice(c * _INPROJ_TN, (c + 1) * _INPROJ_TN)
        a = r * jnp.dot(h, wr_ref[:, cols], preferred_element_type=F32)
        gate_ref[:, cols] = (a * (1.0 / (1.0 + jnp.exp(-a)))).astype(BF16)
    q_scale = r_row * SWA_QSCALE
    for c in range(A_Q // _INPROJ_TN):
        rows = slice(c * _INPROJ_TN, (c + 1) * _INPROJ_TN)
        a = lax.dot_general(wc_ref[rows, :], h, NT_DIMS, preferred_element_type=F32)
        qat_ref[rows, :] = (a * q_scale).astype(BF16)
    vat_ref[...] = (r_row * lax.dot_general(wc_ref[A_Q:COL_W, :], h, NT_DIMS,
                                            preferred_element_type=F32)).astype(BF16)

    cqn = _rms(cq, gq_ref[...]).astype(BF16)
    ckvn = _rms(ckv, gkv_ref[...]).astype(BF16)
    krot = kslab * cos_ref[...] + pltpu.roll(kslab, MLA_ROPE_DIM, 1) * sin_ref[...]
    krot = jnp.where(low, krot, 0.0).astype(BF16)
    knope = jnp.dot(ckvn, wk_ref[...], preferred_element_type=F32)
    for hd in range(MLA_HEADS):
        k_ref[0, hd, :, 0:128] = knope[:, hd * 128:(hd + 1) * 128].astype(BF16)
        k_ref[0, hd, :, 128:256] = krot

    vt = lax.dot_general(wvt_ref[...], ckvn, NT_DIMS, preferred_element_type=F32)
    ones = jnp.ones((_VT_AUG - MLA_V_DIM, _VT_TILE), BF16)
    for hd in range(MLA_HEADS):
        for c in range(tm // _VT_TILE):
            vt_ref[0, hd, c, 0:MLA_V_DIM] = vt[hd * 128:(hd + 1) * 128, c * _VT_TILE:(c + 1) * _VT_TILE].astype(BF16)
            vt_ref[0, hd, c, MLA_V_DIM:_VT_AUG] = ones

    qt = lax.dot_general(wqt_ref[...], cqn, NT_DIMS, preferred_element_type=F32)
    cost = cost_ref[...]
    sint = sint_ref[...]
    pad = jnp.zeros((MLA_QK_PAD - MLA_QK_DIM, tm), BF16)
    hr = MLA_ROPE_DIM // 2
    for hd in range(MLA_HEADS):
        r0 = 1024 + hd * MLA_ROPE_DIM
        rope = qt[r0:r0 + MLA_ROPE_DIM]
        rope_sw = jnp.concatenate([rope[hr:], rope[:hr]], axis=0)
        qr = (rope * cost + rope_sw * sint) * MLA_QSCALE
        qt_ref[0, hd, 0, 0:128, :] = (qt[hd * 128:(hd + 1) * 128] * MLA_QSCALE).astype(BF16)
        qt_ref[0, hd, 0, 128:192, :] = qr.astype(BF16)
        qt_ref[0, hd, 0, 192:256, :] = pad


def _inproj(x2d, g, w_row, w_col, gq, gkv, wqt, wk, wvt, cos2, sin2, cost, sint, l, b, s):
    t = x2d.shape[0]
    tm = _INPROJ_TM
    assert tm == _MLA_TQ
    nt = s // tm
    nvt = tm // _VT_TILE
    return pl.pallas_call(
        _inproj_kernel,
        grid=(t // tm,),
        in_specs=[
            pl.BlockSpec((tm, D_MODEL), lambda i: (i, 0)),
            _layer(g, l),
            _layer(w_row, l),
            _layer(w_col, l),
            _layer(gq, l),
            _layer(gkv, l),
            _layer(wqt, l),
            _layer(wk, l),
            _layer(wvt, l),
            pl.BlockSpec((tm, 128), lambda i: (i % nt, 0)),
            pl.BlockSpec((tm, 128), lambda i: (i % nt, 0)),
            pl.BlockSpec((MLA_ROPE_DIM, tm), lambda i: (0, i % nt)),
            pl.BlockSpec((MLA_ROPE_DIM, tm), lambda i: (0, i % nt)),
        ],
        out_specs=[
            pl.BlockSpec((tm, GATE_W), lambda i: (i, 0)),
            pl.BlockSpec((tm, KK_W), lambda i: (i, 0)),
            pl.BlockSpec((A_Q, tm), lambda i: (0, i)),
            pl.BlockSpec((VT_ROWS, tm), lambda i: (0, i)),
            pl.BlockSpec((1, MLA_HEADS, 1, MLA_QK_PAD, tm), lambda i: (i // nt, 0, i % nt, 0, 0)),
            pl.BlockSpec((1, MLA_HEADS, tm, MLA_QK_PAD), lambda i: (i // nt, 0, i % nt, 0)),
            pl.BlockSpec((1, MLA_HEADS, nvt, _VT_AUG, _VT_TILE), lambda i: (i // nt, 0, i % nt, 0, 0)),
        ],
        out_shape=[
            jax.ShapeDtypeStruct((t, GATE_W), BF16),
            jax.ShapeDtypeStruct((t, KK_W), BF16),
            jax.ShapeDtypeStruct((A_Q, t), BF16),
            jax.ShapeDtypeStruct((VT_ROWS, t), BF16),
            jax.ShapeDtypeStruct((b, MLA_HEADS, nt, MLA_QK_PAD, tm), BF16),
            jax.ShapeDtypeStruct((b, MLA_HEADS, s, MLA_QK_PAD), BF16),
            jax.ShapeDtypeStruct((b, MLA_HEADS, s // _VT_TILE, _VT_AUG, _VT_TILE), BF16),
        ],
        compiler_params=_cparams("parallel"),
        name="inproj",
    )(x2d, g, w_row, w_col, gq, gkv, wqt, wk, wvt, cos2, sin2, cost, sint)


_MLA_TQ = 512
_MLA_HP = 2
_MLA_LOOP_STAGES = 12


def _mla_tile_order(nq):
    off_diag = [(qi, j) for qi in range(nq) for j in range(qi)]
    return off_diag, [(qi, qi) for qi in range(nq)]


def _mla_attn_kernel(order_ref, qt_ref, k_ref, vt_ref, gate_ref, o_ref, *scratch):
    m_refs = scratch[0:_MLA_HP]
    acc_refs = scratch[_MLA_HP:2 * _MLA_HP]
    st_ref, mx_ref = scratch[2 * _MLA_HP:]
    nq, tq = qt_ref.shape[2], qt_ref.shape[4]
    nsub = tq // _VT_TILE

    def scores(hh, qi, j, slot):
        kt = k_ref[0, hh, pl.ds(pl.multiple_of(j * tq, tq), tq), :]
        st = jnp.dot(kt, qt_ref[0, hh, qi], preferred_element_type=F32)
        st_ref[slot, hh] = st
        mx_ref[slot, hh] = jnp.max(st, axis=0, keepdims=True)

    def update(hh, qi, j, slot):
        m_old = m_refs[hh][qi]
        m_new = jnp.maximum(m_old, mx_ref[slot, hh])
        alpha = jnp.exp2(m_old - m_new)
        pb = jnp.exp2(st_ref[slot, hh] - m_new).astype(BF16)
        m_refs[hh][qi] = m_new
        pv = jnp.dot(vt_ref[0, hh, j * nsub], pb[0:_VT_TILE], preferred_element_type=F32)
        for c in range(1, nsub):
            pv += jnp.dot(vt_ref[0, hh, j * nsub + c], pb[c * _VT_TILE:(c + 1) * _VT_TILE],
                          preferred_element_type=F32)
        acc_refs[hh][qi] = alpha * acc_refs[hh][qi] + pv

    hq = _VT_TILE
    assert tq == 2 * hq

    def causal(block):
        kpos = lax.broadcasted_iota(jnp.int32, block.shape, 0)
        qpos = lax.broadcasted_iota(jnp.int32, block.shape, 1)
        return jnp.where(kpos <= qpos, block, NEG)

    def scores_diag(hh, qi, slot):
        base = qi * tq
        qt = qt_ref[0, hh, qi]
        top = jnp.dot(k_ref[0, hh, base:base + hq, :], qt, preferred_element_type=F32)
        top = jnp.concatenate([causal(top[:, 0:hq]), top[:, hq:tq]], axis=1)
        bot = causal(jnp.dot(k_ref[0, hh, base + hq:base + tq, :], qt[:, hq:tq],
                             preferred_element_type=F32))
        st_ref[slot, hh, 0:hq, :] = top
        st_ref[slot, hh, hq:tq, hq:tq] = bot
        mtop = jnp.max(top, axis=0, keepdims=True)
        mbot = jnp.max(bot, axis=0, keepdims=True)
        mx_ref[slot, hh] = jnp.concatenate([mtop[:, 0:hq], jnp.maximum(mtop[:, hq:tq], mbot)], axis=1)

    def update_diag(hh, qi, slot):
        m_new = mx_ref[slot, hh]
        m_refs[hh][qi] = m_new
        p_top = jnp.exp2(st_ref[slot, hh, 0:hq, :] - m_new).astype(BF16)
        p_bot = jnp.exp2(st_ref[slot, hh, hq:tq, hq:tq] - m_new[:, hq:tq]).astype(BF16)
        pv = jnp.dot(vt_ref[0, hh, qi * nsub], p_top, preferred_element_type=F32)
        pv_r = jnp.dot(vt_ref[0, hh, qi * nsub + 1], p_bot, preferred_element_type=F32)
        acc = acc_refs[hh]
        acc[qi, :, 0:hq] = pv[:, 0:hq]
        acc[qi, :, hq:tq] = pv[:, hq:tq] + pv_r

    def stage(cur, nxt, slot, diag_cur, diag_next):
        for hh in range(_MLA_HP):
            if nxt is not None and diag_next:
                scores_diag(hh, nxt[0], 1 - slot)
            elif nxt is not None:
                scores(hh, nxt[0], nxt[1], 1 - slot)
            if diag_cur:
                update_diag(hh, cur[0], slot)
            else:
                update(hh, cur[0], cur[1], slot)

    off_diag, diag = _mla_tile_order(nq)
    assert len(diag) % 2 == 0 and _MLA_LOOP_STAGES % 2 == 0
    n_loop = (len(off_diag) - 1) // _MLA_LOOP_STAGES * _MLA_LOOP_STAGES

    def static_stages(tiles, g0, then):
        for i, cur in enumerate(tiles):
            nxt = tiles[i + 1] if i + 1 < len(tiles) else then
            stage(cur, nxt, (g0 + i) % 2, cur[0] == cur[1], nxt is not None and nxt[0] == nxt[1])

    for hh in range(_MLA_HP):
        scores_diag(hh, diag[0][0], 0)
    static_stages(diag, 0, off_diag[0])

    def body(t, carry):
        for u in range(_MLA_LOOP_STAGES):
            n = t * _MLA_LOOP_STAGES + u
            cur = (order_ref[0, n], order_ref[1, n])
            nxt = (order_ref[0, n + 1], order_ref[1, n + 1])
            stage(cur, nxt, u % 2, False, False)
        return carry

    lax.fori_loop(0, n_loop // _MLA_LOOP_STAGES, body, 0)
    static_stages(off_diag[n_loop:], len(diag) + n_loop, None)

    for qi in range(nq):
        for hh in range(_MLA_HP):
            out = acc_refs[hh][qi, 0:MLA_V_DIM] / acc_refs[hh][qi, MLA_V_DIM:MLA_V_DIM + 1]
            rows, cols = slice(qi * tq, (qi + 1) * tq), slice(hh * MLA_V_DIM, (hh + 1) * MLA_V_DIM)
            o_ref[0, rows, cols] = out.T.astype(BF16) * gate_ref[0, rows, cols]


def _mla_attn(qt, k, vt, gate):
    b, h, nq, _, tq = qt.shape
    s = nq * tq
    hp = _MLA_HP
    off_diag, _ = _mla_tile_order(nq)
    order = jnp.asarray(list(zip(*off_diag)), jnp.int32)
    return pl.pallas_call(
        _mla_attn_kernel,
        grid=(b, h // hp),
        in_specs=[
            pl.BlockSpec(memory_space=pltpu.SMEM),
            pl.BlockSpec((1, hp, nq, MLA_QK_PAD, tq), lambda bi, hi: (bi, hi, 0, 0, 0)),
            pl.BlockSpec((1, hp, s, MLA_QK_PAD), lambda bi, hi: (bi, hi, 0, 0)),
            pl.BlockSpec((1, hp, s // _VT_TILE, _VT_AUG, _VT_TILE), lambda bi, hi: (bi, hi, 0, 0, 0)),
            pl.BlockSpec((1, s, hp * MLA_V_DIM), lambda bi, hi: (bi, 0, SWA_WIDTH // (hp * MLA_V_DIM) + hi)),
        ],
        out_specs=pl.BlockSpec((1, s, hp * MLA_V_DIM), lambda bi, hi: (bi, 0, hi)),
        out_shape=jax.ShapeDtypeStruct((b, s, MLA_WIDTH), BF16),
        scratch_shapes=(
            [pltpu.VMEM((nq, 1, tq), F32) for _ in range(hp)]
            + [pltpu.VMEM((nq, _VT_AUG, tq), F32) for _ in range(hp)]
            + [pltpu.VMEM((2, hp, tq, tq), F32), pltpu.VMEM((2, hp, 1, tq), F32)]
        ),
        compiler_params=_cparams("parallel", "parallel"),
        name="mla_attn",
    )(order, qt, k, vt, gate)


_SWA_TQ = 2048
_SWA_NB = _SWA_TQ // BLOCK
_SWA_GW = SWA_GROUP * BLOCK


def _swa_kernel(sink_ref, bias_ref, pmask_ref, qt_ref, kc_ref, kp_ref, vc_ref, vp_ref, gate_ref,
                o_ref, kbuf_ref, vbuf_ref):
    tq = kc_ref.shape[1]
    kbuf_ref[0:BLOCK] = kp_ref[0]
    kbuf_ref[BLOCK:BLOCK + tq] = kc_ref[0]
    vbuf_ref[:, 0:BLOCK] = vp_ref[...]
    vbuf_ref[:, BLOCK:BLOCK + tq] = vc_ref[...]
    first_variant = jnp.where(pl.program_id(1) == 0, 1, 0)
    half = SWA_HEAD_DIM
    zeros = jnp.zeros((half, BLOCK), BF16)
    ones = jnp.ones((16, 2 * BLOCK), BF16)
    r_io =lax.broadcasted_iota(jnp.int32, (BLOCK, _SWA_GW), 0)
    t_io = lax.broadcasted_iota(jnp.int32, (BLOCK, _SWA_GW), 1) & (BLOCK - 1)
    prev_visible = t_io < r_io

    def scores(blk, hk):
        r0 = blk * BLOCK
        kwin = kbuf_ref[r0:r0 + 2 * BLOCK, hk * BLOCK:(hk + 1) * BLOCK]
        cols = []
        for p in range(SWA_GROUP // 2):
            row = hk * (SWA_GROUP * half) + p * BLOCK
            qt = qt_ref[row:row + BLOCK, r0:r0 + BLOCK]
            cols.append(jnp.concatenate([qt[0:half], zeros], axis=0))
            cols.append(jnp.concatenate([zeros, qt[half:BLOCK]], axis=0))
        qz = jnp.concatenate(cols, axis=1)
        st = jnp.dot(kwin, qz, preferred_element_type=F32)
        variant = first_variant if blk == 0 else 0
        return jnp.where(prev_visible, st[0:BLOCK], st[BLOCK:2 * BLOCK]) + bias_ref[variant, hk]

    def finish(blk, hk, s2):
        r0 = blk * BLOCK
        sink = sink_ref[hk]
        m = jnp.maximum(jnp.max(s2, axis=0, keepdims=True), sink)
        pb = jnp.exp2(s2 - m).astype(BF16)
        p_prev = pb * pmask_ref[...]
        pstack = jnp.concatenate([p_prev, pb - p_prev], axis=0)
        vwin = jnp.concatenate([vbuf_ref[hk * half:(hk + 1) * half, r0:r0 + 2 * BLOCK], ones], axis=0)
        ot = jnp.dot(vwin, pstack, preferred_element_type=F32)
        l = ot[half:half + 1] + jnp.exp2(sink - m)
        ot = ot[0:half] * (1.0 / l)
        for pr in range(SWA_GROUP // 2):
            c0 = 2 * pr * BLOCK
            pair_t = jnp.concatenate([ot[:, c0:c0 + BLOCK], ot[:, c0 + BLOCK:c0 + 2 * BLOCK]], axis=0)
            col = hk * (SWA_GROUP * half) + pr * BLOCK
            gated = pair_t.T.astype(BF16) * gate_ref[r0:r0 + BLOCK, col:col + BLOCK]
            o_ref[0, r0:r0 + BLOCK, col:col + BLOCK] = gated

    chains = [(blk, hk) for blk in range(tq // BLOCK) for hk in range(SWA_KV_HEADS)]
    s_next = scores(*chains[0])
    for c, chain in enumerate(chains):
        s_cur = s_next
        if c + 1 < len(chains):
            s_next = scores(*chains[c + 1])
        finish(*chain, s_cur)


def _swa_attn(sink_rows, bias, pmask, qat, kk, vt, gate, b, s):
    tq = _SWA_TQ
    nq = s // tq
    nb = _SWA_NB

    def prev_block(i):
        return jnp.maximum(i * nb - 1, 0)

    return pl.pallas_call(
        _swa_kernel,
        grid=(b, nq),
        in_specs=[
            _resident(sink_rows.shape),
            _resident(bias.shape),
            _resident(pmask.shape),
            pl.BlockSpec((A_Q, tq), lambda bi, i: (0, bi * nq + i)),
            pl.BlockSpec((1, tq, KK_W), lambda bi, i: (bi, i, 0)),
            pl.BlockSpec((1, BLOCK, KK_W), lambda bi, i: (bi, prev_block(i), 0)),
            pl.BlockSpec((VT_ROWS, tq), lambda bi, i: (0, bi * nq + i)),
            pl.BlockSpec((VT_ROWS, BLOCK), lambda bi, i: (0, bi * nq * nb + prev_block(i))),
            pl.BlockSpec((tq, SWA_WIDTH), lambda bi, i: (bi * nq + i, 0)),
        ],
        out_specs=pl.BlockSpec((1, tq, SWA_WIDTH), lambda bi, i: (bi, i, 0)),
        out_shape=jax.ShapeDtypeStruct((b, s, SWA_WIDTH), BF16),
        scratch_shapes=[
            pltpu.VMEM((tq + BLOCK, KK_W), BF16),
            pltpu.VMEM((VT_ROWS, tq + BLOCK), BF16),
        ],
        compiler_params=_cparams("parallel", "parallel"),
        name="swa_attn",
    )(sink_rows, bias, pmask, qat, kk.reshape(b, s, KK_W), kk.reshape(b, s, KK_W), vt, vt, gate)


def _swa_tables(swa_sinks):
    assert WINDOW == BLOCK
    row = jnp.arange(BLOCK, dtype=jnp.int32)[:, None]
    tcol = jnp.arange(BLOCK, dtype=jnp.int32)[None, :]
    delta = jnp.where(tcol < row, BLOCK + tcol - row, tcol - row)
    slopes = jnp.exp2(-8.0 * jnp.arange(1, SWA_Q_HEADS + 1, dtype=F32) / SWA_Q_HEADS)
    slopes = slopes.reshape(SWA_KV_HEADS, SWA_GROUP)
    has_prev = -(slopes * LOG2E)[:, :, None, None] * delta.astype(F32)[None, None]
    no_prev = jnp.where((tcol < row)[None, None], NEG, has_prev)
    bias = jnp.stack([has_prev, no_prev])
    bias = bias.transpose(0, 1, 3, 2, 4).reshape(2, SWA_KV_HEADS, BLOCK, _SWA_GW)
    pmask = jnp.tile((tcol < row).astype(BF16), (1, SWA_GROUP))
    sinks2 = swa_sinks.astype(F32) * LOG2E
    sink_rows = jnp.repeat(sinks2.reshape(DEPTH, SWA_KV_HEADS, 1, SWA_GROUP), BLOCK, axis=-1)
    return bias, pmask, sink_rows


_OUTPROJ_TM = 1024
_OUTPROJ_TN = 512


def _outproj_kernel(*refs, final):
    if final:
        ya_ref, yb_ref, x_ref, w_ref, gfin_ref, o_ref = refs
    else:
        ya_ref, yb_ref, x_ref, w_ref, o_ref = refs
    gated = jnp.concatenate([ya_ref[...], yb_ref[...]], axis=-1)
    for c in range(D_MODEL // _OUTPROJ_TN):
        sl = slice(c * _OUTPROJ_TN, (c + 1) * _OUTPROJ_TN)
        o_ref[:, sl] = x_ref[:, sl] + jnp.dot(gated, w_ref[:, sl], preferred_element_type=F32)
    if final:
        o_ref[...] = _rms(o_ref[...], gfin_ref[...])


def _outproj(ya, yb, x2d, w, l, final_g=None):
    t = x2d.shape[0]
    tm = _OUTPROJ_TM
    final = final_g is not None
    in_specs = [
        pl.BlockSpec((tm, SWA_WIDTH), lambda i: (i, 0)),
        pl.BlockSpec((tm, MLA_WIDTH), lambda i: (i, 0)),
        pl.BlockSpec((tm, D_MODEL), lambda i: (i, 0)),
        _layer(w, l),
    ]
    args = [ya, yb, x2d, w]
    if final:
        in_specs.append(_resident(final_g.shape))
        args.append(final_g)
    return pl.pallas_call(
        functools.partial(_outproj_kernel, final=final),
        grid=(t // tm,),
        in_specs=in_specs,
        out_specs=pl.BlockSpec((tm, D_MODEL), lambda i: (i, 0)),
        out_shape=jax.ShapeDtypeStruct((t, D_MODEL), F32),
        compiler_params=_cparams("parallel"),
        name="outproj_final" if final else "outproj",
    )(*args)


_PREP_TR = 256


def _prep_w_in_kernel(w_ref, row_ref, col_ref):
    def rows(lo, n):
        return w_ref[0, lo:lo + n, :]

    o = 0
    qa = rows(o, A_Q); o += A_Q
    ka = rows(o, KA_W); o += KA_W
    va = rows(o, VT_ROWS); o += VT_ROWS
    ga = rows(o, 1024); o += 1024
    cq = rows(o, Q_LORA_RANK); o += Q_LORA_RANK
    ckv = rows(o, KV_LORA_RANK); o += KV_LORA_RANK
    half = MLA_ROPE_DIM // 2
    kr = rows(o, MLA_ROPE_DIM)
    kr_sw = jnp.concatenate([rows(o + half, half), rows(o, half)], axis=0)
    o += MLA_ROPE_DIM
    gb = rows(o, 1024)
    col_ref[0, 0:A_Q, :] = qa.astype(BF16)
    col_ref[0, A_Q:COL_W, :] = va.astype(BF16)
    off = 0
    for piece in (ga, gb, cq, ckv, jnp.concatenate([kr, kr_sw, ka], axis=0)):
        n = piece.shape[0]
        row_ref[0, :, off:off + n] = piece.T.astype(BF16)
        off += n


def _prep_w_in(w_in):
    tr = _PREP_TR
    w_t = jnp.swapaxes(w_in, 1, 2)
    return pl.pallas_call(
        _prep_w_in_kernel,
        grid=(DEPTH, D_MODEL // tr),
        in_specs=[pl.BlockSpec((1, IN_WIDTH, tr), lambda l, i: (l, 0, i))],
        out_specs=[
            pl.BlockSpec((1, tr, ROW_W), lambda l, i: (l, i, 0)),
            pl.BlockSpec((1, COL_W, tr), lambda l, i: (l, 0, i)),
        ],
        out_shape=[
            jax.ShapeDtypeStruct((DEPTH, D_MODEL, ROW_W), BF16),
            jax.ShapeDtypeStruct((DEPTH, COL_W, D_MODEL), BF16),
        ],
        compiler_params=_cparams("parallel", "parallel"),
        name="prep_w_in",
    )(w_t)


def _prep_w_q(w_q_b):
    w = w_q_b.reshape(DEPTH, Q_LORA_RANK, MLA_HEADS, MLA_QK_DIM)
    nope = w[..., :MLA_NOPE_DIM].reshape(DEPTH, Q_LORA_RANK, MLA_HEADS * MLA_NOPE_DIM)
    rope = w[..., MLA_NOPE_DIM:].reshape(DEPTH, Q_LORA_RANK, MLA_HEADS * MLA_ROPE_DIM)
    return jnp.swapaxes(jnp.concatenate([nope, rope], axis=-1), 1, 2).astype(BF16)


def _prep_w_kv(w_kv_b):
    w = w_kv_b.reshape(DEPTH, KV_LORA_RANK, MLA_HEADS, MLA_NOPE_DIM + MLA_V_DIM)
    wk = w[..., :MLA_NOPE_DIM].reshape(DEPTH, KV_LORA_RANK, MLA_HEADS * MLA_NOPE_DIM)
    wv = w[..., MLA_NOPE_DIM:].reshape(DEPTH, KV_LORA_RANK, MLA_HEADS * MLA_V_DIM)
    return wk.astype(BF16), jnp.swapaxes(wv, 1, 2).astype(BF16)


def _rope_tables(s):
    pos = jnp.arange(s, dtype=F32)
    inv_freq = ROPE_THETA ** (-jnp.arange(0, MLA_ROPE_DIM, 2, dtype=F32) / MLA_ROPE_DIM)
    ang = pos[:, None] * inv_freq[None, :]
    cos, sin = jnp.cos(ang), jnp.sin(ang)
    cc = jnp.concatenate([cos, cos], axis=-1)
    ss = jnp.concatenate([-sin, sin], axis=-1)
    return jnp.tile(cc, (1, 2)), jnp.tile(ss, (1, 2)), cc.T, ss.T


def kernel(x, attn_norm_g, w_in, swa_sinks, q_a_norm_g, kv_a_norm_g, w_q_b, w_kv_b, w_out, final_norm_g):
    b, s, d = x.shape
    t = b * s
    w_row, w_col = _prep_w_in(w_in)
    w_q_p = _prep_w_q(w_q_b)
    w_k_p, w_vt_p = _prep_w_kv(w_kv_b)
    w_out_p = w_out.astype(BF16)
    cos2, sin2, cost, sint = _rope_tables(s)
    swa_bias, swa_pmask, sink_rows = _swa_tables(swa_sinks)
    g_attn = attn_norm_g[:, None, :]
    g_q = q_a_norm_g[:, None, :]
    g_kv = kv_a_norm_g[:, None, :]

    x2d = x.reshape(t, d)
    for l in range(DEPTH):
        gate, kk, qat, vat, qt, k, vt = _inproj(x2d, g_attn, w_row, w_col, g_q, g_kv, w_q_p, w_k_p, w_vt_p,
                                                cos2, sin2, cost, sint, l, b, s)
        ya = _swa_attn(sink_rows[l], swa_bias, swa_pmask, qat, kk, vat, gate, b, s)
        yb = _mla_attn(qt, k, vt, gate.reshape(b, s, GATE_W))
        final_g = final_norm_g[None] if l == DEPTH - 1 else None
        x2d = _outproj(ya.reshape(t, SWA_WIDTH), yb.reshape(t, MLA_WIDTH), x2d, w_out_p, l, final_g)
    return x2d.reshape(b, s, d)
```

```python
import functools
import math

import jax
import jax.numpy as jnp
from jax import lax
from jax.experimental import pallas as pl
from jax.experimental.pallas import tpu as pltpu

F32 = jnp.float32
BF16 = jnp.bfloat16

D_MODEL = 2048
DEPTH = 4
EPS = 1e-6
BLOCK = 128
WINDOW = 128
NEG = -1e30
LOG2E = math.log2(math.e)

SWA_WIDTH = 1024
SWA_HEAD_DIM = 64
SWA_Q_HEADS = 16
SWA_KV_HEADS = 2
SWA_GROUP = 8

MLA_WIDTH = 1024
MLA_V_DIM = 128
MLA_HEADS = 8
MLA_NOPE_DIM = 128
MLA_ROPE_DIM = 64
MLA_QK_DIM = 192
MLA_QK_PAD = 256
Q_LORA_RANK = 384
KV_LORA_RANK = 256
ROPE_THETA = 10000.0

A_Q = 1024
GATE_W = 2048
IN_WIDTH = 4032
LAT_W = Q_LORA_RANK + KV_LORA_RANK + 2 * MLA_ROPE_DIM
KA_W = SWA_KV_HEADS * SWA_HEAD_DIM
KK_W = 2 * KA_W
VT_ROWS = SWA_KV_HEADS * SWA_HEAD_DIM
ROW_W = GATE_W + LAT_W + KA_W
COL_W = A_Q + VT_ROWS

SWA_QSCALE = SWA_HEAD_DIM ** -0.5 * LOG2E
MLA_QSCALE = MLA_QK_DIM ** -0.5 * LOG2E

VMEM_LIMIT = 56 * 1024 * 1024
NT_DIMS = (((1,), (1,)), ((), ()))


def _cparams(*sem):
    return pltpu.CompilerParams(dimension_semantics=sem, vmem_limit_bytes=VMEM_LIMIT)


def _resident(shape):
    nd = len(shape)
    return pl.BlockSpec(shape, lambda *_: (0,) * nd, pipeline_mode=pl.Buffered(1))


def _layer(stacked, l):
    tail = stacked.shape[1:]
    return pl.BlockSpec((None,) + tail, lambda *_: (l,) + (0,) * len(tail),
                        pipeline_mode=pl.Buffered(1))


def _rms(x, g):
    return x * lax.rsqrt(jnp.mean(x * x, axis=-1, keepdims=True) + EPS) * g


_INPROJ_TM = 512
_INPROJ_TN = 512
_VT_TILE = 256
_VT_AUG = MLA_V_DIM + 16
_LAT0 = GATE_W


def _inproj_kernel(x_ref, g_ref, wr_ref, wc_ref, gq_ref, gkv_ref, wqt_ref, wk_ref, wvt_ref,
                   cos_ref, sin_ref, cost_ref, sint_ref,
                   gate_ref, kk_ref, qat_ref, vat_ref, qt_ref, k_ref, vt_ref):
    tm = x_ref.shape[0]
    x = x_ref[...]
    h = (x * g_ref[...]).astype(BF16)
    r = lax.rsqrt(jnp.mean(x * x, axis=-1, keepdims=True) + EPS)
    r_row = jnp.broadcast_to(r, (tm, 128)).T[0:1]

    lat_a = r * jnp.dot(h, wr_ref[:, _LAT0:_LAT0 + 512], preferred_element_type=F32)
    lat_b = r * jnp.dot(h, wr_ref[:, _LAT0 + 512:ROW_W], preferred_element_type=F32)
    cq = lat_a[:, 0:Q_LORA_RANK]
    ckv = jnp.concatenate([lat_a[:, Q_LORA_RANK:512], lat_b[:, 0:128]], axis=1)
    kslab = lat_b[:, 128:256]
    ka = lat_b[:, 256:384]
    ka_sw = pltpu.roll(ka, SWA_HEAD_DIM, 1)
    low = lax.broadcasted_iota(jnp.int32, ka.shape, 1) < SWA_HEAD_DIM
    kk_ref[:, 0:KA_W] = jnp.where(low, ka, ka_sw).astype(BF16)
    kk_ref[:, KA_W:KK_W] = jnp.where(low, ka_sw, ka).astype(BF16)

    for c in range(GATE_W // _INPROJ_TN):
        cols = slice(c * _INPROJ_TN, (c + 1) * _INPROJ_TN)
        a = r * jnp.dot(h, wr_ref[:, cols], preferred_element_type=F32)
        gate_ref[:, cols] = (a * (1.0 / (1.0 + jnp.exp(-a)))).astype(BF16)
    q_scale = r_row * SWA_QSCALE
    for c in range(A_Q // _INPROJ_TN):
        rows = slice(c * _INPROJ_TN, (c + 1) * _INPROJ_TN)
        a = lax.dot_general(wc_ref[rows, :], h, NT_DIMS, preferred_element_type=F32)
        qat_ref[rows, :] = (a * q_scale).astype(BF16)
    vat_ref[...] = (r_row * lax.dot_general(wc_ref[A_Q:COL_W, :], h, NT_DIMS,
                                            preferred_element_type=F32)).astype(BF16)

    cqn = _rms(cq, gq_ref[...]).astype(BF16)
    ckvn = _rms(ckv, gkv_ref[...]).astype(BF16)
    krot = kslab * cos_ref[...] + pltpu.roll(kslab, MLA_ROPE_DIM, 1) * sin_ref[...]
    krot = jnp.where(low, krot, 0.0).astype(BF16)
    knope = jnp.dot(ckvn, wk_ref[...], preferred_element_type=F32)
    for hd in range(MLA_HEADS):
        k_ref[0, hd, :, 0:128] = knope[:, hd * 128:(hd + 1) * 128].astype(BF16)
        k_ref[0, hd, :, 128:256] = krot

    vt = lax.dot_general(wvt_ref[...], ckvn, NT_DIMS, preferred_element_type=F32)
    ones = jnp.ones((_VT_AUG - MLA_V_DIM, _VT_TILE), BF16)
    for hd in range(MLA_HEADS):
        for c in range(tm // _VT_TILE):
            vt_ref[0, hd, c, 0:MLA_V_DIM] = vt[hd * 128:(hd + 1) * 128, c * _VT_TILE:(c + 1) * _VT_TILE].astype(BF16)
            vt_ref[0, hd, c, MLA_V_DIM:_VT_AUG] = ones

    qt = lax.dot_general(wqt_ref[...], cqn, NT_DIMS, preferred_element_type=F32)
    cost = cost_ref[...]
    sint = sint_ref[...]
    pad = jnp.zeros((MLA_QK_PAD - MLA_QK_DIM, tm), BF16)
    hr = MLA_ROPE_DIM // 2
    for hd in range(MLA_HEADS):
        r0 = 1024 + hd * MLA_ROPE_DIM
        rope = qt[r0:r0 + MLA_ROPE_DIM]
        rope_sw = jnp.concatenate([rope[hr:], rope[:hr]], axis=0)
        qr = (rope * cost + rope_sw * sint) * MLA_QSCALE
        qt_ref[0, hd, 0, 0:128, :] = (qt[hd * 128:(hd + 1) * 128] * MLA_QSCALE).astype(BF16)
        qt_ref[0, hd, 0, 128:192, :] = qr.astype(BF16)
        qt_ref[0, hd, 0, 192:256, :] = pad


def _inproj(x2d, g, w_row, w_col, gq, gkv, wqt, wk, wvt, cos2, sin2, cost, sint, l, b, s):
    t = x2d.shape[0]
    tm = _INPROJ_TM
    nt = s // tm
    nvt = tm // _VT_TILE
    return pl.pallas_call(
        _inproj_kernel,
        grid=(t // tm,),
        in_specs=[
            pl.BlockSpec((tm, D_MODEL), lambda i: (i, 0)),
            _layer(g, l),
            _layer(w_row, l),
            _layer(w_col, l),
            _layer(gq, l),
            _layer(gkv, l),
            _layer(wqt, l),
            _layer(wk, l),
            _layer(wvt, l),
            pl.BlockSpec((tm, 128), lambda i: (i % nt, 0)),
            pl.BlockSpec((tm, 128), lambda i: (i % nt, 0)),
            pl.BlockSpec((MLA_ROPE_DIM, tm), lambda i: (0, i % nt)),
            pl.BlockSpec((MLA_ROPE_DIM, tm), lambda i: (0, i % nt)),
        ],
        out_specs=[
            pl.BlockSpec((tm, GATE_W), lambda i: (i, 0)),
            pl.BlockSpec((tm, KK_W), lambda i: (i, 0)),
            pl.BlockSpec((A_Q, tm), lambda i: (0, i)),
            pl.BlockSpec((VT_ROWS, tm), lambda i: (0, i)),
            pl.BlockSpec((1, MLA_HEADS, 1, MLA_QK_PAD, tm), lambda i: (i // nt, 0, i % nt, 0, 0)),
            pl.BlockSpec((1, MLA_HEADS, tm, MLA_QK_PAD), lambda i: (i // nt, 0, i % nt, 0)),
            pl.BlockSpec((1, MLA_HEADS, nvt, _VT_AUG, _VT_TILE), lambda i: (i // nt, 0, i % nt, 0, 0)),
        ],
        out_shape=[
            jax.ShapeDtypeStruct((t, GATE_W), BF16),
            jax.ShapeDtypeStruct((t, KK_W), BF16),
            jax.ShapeDtypeStruct((A_Q, t), BF16),
            jax.ShapeDtypeStruct((VT_ROWS, t), BF16),
            jax.ShapeDtypeStruct((b, MLA_HEADS, nt, MLA_QK_PAD, tm), BF16),
            jax.ShapeDtypeStruct((b, MLA_HEADS, s, MLA_QK_PAD), BF16),
            jax.ShapeDtypeStruct((b, MLA_HEADS, s // _VT_TILE, _VT_AUG, _VT_TILE), BF16),
        ],
        compiler_params=_cparams("parallel"),
        name="inproj",
    )(x2d, g, w_row, w_col, gq, gkv, wqt, wk, wvt, cos2, sin2, cost, sint)


_MLA_TQ = 512
_MLA_HP = 2
_MLA_LOOP_STAGES = 12


def _mla_tile_order(nq):
    off_diag = [(qi, j) for qi in range(nq) for j in range(qi)]
    return off_diag, [(qi, qi) for qi in range(nq)]


def _mla_attn_kernel(order_ref, qt_ref, k_ref, vt_ref, gate_ref, o_ref, *scratch):
    m_refs = scratch[0:_MLA_HP]
    acc_refs = scratch[_MLA_HP:2 * _MLA_HP]
    st_ref, mx_ref = scratch[2 * _MLA_HP:]
    nq, tq = qt_ref.shape[2], qt_ref.shape[4]
    nsub = tq // _VT_TILE

    def scores(hh, qi, j, slot):
        kt = k_ref[0, hh, pl.ds(pl.multiple_of(j * tq, tq), tq), :]
        st = jnp.dot(kt, qt_ref[0, hh, qi], preferred_element_type=F32)
        st_ref[slot, hh] = st
        mx_ref[slot, hh] = jnp.max(st, axis=0, keepdims=True)

    def update(hh, qi, j, slot):
        m_old = m_refs[hh][qi]
        m_new = jnp.maximum(m_old, mx_ref[slot, hh])
        alpha = jnp.exp2(m_old - m_new)
        pb = jnp.exp2(st_ref[slot, hh] - m_new).astype(BF16)
        m_refs[hh][qi] = m_new
        pv = jnp.dot(vt_ref[0, hh, j * nsub], pb[0:_VT_TILE], preferred_element_type=F32)
        for c in range(1, nsub):
            pv += jnp.dot(vt_ref[0, hh, j * nsub + c], pb[c * _VT_TILE:(c + 1) * _VT_TILE],
                          preferred_element_type=F32)
        acc_refs[hh][qi] = alpha * acc_refs[hh][qi] + pv

    hq = _VT_TILE
    assert tq == 2 * hq

    def causal(block):
        kpos = lax.broadcasted_iota(jnp.int32, block.shape, 0)
        qpos = lax.broadcasted_iota(jnp.int32, block.shape, 1)
        return jnp.where(kpos <= qpos, block, NEG)

    def scores_diag(hh, qi, slot):
        base = qi * tq
        qt = qt_ref[0, hh, qi]
        top = jnp.dot(k_ref[0, hh, base:base + hq, :], qt, preferred_element_type=F32)
        top = jnp.concatenate([causal(top[:, 0:hq]), top[:, hq:tq]], axis=1)
        bot = causal(jnp.dot(k_ref[0, hh, base + hq:base + tq, :], qt[:, hq:tq],
                             preferred_element_type=F32))
        st_ref[slot, hh, 0:hq, :] = top
        st_ref[slot, hh, hq:tq, hq:tq] = bot
        mtop = jnp.max(top, axis=0, keepdims=True)
        mbot = jnp.max(bot, axis=0, keepdims=True)
        mx_ref[slot, hh] = jnp.concatenate([mtop[:, 0:hq], jnp.maximum(mtop[:, hq:tq], mbot)], axis=1)

    def update_diag(hh, qi, slot):
        m_new = mx_ref[slot, hh]
        m_refs[hh][qi] = m_new
        p_top = jnp.exp2(st_ref[slot, hh, 0:hq, :] - m_new).astype(BF16)
        p_bot = jnp.exp2(st_ref[slot, hh, hq:tq, hq:tq] - m_new[:, hq:tq]).astype(BF16)
        pv = jnp.dot(vt_ref[0, hh, qi * nsub], p_top, preferred_element_type=F32)
        pv_r = jnp.dot(vt_ref[0, hh, qi * nsub + 1], p_bot, preferred_element_type=F32)
        acc = acc_refs[hh]
        acc[qi, :, 0:hq] = pv[:, 0:hq]
        acc[qi, :, hq:tq] = pv[:, hq:tq] + pv_r

    def stage(cur, nxt, slot, diag_cur, diag_next):
        for hh in range(_MLA_HP):
            if nxt is not None and diag_next:
                scores_diag(hh, nxt[0], 1 - slot)
            elif nxt is not None:
                scores(hh, nxt[0], nxt[1], 1 - slot)
            if diag_cur:
                update_diag(hh, cur[0], slot)
            else:
                update(hh, cur[0], cur[1], slot)

    off_diag, diag = _mla_tile_order(nq)
    assert len(diag) % 2 == 0 and _MLA_LOOP_STAGES % 2 == 0
    n_loop = (len(off_diag) - 1) // _MLA_LOOP_STAGES * _MLA_LOOP_STAGES

    def static_stages(tiles, g0, then):
        for i, cur in enumerate(tiles):
            nxt = tiles[i + 1] if i + 1 < len(tiles) else then
            stage(cur, nxt, (g0 + i) % 2, cur[0] == cur[1], nxt is not None and nxt[0] == nxt[1])

    for hh in range(_MLA_HP):
        scores_diag(hh, diag[0][0], 0)
    static_stages(diag, 0, off_diag[0])

    def body(t, carry):
        for u in range(_MLA_LOOP_STAGES):
            n = t * _MLA_LOOP_STAGES + u
            cur = (order_ref[0, n], order_ref[1, n])
            nxt = (order_ref[0, n + 1], order_ref[1, n + 1])
            stage(cur, nxt, u % 2, False, False)
        return carry

    lax.fori_loop(0, n_loop // _MLA_LOOP_STAGES, body, 0)
    static_stages(off_diag[n_loop:], len(diag) + n_loop, None)

    for qi in range(nq):
        for hh in range(_MLA_HP):
            out = acc_refs[hh][qi, 0:MLA_V_DIM] / acc_refs[hh][qi, MLA_V_DIM:MLA_V_DIM + 1]
            rows, cols = slice(qi * tq, (qi + 1) * tq), slice(hh * MLA_V_DIM, (hh + 1) * MLA_V_DIM)
            o_ref[0, rows, cols] = out.T.astype(BF16) * gate_ref[0, rows, cols]


def _mla_attn(qt, k, vt, gate):
    b, h, nq, _, tq = qt.shape
    s = nq * tq
    hp = _MLA_HP
    off_diag, _ = _mla_tile_order(nq)
    order = jnp.asarray(list(zip(*off_diag)), jnp.int32)
    return pl.pallas_call(
        _mla_attn_kernel,
        grid=(b, h // hp),
        in_specs=[
            pl.BlockSpec(memory_space=pltpu.SMEM),
            pl.BlockSpec((1, hp, nq, MLA_QK_PAD, tq), lambda bi, hi: (bi, hi, 0, 0, 0)),
            pl.BlockSpec((1, hp, s, MLA_QK_PAD), lambda bi, hi: (bi, hi, 0, 0)),
            pl.BlockSpec((1, hp, s // _VT_TILE, _VT_AUG, _VT_TILE), lambda bi, hi: (bi, hi, 0, 0, 0)),
            pl.BlockSpec((1, s, hp * MLA_V_DIM), lambda bi, hi: (bi, 0, SWA_WIDTH // (hp * MLA_V_DIM) + hi)),
        ],
        out_specs=pl.BlockSpec((1, s, hp * MLA_V_DIM), lambda bi, hi: (bi, 0, hi)),
        out_shape=jax.ShapeDtypeStruct((b, s, MLA_WIDTH), BF16),
        scratch_shapes=(
            [pltpu.VMEM((nq, 1, tq), F32) for _ in range(hp)]
            + [pltpu.VMEM((nq, _VT_AUG, tq), F32) for _ in range(hp)]
            + [pltpu.VMEM((2, hp, tq, tq), F32), pltpu.VMEM((2, hp, 1, tq), F32)]
        ),
        compiler_params=_cparams("parallel", "parallel"),
        name="mla_attn",
    )(order, qt, k, vt, gate)


_SWA_TQ = 2048
_SWA_NB = _SWA_TQ // BLOCK
_SWA_GW = SWA_GROUP * BLOCK


def _swa_kernel(sink_ref, bias_ref, pmask_ref, qt_ref, kc_ref, kp_ref, vc_ref, vp_ref, gate_ref,
                o_ref, kbuf_ref, vbuf_ref):
    tq = kc_ref.shape[1]
    kbuf_ref[0:BLOCK] = kp_ref[0]
    kbuf_ref[BLOCK:BLOCK + tq] = kc_ref[0]
    vbuf_ref[:, 0:BLOCK] = vp_ref[...]
    vbuf_ref[:, BLOCK:BLOCK + tq] = vc_ref[...]
    first_variant = jnp.where(pl.program_id(1) == 0, 1, 0)
    half = SWA_HEAD_DIM
    zeros = jnp.zeros((half, BLOCK), BF16)
    ones = jnp.ones((16, 2 * BLOCK), BF16)
    r_io =lax.broadcasted_iota(jnp.int32, (BLOCK, _SWA_GW), 0)
    t_io = lax.broadcasted_iota(jnp.int32, (BLOCK, _SWA_GW), 1) & (BLOCK - 1)
    prev_visible = t_io < r_io

    def scores(blk, hk):
        r0 = blk * BLOCK
        kwin = kbuf_ref[r0:r0 + 2 * BLOCK, hk * BLOCK:(hk + 1) * BLOCK]
        cols = []
        for p in range(SWA_GROUP // 2):
            row = hk * (SWA_GROUP * half) + p * BLOCK
            qt = qt_ref[row:row + BLOCK, r0:r0 + BLOCK]
            cols.append(jnp.concatenate([qt[0:half], zeros], axis=0))
            cols.append(jnp.concatenate([zeros, qt[half:BLOCK]], axis=0))
        qz = jnp.concatenate(cols, axis=1)
        st = jnp.dot(kwin, qz, preferred_element_type=F32)
        variant = first_variant if blk == 0 else 0
        return jnp.where(prev_visible, st[0:BLOCK], st[BLOCK:2 * BLOCK]) + bias_ref[variant, hk]

    def finish(blk, hk, s2):
        r0 = blk * BLOCK
        sink = sink_ref[hk]
        m = jnp.maximum(jnp.max(s2, axis=0, keepdims=True), sink)
        pb = jnp.exp2(s2 - m).astype(BF16)
        p_prev = pb * pmask_ref[...]
        pstack = jnp.concatenate([p_prev, pb - p_prev], axis=0)
        vwin = jnp.concatenate([vbuf_ref[hk * half:(hk + 1) * half, r0:r0 + 2 * BLOCK], ones], axis=0)
        ot = jnp.dot(vwin, pstack, preferred_element_type=F32)
        l = ot[half:half + 1] + jnp.exp2(sink - m)
        ot = ot[0:half] * (1.0 / l)
        for pr in range(SWA_GROUP // 2):
            c0 = 2 * pr * BLOCK
            pair_t = jnp.concatenate([ot[:, c0:c0 + BLOCK], ot[:, c0 + BLOCK:c0 + 2 * BLOCK]], axis=0)
            col = hk * (SWA_GROUP * half) + pr * BLOCK
            gated = pair_t.T.astype(BF16) * gate_ref[r0:r0 + BLOCK, col:col + BLOCK]
            o_ref[0, r0:r0 + BLOCK, col:col + BLOCK] = gated

    chains = [(blk, hk) for blk in range(tq // BLOCK) for hk in range(SWA_KV_HEADS)]
    s_next = scores(*chains[0])
    for c, chain in enumerate(chains):
        s_cur = s_next
        if c + 1 < len(chains):
            s_next = scores(*chains[c + 1])
        finish(*chain, s_cur)


def _swa_attn(sink_rows, bias, pmask, qat, kk, vt, gate, b, s):
    tq = _SWA_TQ
    nq = s // tq
    nb = _SWA_NB

    def prev_block(i):
        return jnp.maximum(i * nb - 1, 0)

    return pl.pallas_call(
        _swa_kernel,
        grid=(b, nq),
        in_specs=[
            _resident(sink_rows.shape),
            _resident(bias.shape),
            _resident(pmask.shape),
            pl.BlockSpec((A_Q, tq), lambda bi, i: (0, bi * nq + i)),
            pl.BlockSpec((1, tq, KK_W), lambda bi, i: (bi, i, 0)),
            pl.BlockSpec((1, BLOCK, KK_W), lambda bi, i: (bi, prev_block(i), 0)),
            pl.BlockSpec((VT_ROWS, tq), lambda bi, i: (0, bi * nq + i)),
            pl.BlockSpec((VT_ROWS, BLOCK), lambda bi, i: (0, bi * nq * nb + prev_block(i))),
            pl.BlockSpec((tq, SWA_WIDTH), lambda bi, i: (bi * nq + i, 0)),
        ],
        out_specs=pl.BlockSpec((1, tq, SWA_WIDTH), lambda bi, i: (bi, i, 0)),
        out_shape=jax.ShapeDtypeStruct((b, s, SWA_WIDTH), BF16),
        scratch_shapes=[
            pltpu.VMEM((tq + BLOCK, KK_W), BF16),
            pltpu.VMEM((VT_ROWS, tq + BLOCK), BF16),
        ],
        compiler_params=_cparams("parallel", "parallel"),
        name="swa_attn",
    )(sink_rows, bias, pmask, qat, kk.reshape(b, s, KK_W), kk.reshape(b, s, KK_W), vt, vt, gate)


def _swa_tables(swa_sinks):
    row = jnp.arange(BLOCK, dtype=jnp.int32)[:, None]
    tcol = jnp.arange(BLOCK, dtype=jnp.int32)[None, :]
    delta = jnp.where(tcol < row, BLOCK + tcol - row, tcol - row)
    slopes = jnp.exp2(-8.0 * jnp.arange(1, SWA_Q_HEADS + 1, dtype=F32) / SWA_Q_HEADS)
    slopes = slopes.reshape(SWA_KV_HEADS, SWA_GROUP)
    has_prev = -(slopes * LOG2E)[:, :, None, None] * delta.astype(F32)[None, None]
    no_prev = jnp.where((tcol < row)[None, None], NEG, has_prev)
    bias = jnp.stack([has_prev, no_prev])
    bias = bias.transpose(0, 1, 3, 2, 4).reshape(2, SWA_KV_HEADS, BLOCK, _SWA_GW)
    pmask = jnp.tile((tcol < row).astype(BF16), (1, SWA_GROUP))
    sinks2 = swa_sinks.astype(F32) * LOG2E
    sink_rows = jnp.repeat(sinks2.reshape(DEPTH, SWA_KV_HEADS, 1, SWA_GROUP), BLOCK, axis=-1)
    return bias, pmask, sink_rows


_OUTPROJ_TM = 512
_OUTPROJ_TN = 512


def _outproj_kernel(*refs, final, tm):
    if final:
        ya_hbm, yb_hbm, x_hbm, w_ref, gfin_ref, o_hbm = refs
    else:
        ya_hbm, yb_hbm, x_hbm, w_ref, o_hbm = refs

    def body(ya_ref, yb_ref, x_ref, o_ref):
        gated = jnp.concatenate([ya_ref[...], yb_ref[...]], axis=-1)
        for c in range(D_MODEL // _OUTPROJ_TN):
            sl = slice(c * _OUTPROJ_TN, (c + 1) * _OUTPROJ_TN)
            o_ref[:, sl] = x_ref[:, sl] + jnp.dot(gated, w_ref[:, sl], preferred_element_type=F32)
        if final:
            o_ref[...] = _rms(o_ref[...], gfin_ref[...])

    deep = pl.Buffered(3)
    pltpu.emit_pipeline(
        body,
        grid=(x_hbm.shape[0] // tm,),
        in_specs=[
            pl.BlockSpec((tm, SWA_WIDTH), lambda i: (i, 0), pipeline_mode=deep),
            pl.BlockSpec((tm, MLA_WIDTH), lambda i: (i, 0), pipeline_mode=deep),
            pl.BlockSpec((tm, D_MODEL), lambda i: (i, 0), pipeline_mode=deep),
        ],
        out_specs=[pl.BlockSpec((tm, D_MODEL), lambda i: (i, 0))],
    )(ya_hbm, yb_hbm, x_hbm, o_hbm)


def _outproj(ya, yb, x2d, w, l, final_g=None):
    t = x2d.shape[0]
    final = final_g is not None
    any_spec = pl.BlockSpec(memory_space=pl.ANY)
    in_specs = [any_spec, any_spec, any_spec, _layer(w, l)]
    args = [ya, yb, x2d, w]
    if final:
        in_specs.append(_resident(final_g.shape))
        args.append(final_g)
    return pl.pallas_call(
        functools.partial(_outproj_kernel, final=final, tm=_OUTPROJ_TM),
        grid=(1,),
        in_specs=in_specs,
        out_specs=any_spec,
        out_shape=jax.ShapeDtypeStruct((t, D_MODEL), F32),
        compiler_params=pltpu.CompilerParams(vmem_limit_bytes=VMEM_LIMIT),
        name="outproj_final" if final else "outproj",
    )(*args)


def _swap_halves(w):
    half = w.shape[-1] // 2
    return jnp.concatenate([w[..., half:], w[..., :half]], axis=-1)


_PREP_TR = 256


def _prep_w_in_kernel(w_ref, row_ref, col_ref):
    def rows(lo, n):
        return w_ref[0, lo:lo + n, :]

    o = 0
    qa = rows(o, A_Q); o += A_Q
    ka = rows(o, KA_W); o += KA_W
    va = rows(o, VT_ROWS); o += VT_ROWS
    ga = rows(o, 1024); o += 1024
    cq = rows(o, Q_LORA_RANK); o += Q_LORA_RANK
    ckv = rows(o, KV_LORA_RANK); o += KV_LORA_RANK
    half = MLA_ROPE_DIM // 2
    kr = rows(o, MLA_ROPE_DIM)
    kr_sw = jnp.concatenate([rows(o + half, half), rows(o, half)], axis=0)
    o += MLA_ROPE_DIM
    gb = rows(o, 1024)
    col_ref[0, 0:A_Q, :] = qa.astype(BF16)
    col_ref[0, A_Q:COL_W, :] = va.astype(BF16)
    off = 0
    for piece in (ga, gb, cq, ckv, jnp.concatenate([kr, kr_sw, ka], axis=0)):
        n = piece.shape[0]
        row_ref[0, :, off:off + n] = piece.T.astype(BF16)
        off += n


def _prep_w_in(w_in):
    tr = _PREP_TR
    w_t = jnp.swapaxes(w_in, 1, 2)
    return pl.pallas_call(
        _prep_w_in_kernel,
        grid=(DEPTH, D_MODEL // tr),
        in_specs=[pl.BlockSpec((1, IN_WIDTH, tr), lambda l, i: (l, 0, i))],
        out_specs=[
            pl.BlockSpec((1, tr, ROW_W), lambda l, i: (l, i, 0)),
            pl.BlockSpec((1, COL_W, tr), lambda l, i: (l, 0, i)),
        ],
        out_shape=[
            jax.ShapeDtypeStruct((DEPTH, D_MODEL, ROW_W), BF16),
            jax.ShapeDtypeStruct((DEPTH, COL_W, D_MODEL), BF16),
        ],
        compiler_params=_cparams("parallel", "parallel"),
        name="prep_w_in",
    )(w_t)


def _prep_w_q(w_q_b):
    w = w_q_b.reshape(DEPTH, Q_LORA_RANK, MLA_HEADS, MLA_QK_DIM)
    nope = w[..., :MLA_NOPE_DIM].reshape(DEPTH, Q_LORA_RANK, MLA_HEADS * MLA_NOPE_DIM)
    rope = w[..., MLA_NOPE_DIM:].reshape(DEPTH, Q_LORA_RANK, MLA_HEADS * MLA_ROPE_DIM)
    return jnp.swapaxes(jnp.concatenate([nope, rope], axis=-1), 1, 2).astype(BF16)


def _prep_w_kv(w_kv_b):
    w = w_kv_b.reshape(DEPTH, KV_LORA_RANK, MLA_HEADS, MLA_NOPE_DIM + MLA_V_DIM)
    wk = w[..., :MLA_NOPE_DIM].reshape(DEPTH, KV_LORA_RANK, MLA_HEADS * MLA_NOPE_DIM)
    wv = w[..., MLA_NOPE_DIM:].reshape(DEPTH, KV_LORA_RANK, MLA_HEADS * MLA_V_DIM)
    return wk.astype(BF16), jnp.swapaxes(wv, 1, 2).astype(BF16)


def _rope_tables(s):
    pos = jnp.arange(s, dtype=F32)
    inv_freq = ROPE_THETA ** (-jnp.arange(0, MLA_ROPE_DIM, 2, dtype=F32) / MLA_ROPE_DIM)
    ang = pos[:, None] * inv_freq[None, :]
    cos, sin = jnp.cos(ang), jnp.sin(ang)
    cc = jnp.concatenate([cos, cos], axis=-1)
    ss = jnp.concatenate([-sin, sin], axis=-1)
    return jnp.tile(cc, (1, 2)), jnp.tile(ss, (1, 2)), cc.T, ss.T


def kernel(x, attn_norm_g, w_in, swa_sinks, q_a_norm_g, kv_a_norm_g, w_q_b, w_kv_b, w_out, final_norm_g):
    b, s, d = x.shape
    t = b * s
    w_row, w_col = _prep_w_in(w_in)
    w_q_p = _prep_w_q(w_q_b)
    w_k_p, w_vt_p = _prep_w_kv(w_kv_b)
    w_out_p = w_out.astype(BF16)
    cos2, sin2, cost, sint = _rope_tables(s)
    swa_bias, swa_pmask, sink_rows = _swa_tables(swa_sinks)
    g_attn = attn_norm_g[:, None, :]
    g_q = q_a_norm_g[:, None, :]
    g_kv = kv_a_norm_g[:, None, :]

    x2d = x.reshape(t, d)
    for l in range(DEPTH):
        gate, kk, qat, vat, qt, k, vt = _inproj(x2d, g_attn, w_row, w_col, g_q, g_kv, w_q_p, w_k_p, w_vt_p,
                                                cos2, sin2, cost, sint, l, b, s)
        ya = _swa_attn(sink_rows[l], swa_bias, swa_pmask, qat, kk, vat, gate, b, s)
        yb = _mla_attn(qt, k, vt, gate.reshape(b, s, GATE_W))
        final_g = final_norm_g[None] if l == DEPTH - 1 else None
        x2d = _outproj(ya.reshape(t, SWA_WIDTH), yb.reshape(t, MLA_WIDTH), x2d, w_out_p, l, final_g)
    return x2d.reshape(b, s, d)
```
